```python
import jax, jax.numpy as jnp
from jax import lax
import numpy as np

D_MODEL = 1024
BATCH = 4
SEQ = 4096
DEPTH = 2
DEC_BATCH = 128
DEC_SEQ = 1
PAST_LEN = 8192
PAGE_SIZE = 128

RET_HEADS = 4
RET_DK = 64
RET_DV = 128
RET_CHUNK = 128
ATT_HEADS = 4
KV_HEADS = 2
HEAD_DIM = 64
WINDOW = 128
ATT_BLOCK = 128
CONV_CH = 256
CONV_K = 31
FFN_DIM = 2816
EPS = 1e-6
NEG_INF = -1e30

SPLITS = (RET_HEADS * RET_DK, RET_HEADS * RET_DK, RET_HEADS * RET_DV, RET_HEADS * RET_DV,
          ATT_HEADS * HEAD_DIM, KV_HEADS * HEAD_DIM, KV_HEADS * HEAD_DIM, 2 * CONV_CH)
IN_DIM = sum(SPLITS)
MIX_DIM = RET_HEADS * RET_DV + ATT_HEADS * HEAD_DIM + CONV_CH

kernel_name = 'hymba_style_retention_swa_conformer_step'


def rms_norm(x, g):
    xf = x.astype(jnp.float32)
    y = xf * lax.rsqrt(jnp.mean(xf * xf, axis=-1, keepdims=True) + EPS)
    return (y * g.astype(jnp.float32)).astype(x.dtype)


def layer_norm(x, g, b):
    xf = x.astype(jnp.float32)
    mu = jnp.mean(xf, axis=-1, keepdims=True)
    var = jnp.mean(jnp.square(xf - mu), axis=-1, keepdims=True)
    y = (xf - mu) * lax.rsqrt(var + EPS) * g.astype(jnp.float32) + b.astype(jnp.float32)
    return y.astype(x.dtype)


def swiglu_ffn(x, wg, wu, wd):
    return (jax.nn.silu(x @ wg) * (x @ wu)) @ wd


def retention_log_decay():
    return jnp.log1p(-jnp.exp2(-5.0 - jnp.arange(RET_HEADS, dtype=jnp.float32)))


def alibi_slopes():
    return jnp.exp2(-8.0 * (jnp.arange(ATT_HEADS, dtype=jnp.float32) + 1.0) / ATT_HEADS)


def retention(q, k, v, s0):
    B, L = q.shape[:2]
    c = RET_CHUNK if L % RET_CHUNK == 0 else L
    n = L // c
    lg = retention_log_decay()
    pos = jnp.arange(c, dtype=jnp.float32)
    diff = pos[:, None] - pos[None, :]
    inner_decay = jnp.where(diff >= 0, jnp.exp(lg[:, None, None] * jnp.maximum(diff, 0.0)), 0.0)
    q_decay = jnp.exp(lg[:, None] * (pos + 1.0))
    k_decay = jnp.exp(lg[:, None] * (c - 1.0 - pos))
    chunk_decay = jnp.exp(lg * c)

    def to_chunks(t):
        return t.astype(jnp.float32).reshape(B, n, c, RET_HEADS, t.shape[-1]).transpose(1, 0, 3, 2, 4)

    qc = to_chunks(q)
    kc = to_chunks(k * (RET_DK ** -0.5))
    vc = to_chunks(v)

    def step(s, inp):
        qi, ki, vi = inp
        a = jnp.einsum('bhid,bhjd->bhij', qi, ki) * inner_decay
        o = (jnp.einsum('bhij,bhjv->bhiv', a, vi)
             + jnp.einsum('bhid,bhdv->bhiv', qi, s) * q_decay[:, :, None])
        s = s * chunk_decay[:, None, None] + jnp.einsum('bhjd,bhjv->bhdv', ki * k_decay[:, :, None], vi)
        return s, o

    s_final, o = lax.scan(step, s0.astype(jnp.float32), (qc, kc, vc))
    o = o.transpose(1, 0, 3, 2, 4).reshape(B, L, RET_HEADS, RET_DV)
    return o, s_final


def swa_attention(q, k, v, k_buf, v_buf, sinks, base_pos):
    B, L = q.shape[:2]
    P = k_buf.shape[1]
    G = ATT_HEADS // KV_HEADS
    k_all = jnp.concatenate([k_buf.astype(k.dtype), k], axis=1)
    v_all = jnp.concatenate([v_buf.astype(v.dtype), v], axis=1)
    bq = ATT_BLOCK if L % ATT_BLOCK == 0 else L
    nb = L // bq
    kidx = jnp.arange(nb)[:, None] * bq + jnp.arange(bq + P)[None, :]
    kb = jnp.take(k_all, kidx, axis=1)
    vb = jnp.take(v_all, kidx, axis=1)
    qb = q.reshape(B, nb, bq, KV_HEADS, G, HEAD_DIM)
    s = jnp.einsum('bnqhgd,bnshd->bnhgqs', qb, kb).astype(jnp.float32)
    qpos = base_pos + jnp.arange(L).reshape(nb, bq)
    kpos = base_pos - P + kidx
    dist = qpos[:, :, None] - kpos[:, None, :]
    valid = (dist >= 0) & (dist < WINDOW) & (kpos[:, None, :] >= 0)
    slopes = alibi_slopes().reshape(KV_HEADS, G)
    bias = -slopes[None, :, :, None, None] * dist[:, None, None].astype(jnp.float32)
    s = jnp.where(valid[:, None, None], s + bias, NEG_INF)
    sink = jnp.broadcast_to(sinks.astype(jnp.float32).reshape(KV_HEADS, G)[None, None, :, :, None, None],
                            s.shape[:-1] + (1,))
    p = jax.nn.softmax(jnp.concatenate([s, sink], axis=-1), axis=-1)[..., :-1]
    o = jnp.einsum('bnhgqs,bnshd->bnqhgd', p.astype(v.dtype), vb).reshape(B, L, ATT_HEADS * HEAD_DIM)
    return o, k_all[:, -WINDOW:], v_all[:, -WINDOW:]


def conformer_conv(a, gate, buf, w, b, ln_g, ln_b, pw):
    u = a * jax.nn.sigmoid(gate)
    u_all = jnp.concatenate([buf.astype(u.dtype), u], axis=1)
    y = lax.conv_general_dilated(u_all, w[:, None, :].astype(u.dtype), (1,), 'VALID',
                                 dimension_numbers=('NWC', 'WIO', 'NWC'),
                                 feature_group_count=CONV_CH) + b
    y = jax.nn.silu(layer_norm(y, ln_g, ln_b))
    return y @ pw, u_all[:, -(CONV_K - 1):]


def layer(x, ret_s, k_buf, v_buf, conv_buf, base_pos, p):
    B, L, _ = x.shape
    x = x + 0.5 * swiglu_ffn(rms_norm(x, p['ffn1_norm']), p['ffn1_wg'], p['ffn1_wu'], p['ffn1_wd'])
    h = rms_norm(x, p['mix_norm'])
    z = h @ p['w_in']
    rq, rk, rv, rg, aq, ak, av, cu = jnp.split(z, np.cumsum(SPLITS)[:-1].tolist(), axis=-1)
    o_r, ret_new = retention(rq.reshape(B, L, RET_HEADS, RET_DK), rk.reshape(B, L, RET_HEADS, RET_DK),
                             rv.reshape(B, L, RET_HEADS, RET_DV), ret_s)
    o_r = rms_norm(o_r, p['ret_norm_g']).astype(x.dtype).reshape(B, L, RET_HEADS * RET_DV) * jax.nn.silu(rg)
    q = rms_norm(aq.reshape(B, L, ATT_HEADS, HEAD_DIM), p['q_norm_g']) * (HEAD_DIM ** -0.5)
    k = rms_norm(ak.reshape(B, L, KV_HEADS, HEAD_DIM), p['k_norm_g'])
    v = av.reshape(B, L, KV_HEADS, HEAD_DIM)
    o_a, k_new, v_new = swa_attention(q, k, v, k_buf, v_buf, p['sinks'], base_pos)
    ca, cg = jnp.split(cu, 2, axis=-1)
    o_c, conv_new = conformer_conv(ca, cg, conv_buf, p['conv_w'], p['conv_b'], p['conv_ln_g'],
                                   p['conv_ln_b'], p['conv_pw'])
    y = jnp.concatenate([o_r, o_a.astype(x.dtype), o_c.astype(x.dtype)], axis=-1) @ p['w_out']
    x = x + y
    x = x + 0.5 * swiglu_ffn(rms_norm(x, p['ffn2_norm']), p['ffn2_wg'], p['ffn2_wu'], p['ffn2_wd'])
    return x, ret_new, k_new, v_new, conv_new


def setup_inputs(seed: int = 0) -> dict:
    key = jax.random.key(seed)
    ks = iter(jax.random.split(key, 32))
    f32 = jnp.float32

    def nrm(shape, scale):
        return jax.random.normal(next(ks), shape, f32) * scale

    def gain(shape):
        return 1.0 + nrm(shape, 0.1)

    swa_buf = min(WINDOW, PAST_LEN)
    return {
        'x_prompt': nrm((BATCH, SEQ, D_MODEL), 1.0),
        'x_sample': nrm((DEC_BATCH, DEC_SEQ, D_MODEL), 1.0),
        'state_ret': nrm((DEPTH, DEC_BATCH, RET_HEADS, RET_DK, RET_DV), 0.5),
        'cache_k_win': nrm((DEPTH, DEC_BATCH, swa_buf, KV_HEADS, HEAD_DIM), 1.0),
        'cache_v_win': nrm((DEPTH, DEC_BATCH, swa_buf, KV_HEADS, HEAD_DIM), 1.0),
        'state_conv': nrm((DEPTH, DEC_BATCH, CONV_K - 1, CONV_CH), 0.5),
        'ffn1_norm': gain((DEPTH, D_MODEL)),
        'ffn1_wg': nrm((DEPTH, D_MODEL, FFN_DIM), D_MODEL ** -0.5),
        'ffn1_wu': nrm((DEPTH, D_MODEL, FFN_DIM), D_MODEL ** -0.5),
        'ffn1_wd': nrm((DEPTH, FFN_DIM, D_MODEL), FFN_DIM ** -0.5),
        'mix_norm': gain((DEPTH, D_MODEL)),
        'w_in': nrm((DEPTH, D_MODEL, IN_DIM), D_MODEL ** -0.5),
        'ret_norm_g': gain((DEPTH, RET_HEADS, RET_DV)),
        'q_norm_g': gain((DEPTH, HEAD_DIM)),
        'k_norm_g': gain((DEPTH, HEAD_DIM)),
        'sinks': nrm((DEPTH, ATT_HEADS), 0.5),
        'conv_w': nrm((DEPTH, CONV_K, CONV_CH), CONV_K ** -0.5),
        'conv_b': nrm((DEPTH, CONV_CH), 0.02),
        'conv_ln_g': gain((DEPTH, CONV_CH)),
        'conv_ln_b': nrm((DEPTH, CONV_CH), 0.02),
        'conv_pw': nrm((DEPTH, CONV_CH, CONV_CH), CONV_CH ** -0.5),
        'w_out': nrm((DEPTH, MIX_DIM, D_MODEL), MIX_DIM ** -0.5),
        'ffn2_norm': gain((DEPTH, D_MODEL)),
        'ffn2_wg': nrm((DEPTH, D_MODEL, FFN_DIM), D_MODEL ** -0.5),
        'ffn2_wu': nrm((DEPTH, D_MODEL, FFN_DIM), D_MODEL ** -0.5),
        'ffn2_wd': nrm((DEPTH, FFN_DIM, D_MODEL), FFN_DIM ** -0.5),
    }


def reference(x_prompt, x_sample, state_ret, cache_k_win, cache_v_win, state_conv,
              ffn1_norm, ffn1_wg, ffn1_wu, ffn1_wd, mix_norm, w_in, ret_norm_g, q_norm_g, k_norm_g,
              sinks, conv_w, conv_b, conv_ln_g, conv_ln_b, conv_pw, w_out,
              ffn2_norm, ffn2_wg, ffn2_wu, ffn2_wd):
    B = x_prompt.shape[0]
    dt = x_prompt.dtype
    hp, hs = x_prompt, x_sample
    ret_p, ret_s, kp, ks_, vp, vs, cp, cs = [], [], [], [], [], [], [], []
    for l in range(DEPTH):
        p = dict(ffn1_norm=ffn1_norm[l], ffn1_wg=ffn1_wg[l], ffn1_wu=ffn1_wu[l], ffn1_wd=ffn1_wd[l],
                 mix_norm=mix_norm[l], w_in=w_in[l], ret_norm_g=ret_norm_g[l], q_norm_g=q_norm_g[l],
                 k_norm_g=k_norm_g[l], sinks=sinks[l], conv_w=conv_w[l], conv_b=conv_b[l],
                 conv_ln_g=conv_ln_g[l], conv_ln_b=conv_ln_b[l], conv_pw=conv_pw[l], w_out=w_out[l],
                 ffn2_norm=ffn2_norm[l], ffn2_wg=ffn2_wg[l], ffn2_wu=ffn2_wu[l], ffn2_wd=ffn2_wd[l])
        hp, r1, k1, v1, c1 = layer(
            hp,
            jnp.zeros((B, RET_HEADS, RET_DK, RET_DV), jnp.float32),
            jnp.zeros((B, WINDOW, KV_HEADS, HEAD_DIM), dt),
            jnp.zeros((B, WINDOW, KV_HEADS, HEAD_DIM), dt),
            jnp.zeros((B, CONV_K - 1, CONV_CH), dt),
            0, p)
        hs, r2, k2, v2, c2 = layer(hs, state_ret[l], cache_k_win[l], cache_v_win[l], state_conv[l],
                                   PAST_LEN, p)
        ret_p.append(r1); ret_s.append(r2)
        kp.append(k1); ks_.append(k2)
        vp.append(v1); vs.append(v2)
        cp.append(c1); cs.append(c2)
    return (hp, hs, jnp.stack(ret_p), jnp.stack(ret_s), jnp.stack(kp), jnp.stack(ks_),
            jnp.stack(vp), jnp.stack(vs), jnp.stack(cp), jnp.stack(cs))
```

```python
import functools
import math

import jax
import jax.numpy as jnp
from jax import lax
from jax.experimental import pallas as pl
from jax.experimental.pallas import tpu as pltpu

F32 = jnp.float32
BF16 = jnp.bfloat16

D_MODEL = 1024
DEPTH = 2
PAST_LEN = 8192
RET_HEADS = 4
RET_DK = 64
RET_DV = 128
RET_CHUNK = 128
ATT_HEADS = 4
KV_HEADS = 2
HEAD_DIM = 64
WINDOW = 128
CONV_CH = 256
CONV_K = 31
FFN_DIM = 2816
EPS = 1e-6
NEG_INF = -1e30

C_RQ, C_RK, C_RV, C_RG = 0, 256, 512, 1024
C_AQ, C_AK, C_AV, C_CA, C_CG = 1536, 1792, 1920, 2048, 2304
IN_DIM = 2560
MIX_DIM = 1024
M_RET, M_ATT, M_CONV = 0, 512, 768

LOG_DECAY = [math.log1p(-2.0 ** (-5 - h)) for h in range(RET_HEADS)]
SLOPES = [2.0 ** (-8.0 * (h + 1) / ATT_HEADS) for h in range(ATT_HEADS)]
ATT_PERM = (0, 2, 1, 3)

V7X_VMEM_LIMIT = 56 * 1024 * 1024
TOKEN_TILE = 512
FFN_COLS = 256
OUT_COLS = 256
SAMPLE_BLOCK = 8


def _dot(a, b):
    return jnp.dot(a, b, preferred_element_type=F32)


def _dot_nt(a, b):
    return lax.dot_general(a, b, (((1,), (1,)), ((), ())), preferred_element_type=F32)


def _dot_tn(a, b):
    return lax.dot_general(a, b, (((0,), (0,)), ((), ())), preferred_element_type=F32)


def _silu(x):
    return x * jax.nn.sigmoid(x)


def _rms(x, g):
    return x * lax.rsqrt(jnp.mean(x * x, axis=-1, keepdims=True) + EPS) * g


def _seg_rms(x, g):
    r = lax.broadcasted_iota(jnp.int32, (128, 128), 0)
    c = lax.broadcasted_iota(jnp.int32, (128, 128), 1)
    seg = jnp.where((r < 64) == (c < 64), 1.0 / HEAD_DIM, 0.0).astype(BF16)
    xx = x * x
    hi = xx.astype(BF16)
    lo = (xx - hi.astype(F32)).astype(BF16)
    ms = _dot(hi, seg) + _dot(lo, seg)
    return x * lax.rsqrt(ms + EPS) * g


def _layer_norm(y, g, b):
    mu = jnp.mean(y, axis=-1, keepdims=True)
    d = y - mu
    var = jnp.mean(d * d, axis=-1, keepdims=True)
    return d * lax.rsqrt(var + EPS) * g + b


def _ffn(x_ref, g_ref, wg_ref, wu_ref, wd_ref, out_ref, xn_scr, a_scr):
    xn_scr[...] = _rms(x_ref[...], g_ref[...]).astype(BF16)
    for c in range(FFN_DIM // FFN_COLS):
        sl = slice(c * FFN_COLS, (c + 1) * FFN_COLS)
        g = _dot(xn_scr[...], wg_ref[:, sl])
        u = _dot(xn_scr[...], wu_ref[:, sl])
        a_scr[:, sl] = (_silu(g) * u).astype(BF16)
    for c in range(D_MODEL // OUT_COLS):
        sl = slice(c * OUT_COLS, (c + 1) * OUT_COLS)
        out_ref[:, sl] = x_ref[:, sl] + 0.5 * _dot(a_scr[...], wd_ref[:, sl])


def _ffn_in_kernel(x_ref, g1_ref, wg_ref, wu_ref, wd_ref, g2_ref, win_ref,
                   x1_ref, z_ref, xn_scr, a_scr):
    _ffn(x_ref, g1_ref, wg_ref, wu_ref, wd_ref, x1_ref, xn_scr, a_scr)
    xn_scr[...] = _rms(x1_ref[...], g2_ref[...]).astype(BF16)
    for c in range(IN_DIM // OUT_COLS):
        sl = slice(c * OUT_COLS, (c + 1) * OUT_COLS)
        z_ref[:, sl] = _dot(xn_scr[...], win_ref[:, sl])


def _out_ffn_kernel(x1_ref, mix_ref, wout_ref, g_ref, wg_ref, wu_ref, wd_ref,
                    x3_ref, x2_scr, xn_scr, a_scr):
    mix = mix_ref[...].astype(BF16)
    for c in range(D_MODEL // OUT_COLS):
        sl = slice(c * OUT_COLS, (c + 1) * OUT_COLS)
        x2_scr[:, sl] = x1_ref[:, sl] + _dot(mix, wout_ref[:, sl])
    _ffn(x2_scr, g_ref, wg_ref, wu_ref, wd_ref, x3_ref, xn_scr, a_scr)


def _resident(shape):
    return pl.BlockSpec(shape, lambda i: (0,) * len(shape), pipeline_mode=pl.Buffered(1))


def _dense_params():
    return pltpu.CompilerParams(dimension_semantics=("arbitrary",),
                                vmem_limit_bytes=V7X_VMEM_LIMIT)


def _ffn_in(x, g1, wg, wu, wd, g2, win):
    rows = x.shape[0]
    tm = min(TOKEN_TILE, rows)
    row_spec = lambda w: pl.BlockSpec((tm, w), lambda i: (i, 0))
    return pl.pallas_call(
        _ffn_in_kernel,
        grid=(rows // tm,),
        in_specs=[row_spec(D_MODEL), _resident((1, D_MODEL)),
                  _resident((D_MODEL, FFN_DIM)), _resident((D_MODEL, FFN_DIM)),
                  _resident((FFN_DIM, D_MODEL)), _resident((1, D_MODEL)),
                  _resident((D_MODEL, IN_DIM))],
        out_specs=[row_spec(D_MODEL), row_spec(IN_DIM)],
        out_shape=[jax.ShapeDtypeStruct((rows, D_MODEL), F32),
                   jax.ShapeDtypeStruct((rows, IN_DIM), F32)],
        scratch_shapes=[pltpu.VMEM((tm, D_MODEL), BF16), pltpu.VMEM((tm, FFN_DIM), BF16)],
        compiler_params=_dense_params(),
        name="ffn_in",
    )(x, g1, wg, wu, wd, g2, win)


def _out_ffn(x1, mix, wout, g, wg, wu, wd):
    rows = x1.shape[0]
    tm = min(TOKEN_TILE, rows)
    row_spec = lambda w: pl.BlockSpec((tm, w), lambda i: (i, 0))
    return pl.pallas_call(
        _out_ffn_kernel,
        grid=(rows // tm,),
        in_specs=[row_spec(D_MODEL), row_spec(MIX_DIM), _resident((MIX_DIM, D_MODEL)),
                  _resident((1, D_MODEL)),
                  _resident((D_MODEL, FFN_DIM)), _resident((D_MODEL, FFN_DIM)),
                  _resident((FFN_DIM, D_MODEL))],
        out_specs=row_spec(D_MODEL),
        out_shape=jax.ShapeDtypeStruct((rows, D_MODEL), F32),
        scratch_shapes=[pltpu.VMEM((tm, D_MODEL), F32), pltpu.VMEM((tm, D_MODEL), BF16),
                        pltpu.VMEM((tm, FFN_DIM), BF16)],
        compiler_params=_dense_params(),
        name="out_ffn",
    )(x1, mix, wout, g, wg, wu, wd)


def _mix_prompt_kernel(sinks_ref, z_ref, rng_ref, qg_ref, kg_ref, cw_ref, cb_ref, lng_ref, lnb_ref,
                       pw_ref,
                       mix_ref, ret_ref, kwin_ref, vwin_ref, conv_ref,
                       s_scr, kprev, vprev, ubuf, dmask, qdec, kdec):
    b = pl.program_id(0)
    c = pl.program_id(1)
    last = pl.num_programs(1) - 1
    T = RET_CHUNK
    row_i = lax.broadcasted_iota(jnp.int32, (T, T), 0)
    col_i = lax.broadcasted_iota(jnp.int32, (T, T), 1)
    row_f = row_i.astype(F32)
    col_f = col_i.astype(F32)
    low = col_i < 64

    @pl.when((b == 0) & (c == 0))
    def _():
        for h in range(RET_HEADS):
            lg = LOG_DECAY[h]
            diff = row_f - col_f
            dmask[h] = jnp.where(diff >= 0, jnp.exp(lg * jnp.maximum(diff, 0.0)), 0.0)
            qdec[h] = jnp.exp(lg * (row_f + 1.0))
        for p in range(RET_HEADS // 2):
            lgp = jnp.where(low, LOG_DECAY[2 * p], LOG_DECAY[2 * p + 1])
            kdec[p] = jnp.exp(lgp * (T - 1.0 - row_f))

    @pl.when(c == 0)
    def _():
        s_scr[...] = jnp.zeros_like(s_scr)
        kprev[...] = jnp.zeros_like(kprev)
        vprev[...] = jnp.zeros_like(vprev)
        ubuf[0:32, :] = jnp.zeros((32, CONV_CH), F32)

    @pl.when(c > 0)
    def _():
        ubuf[0:32, :] = ubuf[T:T + 32, :]

    def zc(a, w):
        return z_ref[0, :, a:a + w]

    for p in range(RET_HEADS // 2):
        qp = zc(C_RQ + 128 * p, 128)
        kp = zc(C_RK + 128 * p, 128) * (RET_DK ** -0.5)
        vb = zc(C_RV + 256 * p, 256).astype(BF16)
        kb = kp.astype(BF16)
        s_old = s_scr[p]
        sb = s_old.astype(BF16)
        for e in range(2):
            h = 2 * p + e
            qm = jnp.where(low if e == 0 else jnp.logical_not(low), qp, 0.0).astype(BF16)
            a = _dot_nt(qm, kb) * dmask[h]
            o = _dot(a.astype(BF16), vb[:, 128 * e:128 * e + 128]) + _dot(qm, sb) * qdec[h]
            o = _rms(o, rng_ref[h:h + 1, :])
            mix_ref[0, :, M_RET + 128 * h:M_RET + 128 * h + 128] = (
                o * _silu(zc(C_RG + 128 * h, 128))).astype(BF16)
        kd = (kp * kdec[p]).astype(BF16)
        upd = _dot_tn(kd, vb)
        top = row_i < 64
        new = jnp.where(top, upd[:, 0:128], upd[:, 128:256])
        cd = jnp.where(top, math.exp(LOG_DECAY[2 * p] * T), math.exp(LOG_DECAY[2 * p + 1] * T))
        s_scr[p] = s_old * cd + new

    @pl.when(c == last)
    def _():
        ret_ref[0] = s_scr[...]

    kn = _seg_rms(zc(C_AK, 128), kg_ref[...])
    av = zc(C_AV, 128)
    knb = kn.astype(BF16)
    avb = av.astype(BF16)
    kpb = kprev[...]
    vpb = vprev[...]
    prev_lim = row_f + jnp.where(c > 0, 0.0, 1e4)
    for t in range(2):
        qt = _seg_rms(zc(C_AQ + 128 * t, 128), qg_ref[...]) * (HEAD_DIM ** -0.5)
        halves = []
        for kvh in range(KV_HEADS):
            head = 2 * kvh + t
            slope = SLOPES[head]
            sink = sinks_ref[head]
            qm = jnp.where(low if kvh == 0 else jnp.logical_not(low), qt, 0.0).astype(BF16)
            s_c = _dot_nt(qm, knb)
            s_p = _dot_nt(qm, kpb)
            s_c = jnp.where(col_f <= row_f, s_c - slope * (row_f - col_f), NEG_INF)
            s_p = jnp.where(col_f > prev_lim, s_p - slope * (row_f + float(T) - col_f), NEG_INF)
            m = jnp.maximum(jnp.maximum(jnp.max(s_c, axis=-1, keepdims=True),
                                        jnp.max(s_p, axis=-1, keepdims=True)), sink)
            e_c = jnp.exp(s_c - m)
            e_p = jnp.exp(s_p - m)
            den = (jnp.sum(e_c, axis=-1, keepdims=True) + jnp.sum(e_p, axis=-1, keepdims=True)
                   + jnp.exp(sink - m))
            inv = 1.0 / den
            halves.append(_dot((e_c * inv).astype(BF16), avb) + _dot((e_p * inv).astype(BF16), vpb))
        mix_ref[0, :, M_ATT + 128 * t:M_ATT + 128 * t + 128] = jnp.where(
            low, halves[0], halves[1]).astype(BF16)
    kprev[...] = knb
    vprev[...] = avb

    @pl.when(c == last)
    def _():
        kwin_ref[0] = kn
        vwin_ref[0] = av

    u = zc(C_CA, CONV_CH) * jax.nn.sigmoid(zc(C_CG, CONV_CH))
    ubuf[32:32 + T, :] = u
    acc = None
    for k in range(CONV_K):
        term = cw_ref[k:k + 1, :] * ubuf[2 + k:2 + k + T, :]
        acc = term if acc is None else acc + term
    y = _silu(_layer_norm(acc + cb_ref[...], lng_ref[...], lnb_ref[...]))
    mix_ref[0, :, M_CONV:M_CONV + CONV_CH] = _dot(y.astype(BF16), pw_ref[...]).astype(BF16)

    @pl.when(c == last)
    def _():
        conv_ref[0] = ubuf[T + 2:T + 32, :]


def _mix_prompt(z, sinks, rng, qg, kg, cw, cb, lng, lnb, pw):
    nb, seq, _ = z.shape
    T = RET_CHUNK
    full = lambda shape: pl.BlockSpec(shape, lambda b, c: (0,) * len(shape))
    per_seq = lambda shape: pl.BlockSpec((1,) + shape, lambda b, c: (b,) + (0,) * len(shape))
    return pl.pallas_call(
        _mix_prompt_kernel,
        grid=(nb, seq // T),
        in_specs=[pl.BlockSpec(memory_space=pltpu.SMEM),
                  pl.BlockSpec((1, T, IN_DIM), lambda b, c: (b, c, 0)),
                  full((RET_HEADS, RET_DV)), full((1, 128)), full((1, 128)),
                  full((CONV_K, CONV_CH)), full((1, CONV_CH)), full((1, CONV_CH)),
                  full((1, CONV_CH)), full((CONV_CH, CONV_CH))],
        out_specs=[pl.BlockSpec((1, T, MIX_DIM), lambda b, c: (b, c, 0)),
                   per_seq((2, 128, RET_DV)), per_seq((WINDOW, 128)), per_seq((WINDOW, 128)),
                   per_seq((CONV_K - 1, CONV_CH))],
        out_shape=[jax.ShapeDtypeStruct((nb, seq, MIX_DIM), BF16),
                   jax.ShapeDtypeStruct((nb, 2, 128, RET_DV), F32),
                   jax.ShapeDtypeStruct((nb, WINDOW, 128), F32),
                   jax.ShapeDtypeStruct((nb, WINDOW, 128), F32),
                   jax.ShapeDtypeStruct((nb, CONV_K - 1, CONV_CH), F32)],
        scratch_shapes=[pltpu.VMEM((2, 128, RET_DV), F32),
                        pltpu.VMEM((T, 128), BF16), pltpu.VMEM((T, 128), BF16),
                        pltpu.VMEM((T + 32, CONV_CH), F32),
                        pltpu.VMEM((RET_HEADS, T, T), F32), pltpu.VMEM((RET_HEADS, T, T), F32),
                        pltpu.VMEM((2, T, T), F32)],
        compiler_params=pltpu.CompilerParams(dimension_semantics=("arbitrary", "arbitrary")),
        name="mix_prompt",
    )(sinks, z, rng, qg, kg, cw, cb, lng, lnb, pw)


def _mix_sample_kernel(sinks_ref, z_ref, qkt_ref, s_ref, ck_ref, cv_ref, sc_ref,
                       rng_ref, qg_ref, kg_ref, cw_ref, cb_ref, lng_ref, lnb_ref, pw_ref,
                       mix_ref, so_ref, cko_ref, cvo_ref, sco_ref,
                       o_scr, oa_scr, y_scr):
    NB = SAMPLE_BLOCK
    P = WINDOW

    for bl in range(NB):
        for h in range(RET_HEADS):
            gamma = math.exp(LOG_DECAY[h])
            r0 = RET_DK * h
            S = s_ref[bl, r0:r0 + RET_DK, :]
            qc = qkt_ref[0, r0:r0 + RET_DK, bl:bl + 1]
            kc = qkt_ref[0, 256 + r0:256 + r0 + RET_DK, bl:bl + 1] * (RET_DK ** -0.5)
            v = z_ref[bl:bl + 1, C_RV + 128 * h:C_RV + 128 * h + 128]
            qk = jnp.sum(qc * kc, axis=0, keepdims=True)
            o_scr[bl:bl + 1, 128 * h:128 * h + 128] = (
                gamma * jnp.sum(qc * S, axis=0, keepdims=True) + qk * v)
            so_ref[bl, r0:r0 + RET_DK, :] = gamma * S + kc * v
    for h in range(RET_HEADS):
        o = _rms(o_scr[:, 128 * h:128 * h + 128], rng_ref[h:h + 1, :])
        mix_ref[:, M_RET + 128 * h:M_RET + 128 * h + 128] = (
            o * _silu(z_ref[:, C_RG + 128 * h:C_RG + 128 * h + 128]))

    q_tiles = [_seg_rms(z_ref[:, C_AQ + 128 * t:C_AQ + 128 * t + 128], qg_ref[...])
               * (HEAD_DIM ** -0.5) for t in range(2)]
    kn = _seg_rms(z_ref[:, C_AK:C_AK + 128], kg_ref[...])
    vn = z_ref[:, C_AV:C_AV + 128]
    rid = lax.broadcasted_iota(jnp.int32, (8, 128), 0)
    lane = lax.broadcasted_iota(jnp.int32, (8, 128), 1)
    sel = ((rid % 2 == 0) == (lane < 64)) & (rid < ATT_HEADS)
    rcol = lax.broadcasted_iota(jnp.int32, (8, 1), 0)
    heads = [2 * (r % 2) + r // 2 for r in range(ATT_HEADS)]
    slope_col = jnp.zeros((8, 1), F32)
    sink_col = jnp.zeros((8, 1), F32)
    for r, hd in enumerate(heads):
        slope_col = jnp.where(rcol == r, SLOPES[hd], slope_col)
        sink_col = jnp.where(rcol == r, sinks_ref[hd], sink_col)
    dist = float(P) - lane.astype(F32)
    key_ok = (dist < float(WINDOW)) & (lane + (PAST_LEN - P) >= 0)
    for bl in range(NB):
        K = ck_ref[bl]
        V = cv_ref[bl]
        qrows = jnp.where(rid < 2, q_tiles[0][bl:bl + 1, :], q_tiles[1][bl:bl + 1, :])
        q4 = jnp.where(sel, qrows, 0.0)
        s = _dot_nt(q4.astype(BF16), K.astype(BF16))
        s = jnp.where(key_ok, s - slope_col * dist, NEG_INF)
        s_new = jnp.sum(q4 * kn[bl:bl + 1, :], axis=-1, keepdims=True)
        m = jnp.maximum(jnp.maximum(jnp.max(s, axis=-1, keepdims=True), s_new), sink_col)
        e = jnp.exp(s - m)
        e_new = jnp.exp(s_new - m)
        inv = 1.0 / (jnp.sum(e, axis=-1, keepdims=True) + e_new + jnp.exp(sink_col - m))
        o = _dot((e * inv).astype(BF16), V.astype(BF16)) + (e_new * inv) * vn[bl:bl + 1, :]
        for t in range(2):
            oa_scr[bl:bl + 1, 128 * t:128 * t + 128] = jnp.where(
                lane[0:1, :] < 64, o[2 * t:2 * t + 1, :], o[2 * t + 1:2 * t + 2, :])
        cko_ref[bl, 0:P - 1, :] = ck_ref[bl, 1:P, :]
        cko_ref[bl, P - 1:P, :] = kn[bl:bl + 1, :]
        cvo_ref[bl, 0:P - 1, :] = cv_ref[bl, 1:P, :]
        cvo_ref[bl, P - 1:P, :] = vn[bl:bl + 1, :]
    mix_ref[:, M_ATT:M_ATT + 256] = oa_scr[...]

    u = z_ref[:, C_CA:C_CA + CONV_CH] * jax.nn.sigmoid(z_ref[:, C_CG:C_CG + CONV_CH])
    KT = CONV_K - 1
    for bl in range(NB):
        y_scr[bl:bl + 1, :] = (jnp.sum(cw_ref[0:KT, :] * sc_ref[bl], axis=0, keepdims=True)
                               + cw_ref[KT:KT + 1, :] * u[bl:bl + 1, :])
        sco_ref[bl, 0:KT - 1, :] = sc_ref[bl, 1:KT, :]
        sco_ref[bl, KT - 1:KT, :] = u[bl:bl + 1, :]
    y = _silu(_layer_norm(y_scr[...] + cb_ref[...], lng_ref[...], lnb_ref[...]))
    mix_ref[:, M_CONV:M_CONV + CONV_CH] = _dot(y.astype(BF16), pw_ref[...])


def _mix_sample(z, qkt, s, ck, cv, sc, sinks, rng, qg, kg, cw, cb, lng, lnb, pw):
    nseq = z.shape[0]
    NB = SAMPLE_BLOCK
    full = lambda shape: pl.BlockSpec(shape, lambda i: (0,) * len(shape))
    blk = lambda shape: pl.BlockSpec((NB,) + shape, lambda i: (i,) + (0,) * len(shape))
    return pl.pallas_call(
        _mix_sample_kernel,
        grid=(nseq // NB,),
        in_specs=[pl.BlockSpec(memory_space=pltpu.SMEM),
                  blk((IN_DIM,)), pl.BlockSpec((1, 512, NB), lambda i: (i, 0, 0)),
                  blk((256, RET_DV)), blk((WINDOW, 128)), blk((WINDOW, 128)),
                  blk((CONV_K - 1, CONV_CH)),
                  full((RET_HEADS, RET_DV)), full((1, 128)), full((1, 128)),
                  full((CONV_K, CONV_CH)), full((1, CONV_CH)), full((1, CONV_CH)),
                  full((1, CONV_CH)), full((CONV_CH, CONV_CH))],
        out_specs=[blk((MIX_DIM,)), blk((256, RET_DV)), blk((WINDOW, 128)), blk((WINDOW, 128)),
                   blk((CONV_K - 1, CONV_CH))],
        out_shape=[jax.ShapeDtypeStruct((nseq, MIX_DIM), F32),
                   jax.ShapeDtypeStruct(s.shape, F32),
                   jax.ShapeDtypeStruct(ck.shape, F32),
                   jax.ShapeDtypeStruct(cv.shape, F32),
                   jax.ShapeDtypeStruct(sc.shape, F32)],
        scratch_shapes=[pltpu.VMEM((NB, 512), F32), pltpu.VMEM((NB, 256), F32),
                        pltpu.VMEM((NB, CONV_CH), F32)],
        compiler_params=pltpu.CompilerParams(dimension_semantics=("arbitrary",)),
        name="mix_sample",
    )(sinks, z, qkt, s, ck, cv, sc, rng, qg, kg, cw, cb, lng, lnb, pw)


def _perm_heads(w, axis, start):
    idx = list(range(w.shape[axis]))
    blockwise = [start + HEAD_DIM * h + d for h in ATT_PERM for d in range(HEAD_DIM)]
    idx[start:start + ATT_HEADS * HEAD_DIM] = blockwise
    return jnp.take(w, jnp.asarray(idx, jnp.int32), axis=axis)


def _layer_params(l, ffn1_norm, ffn1_wg, ffn1_wu, ffn1_wd, mix_norm, w_in, ret_norm_g, q_norm_g,
                  k_norm_g, sinks, conv_w, conv_b, conv_ln_g, conv_ln_b, conv_pw, w_out,
                  ffn2_norm, ffn2_wg, ffn2_wu, ffn2_wd):
    row = lambda v: v.reshape(1, -1)
    return dict(
        ffn1=(row(ffn1_norm[l]), ffn1_wg[l].astype(BF16), ffn1_wu[l].astype(BF16),
              ffn1_wd[l].astype(BF16)),
        mix_norm=row(mix_norm[l]),
        w_in=_perm_heads(w_in[l], 1, C_AQ).astype(BF16),
        w_out=_perm_heads(w_out[l], 0, M_ATT).astype(BF16),
        ffn2=(row(ffn2_norm[l]), ffn2_wg[l].astype(BF16), ffn2_wu[l].astype(BF16),
              ffn2_wd[l].astype(BF16)),
        mixer=(sinks[l], ret_norm_g[l], jnp.tile(row(q_norm_g[l]), (1, 2)),
               jnp.tile(row(k_norm_g[l]), (1, 2)), conv_w[l], row(conv_b[l]),
               row(conv_ln_g[l]), row(conv_ln_b[l]), conv_pw[l].astype(BF16)),
    )


def kernel(x_prompt, x_sample, state_ret, cache_k_win, cache_v_win, state_conv, ffn1_norm, ffn1_wg, ffn1_wu, ffn1_wd, mix_norm, w_in, ret_norm_g, q_norm_g, k_norm_g, sinks, conv_w, conv_b, conv_ln_g, conv_ln_b, conv_pw, w_out, ffn2_norm, ffn2_wg, ffn2_wu, ffn2_wd):
    nb, seq, _ = x_prompt.shape
    ns = x_sample.shape[0]
    assert x_sample.shape[1] == 1 and seq % RET_CHUNK == 0 and ns % SAMPLE_BLOCK == 0
    assert cache_k_win.shape[2] == WINDOW
    weights = (ffn1_norm, ffn1_wg, ffn1_wu, ffn1_wd, mix_norm, w_in, ret_norm_g, q_norm_g,
               k_norm_g, sinks, conv_w, conv_b, conv_ln_g, conv_ln_b, conv_pw, w_out,
               ffn2_norm, ffn2_wg, ffn2_wu, ffn2_wd)
    hp = x_prompt.reshape(nb * seq, D_MODEL)
    hs = x_sample.reshape(ns, D_MODEL)
    outs = [[] for _ in range(8)]
    for l in range(DEPTH):
        p = _layer_params(l, *weights)
        hp1, zp = _ffn_in(hp, *p["ffn1"], p["mix_norm"], p["w_in"])
        mixp, r1, k1, v1, c1 = _mix_prompt(zp.reshape(nb, seq, IN_DIM), *p["mixer"])
        hp = _out_ffn(hp1, mixp.reshape(nb * seq, MIX_DIM), p["w_out"], *p["ffn2"])
        hs1, zs = _ffn_in(hs, *p["ffn1"], p["mix_norm"], p["w_in"])
        qkt = zs[:, 0:512].T.reshape(512, ns // SAMPLE_BLOCK, SAMPLE_BLOCK).transpose(1, 0, 2)
        mixs, r2, k2, v2, c2 = _mix_sample(
            zs, qkt, state_ret[l].reshape(ns, 256, RET_DV),
            cache_k_win[l].reshape(ns, WINDOW, 128), cache_v_win[l].reshape(ns, WINDOW, 128),
            state_conv[l], *p["mixer"])
        hs = _out_ffn(hs1, mixs, p["w_out"], *p["ffn2"])
        for lst, val in zip(outs, (
                r1.reshape(nb, RET_HEADS, RET_DK, RET_DV),
                r2.reshape(ns, RET_HEADS, RET_DK, RET_DV),
                k1.reshape(nb, WINDOW, KV_HEADS, HEAD_DIM),
                k2.reshape(ns, WINDOW, KV_HEADS, HEAD_DIM),
                v1.reshape(nb, WINDOW, KV_HEADS, HEAD_DIM),
                v2.reshape(ns, WINDOW, KV_HEADS, HEAD_DIM),
                c1, c2)):
            lst.append(val)
    return (hp.reshape(nb, seq, D_MODEL), hs.reshape(ns, 1, D_MODEL)) + tuple(
        jnp.stack(o) for o in outs)
```

```python
import functools
import math

import jax
import jax.numpy as jnp
from jax import lax
from jax.experimental import pallas as pl
from jax.experimental.pallas import tpu as pltpu

F32 = jnp.float32
BF16 = jnp.bfloat16

D_MODEL = 1024
DEPTH = 2
PAST_LEN = 8192
RET_HEADS = 4
RET_DK = 64
RET_DV = 128
RET_CHUNK = 128
ATT_HEADS = 4
KV_HEADS = 2
HEAD_DIM = 64
WINDOW = 128
CONV_CH = 256
CONV_K = 31
FFN_DIM = 2816
EPS = 1e-6
NEG_INF = -1e30

C_RQ, C_RK, C_RV, C_RG = 0, 256, 512, 1024
C_AQ, C_AK, C_AV, C_CA, C_CG = 1536, 1792, 1920, 2048, 2304
IN_DIM = 2560
MIX_DIM = 1024
M_RET, M_ATT, M_CONV = 0, 512, 768

LOG_DECAY = [math.log1p(-2.0 ** (-5 - h)) for h in range(RET_HEADS)]
SLOPES = [2.0 ** (-8.0 * (h + 1) / ATT_HEADS) for h in range(ATT_HEADS)]

V7X_VMEM_LIMIT = 60 * 1024 * 1024
TOKEN_TILE = 512
FFN_COLS = 256
OUT_COLS = 256
STAGE_ROWS = 128
MIX_TILE = 512
SAMPLE_BLOCK = 8
CONV_ROWS = 64
CONV_PAD = 32


def _dot(a, b):
    return jnp.dot(a, b, preferred_element_type=F32)


def _dot_nt(a, b):
    return lax.dot_general(a, b, (((1,), (1,)), ((), ())), preferred_element_type=F32)


def _dot_tn(a, b):
    return lax.dot_general(a, b, (((0,), (0,)), ((), ())), preferred_element_type=F32)


def _silu(x):
    return x * jax.nn.sigmoid(x)


def _rms(x, g):
    return x * lax.rsqrt(jnp.mean(x * x, axis=-1, keepdims=True) + EPS) * g


def _seg_rms(x, g):
    r = lax.broadcasted_iota(jnp.int32, (128, 128), 0)
    c = lax.broadcasted_iota(jnp.int32, (128, 128), 1)
    seg = jnp.where((r < 64) == (c < 64), 1.0 / HEAD_DIM, 0.0).astype(BF16)
    xx = x * x
    hi = xx.astype(BF16)
    lo = (xx - hi.astype(F32)).astype(BF16)
    ms = _dot(hi, seg) + _dot(lo, seg)
    return x * lax.rsqrt(ms + EPS) * g


def _layer_norm(y, g, b):
    mu = jnp.mean(y, axis=-1, keepdims=True)
    d = y - mu
    var = jnp.mean(d * d, axis=-1, keepdims=True)
    return d * lax.rsqrt(var + EPS) * g + b


def _stage_copy(src, l, j, stage, sem, slot, width):
    return pltpu.make_async_copy(src.at[l, pl.ds(j * STAGE_ROWS, STAGE_ROWS), :],
                                 stage.at[slot, :, pl.ds(0, width)], sem.at[slot])


def _swap_inner_heads(t0, t1):
    low = lax.broadcasted_iota(jnp.int32, t0.shape, 1) < 64
    return (jnp.where(low, t0, pltpu.roll(t1, 64, 1)), jnp.where(low, pltpu.roll(t0, 64, 1), t1))


def _stage_weight(src, l, dst, stage, sem, permute_q_cols=False):
    rows, width = dst.shape
    n = rows // STAGE_ROWS
    _stage_copy(src, l, 0, stage, sem, 0, width).start()

    def body(j, carry):
        slot = lax.rem(j, 2)

        @pl.when(j + 1 < n)
        def _():
            _stage_copy(src, l, j + 1, stage, sem, 1 - slot, width).start()

        _stage_copy(src, l, j, stage, sem, slot, width).wait()
        r0 = pl.multiple_of(j * STAGE_ROWS, STAGE_ROWS)
        dst[pl.ds(r0, STAGE_ROWS), :] = stage[slot, :, 0:width].astype(BF16)
        if permute_q_cols:
            t0, t1 = _swap_inner_heads(stage[slot, :, C_AQ:C_AQ + 128],
                                       stage[slot, :, C_AQ + 128:C_AQ + 256])
            dst[pl.ds(r0, STAGE_ROWS), C_AQ:C_AQ + 128] = t0.astype(BF16)
            dst[pl.ds(r0, STAGE_ROWS), C_AQ + 128:C_AQ + 256] = t1.astype(BF16)
        return carry

    lax.fori_loop(0, n, body, 0)


def _stage_w_out(src, l, dst, stage, sem):
    rows, width = dst.shape
    n = rows // STAGE_ROWS
    _stage_copy(src, l, 0, stage, sem, 0, width).start()
    for j in range(n):
        slot = j % 2
        if j + 1 < n:
            _stage_copy(src, l, j + 1, stage, sem, 1 - slot, width).start()
        _stage_copy(src, l, j, stage, sem, slot, width).wait()
        r0 = j * STAGE_ROWS
        if r0 == M_ATT:
            dst[M_ATT:M_ATT + 64, :] = stage[slot, 0:64, 0:width].astype(BF16)
            dst[M_ATT + 128:M_ATT + 192, :] = stage[slot, 64:128, 0:width].astype(BF16)
        elif r0 == M_ATT + 128:
            dst[M_ATT + 64:M_ATT + 128, :] = stage[slot, 0:64, 0:width].astype(BF16)
            dst[M_ATT + 192:M_ATT + 256, :] = stage[slot, 64:128, 0:width].astype(BF16)
        else:
            dst[r0:r0 + STAGE_ROWS, :] = stage[slot, :, 0:width].astype(BF16)


def _ffn(x_ref, g, wg_ref, wu_ref, wd_ref, out_ref, xn_scr, a_scr):
    xn_scr[...] = _rms(x_ref[...], g).astype(BF16)
    for c in range(FFN_DIM // FFN_COLS):
        sl = slice(c * FFN_COLS, (c + 1) * FFN_COLS)
        gate = _dot(xn_scr[...], wg_ref[:, sl])
        up = _dot(xn_scr[...], wu_ref[:, sl])
        a_scr[:, sl] = (_silu(gate) * up).astype(BF16)
    for c in range(D_MODEL // OUT_COLS):
        sl = slice(c * OUT_COLS, (c + 1) * OUT_COLS)
        out_ref[:, sl] = x_ref[:, sl] + 0.5 * _dot(a_scr[...], wd_ref[:, sl])


def _ffn_in_kernel(l, n_tiles, xp_ref, xs_ref, g1_ref, g2_ref, wg_hbm, wu_hbm, wd_hbm, win_hbm,
                   x1p_ref, zp_ref, x1s_ref, zs_ref,
                   wg_b, wu_b, wd_b, win_b, stage, sem, xn_scr, a_scr):
    i = pl.program_id(0)

    @pl.when(i == 0)
    def _():
        _stage_weight(wg_hbm, l, wg_b, stage, sem)
        _stage_weight(wu_hbm, l, wu_b, stage, sem)
        _stage_weight(wd_hbm, l, wd_b, stage, sem)
        _stage_weight(win_hbm, l, win_b, stage, sem, permute_q_cols=True)

    def body(x_ref, x1_ref, z_ref, xn, a):
        _ffn(x_ref, g1_ref[l:l + 1, :], wg_b, wu_b, wd_b, x1_ref, xn, a)
        xn[...] = _rms(x1_ref[...], g2_ref[l:l + 1, :]).astype(BF16)
        for c in range(IN_DIM // OUT_COLS):
            sl = slice(c * OUT_COLS, (c + 1) * OUT_COLS)
            z_ref[:, sl] = _dot(xn[...], win_b[:, sl])

    @pl.when(i < n_tiles)
    def _():
        body(xp_ref, x1p_ref, zp_ref, xn_scr, a_scr)

    @pl.when(i == n_tiles)
    def _():
        ns = xs_ref.shape[0]
        body(xs_ref, x1s_ref, zs_ref, xn_scr.at[0:ns], a_scr.at[0:ns])


def _out_ffn_kernel(l, n_tiles, x1p_ref, mixp_ref, x1s_ref, mixs_ref, g_ref,
                    wout_hbm, wg_hbm, wu_hbm, wd_hbm,
                    x3p_ref, x3s_ref,
                    wout_b, wg_b, wu_b, wd_b, stage, sem, x2_scr, xn_scr, a_scr):
    i = pl.program_id(0)

    @pl.when(i == 0)
    def _():
        _stage_w_out(wout_hbm, l, wout_b, stage, sem)
        _stage_weight(wg_hbm, l, wg_b, stage, sem)
        _stage_weight(wu_hbm, l, wu_b, stage, sem)
        _stage_weight(wd_hbm, l, wd_b, stage, sem)

    def body(x1_ref, mix_ref, x3_ref, x2, xn, a):
        mix = mix_ref[...].astype(BF16)
        for c in range(D_MODEL // OUT_COLS):
            sl = slice(c * OUT_COLS, (c + 1) * OUT_COLS)
            x2[:, sl] = x1_ref[:, sl] + _dot(mix, wout_b[:, sl])
        _ffn(x2, g_ref[l:l + 1, :], wg_b, wu_b, wd_b, x3_ref, xn, a)

    @pl.when(i < n_tiles)
    def _():
        body(x1p_ref, mixp_ref, x3p_ref, x2_scr, xn_scr, a_scr)

    @pl.when(i == n_tiles)
    def _():
        ns = x1s_ref.shape[0]
        body(x1s_ref, mixs_ref, x3s_ref, x2_scr.at[0:ns], xn_scr.at[0:ns], a_scr.at[0:ns])


def _dense_specs(n_tiles, ns):
    tile = lambda w: pl.BlockSpec((TOKEN_TILE, w), lambda i: (jnp.minimum(i, n_tiles - 1), 0))
    once = lambda w: pl.BlockSpec((ns, w), lambda i: (0, 0), pipeline_mode=pl.Buffered(1))
    whole = lambda shape: pl.BlockSpec(shape, lambda i: (0,) * len(shape))
    hbm = pl.BlockSpec(memory_space=pl.ANY)
    return tile, once, whole, hbm


def _dense_params():
    return pltpu.CompilerParams(dimension_semantics=("arbitrary",),
                                vmem_limit_bytes=V7X_VMEM_LIMIT)


def _stage_scratch():
    return [pltpu.VMEM((2, STAGE_ROWS, FFN_DIM), F32), pltpu.SemaphoreType.DMA((2,))]


def _ffn_in(l, xp, xs, g1, g2, wg, wu, wd, win):
    rows, ns = xp.shape[0], xs.shape[0]
    n_tiles = rows // TOKEN_TILE
    tile, once, whole, hbm = _dense_specs(n_tiles, ns)
    return pl.pallas_call(
        functools.partial(_ffn_in_kernel, l, n_tiles),
        grid=(n_tiles + 1,),
        in_specs=[tile(D_MODEL), once(D_MODEL), whole(g1.shape), whole(g2.shape), hbm, hbm, hbm, hbm],
        out_specs=[tile(D_MODEL), tile(IN_DIM), once(D_MODEL), once(IN_DIM)],
        out_shape=[jax.ShapeDtypeStruct((rows, D_MODEL), F32),
                   jax.ShapeDtypeStruct((rows, IN_DIM), F32),
                   jax.ShapeDtypeStruct((ns, D_MODEL), F32),
                   jax.ShapeDtypeStruct((ns, IN_DIM), F32)],
        scratch_shapes=[pltpu.VMEM((D_MODEL, FFN_DIM), BF16), pltpu.VMEM((D_MODEL, FFN_DIM), BF16),
                        pltpu.VMEM((FFN_DIM, D_MODEL), BF16), pltpu.VMEM((D_MODEL, IN_DIM), BF16),
                        *_stage_scratch(),
                        pltpu.VMEM((TOKEN_TILE, D_MODEL), BF16),
                        pltpu.VMEM((TOKEN_TILE, FFN_DIM), BF16)],
        compiler_params=_dense_params(),
        name="ffn_in",
    )(xp, xs, g1, g2, wg, wu, wd, win)


def _out_ffn(l, x1p, mixp, x1s, mixs, g, wout, wg, wu, wd):
    rows, ns = x1p.shape[0], x1s.shape[0]
    n_tiles = rows // TOKEN_TILE
    tile, once, whole, hbm = _dense_specs(n_tiles, ns)
    return pl.pallas_call(
        functools.partial(_out_ffn_kernel, l, n_tiles),
        grid=(n_tiles + 1,),
        in_specs=[tile(D_MODEL), tile(MIX_DIM), once(D_MODEL), once(MIX_DIM), whole(g.shape),
                  hbm, hbm, hbm, hbm],
        out_specs=[tile(D_MODEL), once(D_MODEL)],
        out_shape=[jax.ShapeDtypeStruct((rows, D_MODEL), F32),
                   jax.ShapeDtypeStruct((ns, D_MODEL), F32)],
        scratch_shapes=[pltpu.VMEM((MIX_DIM, D_MODEL), BF16),
                        pltpu.VMEM((D_MODEL, FFN_DIM), BF16), pltpu.VMEM((D_MODEL, FFN_DIM), BF16),
                        pltpu.VMEM((FFN_DIM, D_MODEL), BF16),
                        *_stage_scratch(),
                        pltpu.VMEM((TOKEN_TILE, D_MODEL), F32),
                        pltpu.VMEM((TOKEN_TILE, D_MODEL), BF16),
                        pltpu.VMEM((TOKEN_TILE, FFN_DIM), BF16)],
        compiler_params=_dense_params(),
        name="out_ffn",
    )(x1p, mixp, x1s, mixs, g, wout, wg, wu, wd)


def _mix_prompt_kernel(l, sinks_ref, z_ref, rng_ref, qg_ref, kg_ref, cw_ref, cb_ref, lng_ref,
                       lnb_ref, pw_ref,
                       mix_ref, ret_ref, kwin_ref, vwin_ref, conv_ref,
                       s_scr, kbuf, vbuf, ubuf, ush, dmask, qdec, kdec, abias, pw_b):
    b = pl.program_id(0)
    c = pl.program_id(1)
    last = pl.num_programs(1) - 1
    T = RET_CHUNK
    NCH = MIX_TILE // T
    row_i = lax.broadcasted_iota(jnp.int32, (T, T), 0)
    col_i = lax.broadcasted_iota(jnp.int32, (T, T), 1)
    low = col_i < 64

    @pl.when((b == 0) & (c == 0))
    def _():
        row_f = row_i.astype(F32)
        col_f = col_i.astype(F32)
        for h in range(RET_HEADS):
            lg = LOG_DECAY[h]
            diff = row_f - col_f
            dmask[h] = jnp.where(diff >= 0, jnp.exp(lg * jnp.maximum(diff, 0.0)), 0.0)
            qdec[h] = jnp.exp(lg * (row_f + 1.0))
        for p in range(RET_HEADS // 2):
            lgp = jnp.where(low, LOG_DECAY[2 * p], LOG_DECAY[2 * p + 1])
            kdec[p] = jnp.exp(lgp * (T - 1.0 - row_f))
        qpos = lax.broadcasted_iota(jnp.int32, (T, 2 * T), 0) + T
        kpos = lax.broadcasted_iota(jnp.int32, (T, 2 * T), 1)
        dist = (qpos - kpos).astype(F32)
        for h in range(ATT_HEADS):
            abias[h] = jnp.where((dist >= 0) & (dist < WINDOW), -SLOPES[h] * dist, NEG_INF)
        pw_b[...] = pw_ref[l].astype(BF16)

    @pl.when(c == 0)
    def _():
        s_scr[...] = jnp.zeros_like(s_scr)
        kbuf[0:T, :] = jnp.zeros((T, 128), BF16)
        vbuf[0:T, :] = jnp.zeros((T, 128), BF16)
        ubuf[0:CONV_PAD, :] = jnp.zeros((CONV_PAD, CONV_CH), F32)

    @pl.when(c > 0)
    def _():
        kbuf[0:T, :] = kbuf[MIX_TILE:MIX_TILE + T, :]
        vbuf[0:T, :] = vbuf[MIX_TILE:MIX_TILE + T, :]
        ubuf[0:CONV_PAD, :] = ubuf[MIX_TILE:MIX_TILE + CONV_PAD, :]

    def zc(r0, nrows, a, w):
        return z_ref[0, r0:r0 + nrows, a:a + w]

    for p in range(RET_HEADS // 2):
        s_cur = s_scr[p]
        for j in range(NCH):
            r0 = T * j
            qp = zc(r0, T, C_RQ + 128 * p, 128)
            kp = zc(r0, T, C_RK + 128 * p, 128) * (RET_DK ** -0.5)
            vb = zc(r0, T, C_RV + 256 * p, 256).astype(BF16)
            kb = kp.astype(BF16)
            sb = s_cur.astype(BF16)
            for e in range(2):
                h = 2 * p + e
                qm = jnp.where(low if e == 0 else jnp.logical_not(low), qp, 0.0).astype(BF16)
                a = _dot_nt(qm, kb) * dmask[h]
                o = _dot(a.astype(BF16), vb[:, 128 * e:128 * e + 128]) + _dot(qm, sb) * qdec[h]
                o = _rms(o, rng_ref[l, h:h + 1, :])
                mix_ref[0, r0:r0 + T, M_RET + 128 * h:M_RET + 128 * h + 128] = (
                    o * _silu(zc(r0, T, C_RG + 128 * h, 128))).astype(BF16)
            kd = (kp * kdec[p]).astype(BF16)
            upd = _dot_tn(kd, vb)
            top = row_i < 64
            new = jnp.where(top, upd[:, 0:128], upd[:, 128:256])
            cd = jnp.where(top, math.exp(LOG_DECAY[2 * p] * T), math.exp(LOG_DECAY[2 * p + 1] * T))
            s_cur = s_cur * cd + new
        s_scr[p] = s_cur

    @pl.when(c == last)
    def _():
        ret_ref[0] = s_scr[...]

    kn = _seg_rms(zc(0, MIX_TILE, C_AK, 128), kg_ref[l:l + 1, :])
    kbuf[T:T + MIX_TILE, :] = kn.astype(BF16)
    vbuf[T:T + MIX_TILE, :] = zc(0, MIX_TILE, C_AV, 128).astype(BF16)
    no_prev = lax.broadcasted_iota(jnp.int32, (T, 2 * T), 1) < jnp.where(c == 0, T, 0)
    for t in range(2):
        qt = _seg_rms(zc(0, MIX_TILE, C_AQ + 128 * t, 128), qg_ref[l:l + 1, :]) * (HEAD_DIM ** -0.5)
        for j in range(NCH):
            r0 = T * j
            kcat = kbuf[r0:r0 + 2 * T, :]
            vcat = vbuf[r0:r0 + 2 * T, :]
            qj = qt[r0:r0 + T, :]
            halves = []
            for kvh in range(KV_HEADS):
                head = 2 * kvh + t
                sink = sinks_ref[l, head]
                qm = jnp.where(low if kvh == 0 else jnp.logical_not(low), qj, 0.0).astype(BF16)
                bias = abias[head]
                if j == 0:
                    bias = jnp.where(no_prev, NEG_INF, bias)
                s = _dot_nt(qm, kcat) + bias
                m = jnp.maximum(jnp.max(s, axis=-1, keepdims=True), sink)
                e = jnp.exp(s - m)
                inv = 1.0 / (jnp.sum(e, axis=-1, keepdims=True) + jnp.exp(sink - m))
                halves.append(_dot((e * inv).astype(BF16), vcat))
            mix_ref[0, r0:r0 + T, M_ATT + 128 * t:M_ATT + 128 * t + 128] = jnp.where(
                low, halves[0], halves[1]).astype(BF16)

    @pl.when(c == last)
    def _():
        kwin_ref[0] = kn[MIX_TILE - T:MIX_TILE, :].T
        vwin_ref[0] = zc(MIX_TILE - T, T, C_AV, 128).T

    ubuf[CONV_PAD:CONV_PAD + MIX_TILE, :] = (
        zc(0, MIX_TILE, C_CA, CONV_CH) * jax.nn.sigmoid(zc(0, MIX_TILE, C_CG, CONV_CH)))
    span = ush.shape[1]
    for r in range(1, 8):
        ush[r - 1] = ubuf[r:r + span, :]
    for rb in range(MIX_TILE // CONV_ROWS):
        acc = None
        for k in range(CONV_K):
            off = CONV_PAD - (CONV_K - 1) + k
            a8, r = off // 8 * 8, off % 8
            rows = slice(a8 + rb * CONV_ROWS, a8 + (rb + 1) * CONV_ROWS)
            src = ubuf[rows, :] if r == 0 else ush[r - 1, rows, :]
            term = cw_ref[l, k:k + 1, :] * src
            acc = term if acc is None else acc + term
        y = _silu(_layer_norm(acc + cb_ref[l:l + 1, :], lng_ref[l:l + 1, :], lnb_ref[l:l + 1, :]))
        mix_ref[0, rb * CONV_ROWS:(rb + 1) * CONV_ROWS, M_CONV:M_CONV + CONV_CH] = _dot(
            y.astype(BF16), pw_b[...]).astype(BF16)

    @pl.when(c == last)
    def _():
        conv_ref[0] = ubuf[CONV_PAD + MIX_TILE - (CONV_K - 1):CONV_PAD + MIX_TILE, :]


def _mix_prompt(l, z, sinks, rng, qg2, kg2, cw, cb, lng, lnb, pw):
    nb, seq, _ = z.shape
    T = RET_CHUNK
    whole = lambda a: pl.BlockSpec(a.shape, lambda b, c: (0,) * a.ndim)
    per_seq = lambda shape: pl.BlockSpec((1,) + shape, lambda b, c: (b,) + (0,) * len(shape))
    span = MIX_TILE + CONV_PAD - 8
    return pl.pallas_call(
        functools.partial(_mix_prompt_kernel, l),
        grid=(nb, seq // MIX_TILE),
        in_specs=[pl.BlockSpec(memory_space=pltpu.SMEM),
                  pl.BlockSpec((1, MIX_TILE, IN_DIM), lambda b, c: (b, c, 0)),
                  whole(rng), whole(qg2), whole(kg2), whole(cw), whole(cb), whole(lng), whole(lnb),
                  whole(pw)],
        out_specs=[pl.BlockSpec((1, MIX_TILE, MIX_DIM), lambda b, c: (b, c, 0)),
                   per_seq((2, 128, RET_DV)), per_seq((128, WINDOW)), per_seq((128, WINDOW)),
                   per_seq((CONV_K - 1, CONV_CH))],
        out_shape=[jax.ShapeDtypeStruct((nb, seq, MIX_DIM), BF16),
                   jax.ShapeDtypeStruct((nb, 2, 128, RET_DV), F32),
                   jax.ShapeDtypeStruct((nb, 128, WINDOW), F32),
                   jax.ShapeDtypeStruct((nb, 128, WINDOW), F32),
                   jax.ShapeDtypeStruct((nb, CONV_K - 1, CONV_CH), F32)],
        scratch_shapes=[pltpu.VMEM((2, 128, RET_DV), F32),
                        pltpu.VMEM((T + MIX_TILE, 128), BF16), pltpu.VMEM((T + MIX_TILE, 128), BF16),
                        pltpu.VMEM((CONV_PAD + MIX_TILE, CONV_CH), F32),
                        pltpu.VMEM((7, span, CONV_CH), F32),
                        pltpu.VMEM((RET_HEADS, T, T), F32), pltpu.VMEM((RET_HEADS, T, T), F32),
                        pltpu.VMEM((2, T, T), F32), pltpu.VMEM((ATT_HEADS, T, 2 * T), F32),
                        pltpu.VMEM((CONV_CH, CONV_CH), BF16)],
        compiler_params=pltpu.CompilerParams(dimension_semantics=("arbitrary", "arbitrary"),
                                             vmem_limit_bytes=V7X_VMEM_LIMIT),
        name="mix_prompt",
    )(sinks, z, rng, qg2, kg2, cw, cb, lng, lnb, pw)


def _mix_sample_kernel(l, sinks_ref, z_ref, col_ref, s_ref, ck_ref, cv_ref, sc_ref,
                       rng_ref, qg_ref, kg_ref, kgc_ref, cw_ref, cb_ref, lng_ref, lnb_ref, pw_ref,
                       *rest):
    mix_ref, so_ref, cko_ref, cvo_ref, sco_ref, o_scr, oa_scr = rest[-7:]
    NB = SAMPLE_BLOCK
    P = WINDOW

    def put(ref, idx, val):
        for slab in range(ref.shape[0]):
            ref[(slab,) + idx] = val

    for bl in range(NB):
        for h in range(RET_HEADS):
            gamma = math.exp(LOG_DECAY[h])
            r0 = RET_DK * h
            S = s_ref[0, bl, h]
            qc = col_ref[0, r0:r0 + RET_DK, bl:bl + 1]
            kc = col_ref[0, 256 + r0:256 + r0 + RET_DK, bl:bl + 1] * (RET_DK ** -0.5)
            v = z_ref[bl:bl + 1, C_RV + 128 * h:C_RV + 128 * h + 128]
            qk = jnp.sum(qc * kc, axis=0, keepdims=True)
            o_scr[bl:bl + 1, 128 * h:128 * h + 128] = (
                gamma * jnp.sum(qc * S, axis=0, keepdims=True) + qk * v)
            put(so_ref, (bl, h), gamma * S + kc * v)
    for h in range(RET_HEADS):
        o = _rms(o_scr[:, 128 * h:128 * h + 128], rng_ref[l, h:h + 1, :])
        mix_ref[:, M_RET + 128 * h:M_RET + 128 * h + 128] = (
            o * _silu(z_ref[:, C_RG + 128 * h:C_RG + 128 * h + 128]))

    q_tiles = [_seg_rms(z_ref[:, C_AQ + 128 * t:C_AQ + 128 * t + 128], qg_ref[l:l + 1, :])
               * (HEAD_DIM ** -0.5) for t in range(2)]
    kn = _seg_rms(z_ref[:, C_AK:C_AK + 128], kg_ref[l:l + 1, :])
    vn = z_ref[:, C_AV:C_AV + 128]
    akc = col_ref[0, 512:640, :]
    vnc = col_ref[0, 640:768, :]
    ms = jnp.concatenate(
        [jnp.broadcast_to(jnp.mean(akc[64 * s:64 * s + 64, :] ** 2, axis=0, keepdims=True), (64, NB))
         for s in range(KV_HEADS)], axis=0)
    knc = akc * lax.rsqrt(ms + EPS) * kgc_ref[l]
    rid = lax.broadcasted_iota(jnp.int32, (8, 128), 0)
    lane = lax.broadcasted_iota(jnp.int32, (8, 128), 1)
    sel = ((rid % 2 == 0) == (lane < 64)) & (rid < ATT_HEADS)
    rcol = lax.broadcasted_iota(jnp.int32, (8, 1), 0)
    heads = [2 * (r % 2) + r // 2 for r in range(ATT_HEADS)]
    slope_col = jnp.zeros((8, 1), F32)
    sink_col = jnp.zeros((8, 1), F32)
    for r, hd in enumerate(heads):
        slope_col = jnp.where(rcol == r, SLOPES[hd], slope_col)
        sink_col = jnp.where(rcol == r, sinks_ref[l, hd], sink_col)
    dist = float(P) - lane.astype(F32)
    key_ok = (dist < float(WINDOW)) & (lane + (PAST_LEN - P) >= 0)
    newest = lax.broadcasted_iota(jnp.int32, (128, P), 1) == P - 1
    for bl in range(NB):
        KT = ck_ref[0, bl].reshape(128, P)
        VT = cv_ref[0, bl].reshape(128, P)
        qrows = jnp.where(rid < 2, q_tiles[0][bl:bl + 1, :], q_tiles[1][bl:bl + 1, :])
        q4 = jnp.where(sel, qrows, 0.0)
        s = _dot(q4.astype(BF16), KT.astype(BF16))
        s = jnp.where(key_ok, s - slope_col * dist, NEG_INF)
        s_new = jnp.sum(q4 * kn[bl:bl + 1, :], axis=-1, keepdims=True)
        m = jnp.maximum(jnp.maximum(jnp.max(s, axis=-1, keepdims=True), s_new), sink_col)
        e = jnp.exp(s - m)
        e_new = jnp.exp(s_new - m)
        inv = 1.0 / (jnp.sum(e, axis=-1, keepdims=True) + e_new + jnp.exp(sink_col - m))
        o = _dot_nt((e * inv).astype(BF16), VT.astype(BF16)) + (e_new * inv) * vn[bl:bl + 1, :]
        for t in range(2):
            oa_scr[bl:bl + 1, 128 * t:128 * t + 128] = jnp.where(
                lane[0:1, :] < 64, o[2 * t:2 * t + 1, :], o[2 * t + 1:2 * t + 2, :])
        put(cko_ref, (bl,), jnp.where(newest, knc[:, bl:bl + 1], pltpu.roll(KT, P - 1, 1)).reshape(
            KV_HEADS, HEAD_DIM, P))
        put(cvo_ref, (bl,), jnp.where(newest, vnc[:, bl:bl + 1], pltpu.roll(VT, P - 1, 1)).reshape(
            KV_HEADS, HEAD_DIM, P))
    mix_ref[:, M_ATT:M_ATT + 256] = oa_scr[...]

    u = z_ref[:, C_CA:C_CA + CONV_CH] * jax.nn.sigmoid(z_ref[:, C_CG:C_CG + CONV_CH])
    KT1 = CONV_K - 1
    y = cw_ref[l, KT1:KT1 + 1, :] * u
    for k in range(KT1):
        y = y + cw_ref[l, k:k + 1, :] * sc_ref[0, k]
        if k > 0:
            put(sco_ref, (k - 1,), sc_ref[0, k])
    put(sco_ref, (KT1 - 1,), u)
    y = _silu(_layer_norm(y + cb_ref[l:l + 1, :], lng_ref[l:l + 1, :], lnb_ref[l:l + 1, :]))
    mix_ref[:, M_CONV:M_CONV + CONV_CH] = _dot(y.astype(BF16), pw_ref[l].astype(BF16))


def _mix_sample(l, z, cols, s, ckt, cvt, sct, sinks, rng, qg2, kg2, kgc, cw, cb, lng, lnb, pw, prev):
    ns = z.shape[0]
    NB = SAMPLE_BLOCK
    whole = lambda a: pl.BlockSpec(a.shape, lambda i: (0,) * a.ndim)
    def specs(slabs, at):
        return [pl.BlockSpec((slabs, NB, RET_HEADS, RET_DK, RET_DV), lambda i: (at, i, 0, 0, 0)),
                pl.BlockSpec((slabs, NB, KV_HEADS, HEAD_DIM, WINDOW), lambda i: (at, i, 0, 0, 0)),
                pl.BlockSpec((slabs, NB, KV_HEADS, HEAD_DIM, WINDOW), lambda i: (at, i, 0, 0, 0)),
                pl.BlockSpec((slabs, CONV_K - 1, NB, CONV_CH), lambda i: (at, 0, i, 0))]

    state_specs = specs(1, l)
    out_state_specs = specs(s.shape[0], 0) if prev is None else state_specs
    in_specs = [pl.BlockSpec(memory_space=pltpu.SMEM),
                pl.BlockSpec((NB, IN_DIM), lambda i: (i, 0)),
                pl.BlockSpec((1, 768, NB), lambda i: (i, 0, 0)),
                *state_specs,
                whole(rng), whole(qg2), whole(kg2), whole(kgc), whole(cw), whole(cb), whole(lng),
                whole(lnb), whole(pw)]
    args = [sinks, z, cols, s, ckt, cvt, sct, rng, qg2, kg2, kgc, cw, cb, lng, lnb, pw]
    aliases = {}
    if prev is not None:
        for k, a in enumerate(prev):
            aliases[len(args)] = 1 + k
            in_specs.append(pl.BlockSpec(memory_space=pl.ANY))
            args.append(a)
    return pl.pallas_call(
        functools.partial(_mix_sample_kernel, l),
        grid=(ns // NB,),
        in_specs=in_specs,
        out_specs=[pl.BlockSpec((NB, MIX_DIM), lambda i: (i, 0)), *out_state_specs],
        out_shape=[jax.ShapeDtypeStruct((ns, MIX_DIM), F32),
                   jax.ShapeDtypeStruct(s.shape, F32), jax.ShapeDtypeStruct(ckt.shape, F32),
                   jax.ShapeDtypeStruct(cvt.shape, F32), jax.ShapeDtypeStruct(sct.shape, F32)],
        scratch_shapes=[pltpu.VMEM((NB, 512), F32), pltpu.VMEM((NB, 256), F32)],
        input_output_aliases=aliases,
        compiler_params=pltpu.CompilerParams(dimension_semantics=("arbitrary",)),
        name="mix_sample",
    )(*args)


def kernel(x_prompt, x_sample, state_ret, cache_k_win, cache_v_win, state_conv, ffn1_norm, ffn1_wg, ffn1_wu, ffn1_wd, mix_norm, w_in, ret_norm_g, q_norm_g, k_norm_g, sinks, conv_w, conv_b, conv_ln_g, conv_ln_b, conv_pw, w_out, ffn2_norm, ffn2_wg, ffn2_wu, ffn2_wd):
    nb, seq, _ = x_prompt.shape
    ns = x_sample.shape[0]
    assert x_sample.shape[1] == 1 and seq % MIX_TILE == 0 and (nb * seq) % TOKEN_TILE == 0
    assert ns % SAMPLE_BLOCK == 0 and ns <= TOKEN_TILE and cache_k_win.shape[2] == WINDOW
    hp = x_prompt.reshape(nb * seq, D_MODEL)
    hs = x_sample.reshape(ns, D_MODEL)
    ckt = cache_k_win.transpose(0, 1, 3, 4, 2)
    cvt = cache_v_win.transpose(0, 1, 3, 4, 2)
    sct = state_conv.transpose(0, 2, 1, 3)
    qg2 = jnp.tile(q_norm_g, (1, 2))
    kg2 = jnp.tile(k_norm_g, (1, 2))
    kgc = kg2.reshape(DEPTH, 128, 1)
    mixer_w = (conv_w, conv_b, conv_ln_g, conv_ln_b, conv_pw)
    prompt_states = [[] for _ in range(4)]
    sample_states = None
    for l in range(DEPTH):
        hp1, zp, hs1, zs = _ffn_in(l, hp, hs, ffn1_norm, mix_norm, ffn1_wg, ffn1_wu, ffn1_wd, w_in)
        mixp, r1, k1, v1, c1 = _mix_prompt(l, zp.reshape(nb, seq, IN_DIM), sinks, ret_norm_g, qg2,
                                           kg2, *mixer_w)
        cols = jnp.concatenate([zs[:, C_RQ:C_RQ + 512], zs[:, C_AK:C_AK + 256]], axis=1)
        cols = cols.T.reshape(768, ns // SAMPLE_BLOCK, SAMPLE_BLOCK).transpose(1, 0, 2)
        mixs, *sample_states = _mix_sample(l, zs, cols, state_ret, ckt, cvt, sct, sinks,
                                           ret_norm_g, qg2, kg2, kgc, *mixer_w, sample_states)
        hp, hs = _out_ffn(l, hp1, mixp.reshape(nb * seq, MIX_DIM), hs1, mixs, ffn2_norm,
                          w_out, ffn2_wg, ffn2_wu, ffn2_wd)
        for lst, val in zip(prompt_states, (r1, k1, v1, c1)):
            lst.append(val)
    ret_p, kwin_p, vwin_p, conv_p = (jnp.stack(s) for s in prompt_states)
    ret_s, kwin_s, vwin_s, conv_s = sample_states
    return (hp.reshape(nb, seq, D_MODEL), hs.reshape(ns, 1, D_MODEL),
            ret_p.reshape(DEPTH, nb, RET_HEADS, RET_DK, RET_DV), ret_s,
            kwin_p.reshape(DEPTH, nb, KV_HEADS, HEAD_DIM, WINDOW).transpose(0, 1, 4, 2, 3),
            kwin_s.transpose(0, 1, 4, 2, 3),
            vwin_p.reshape(DEPTH, nb, KV_HEADS, HEAD_DIM, WINDOW).transpose(0, 1, 4, 2, 3),
            vwin_s.transpose(0, 1, 4, 2, 3),
            conv_p, conv_s.transpose(0, 2, 1, 3))
```

```python
import functools
import math

import jax
import jax.numpy as jnp
from jax import lax
from jax.experimental import pallas as pl
from jax.experimental.pallas import tpu as pltpu

F32 = jnp.float32
BF16 = jnp.bfloat16

D_MODEL = 1024
DEPTH = 2
PAST_LEN = 8192
RET_HEADS = 4
RET_DK = 64
RET_DV = 128
RET_CHUNK = 128
ATT_HEADS = 4
KV_HEADS = 2
HEAD_DIM = 64
WINDOW = 128
CONV_CH = 256
CONV_K = 31
FFN_DIM = 2816
EPS = 1e-6
NEG_INF = -1e30

C_RQ, C_RK, C_RV, C_RG = 0, 256, 512, 1024
C_AQ, C_AK, C_AV, C_CA, C_CG = 1536, 1792, 1920, 2048, 2304
IN_DIM = 2560
MIX_DIM = 1024
M_RET, M_ATT, M_CONV = 0, 512, 768

LOG_DECAY = [math.log1p(-2.0 ** (-5 - h)) for h in range(RET_HEADS)]
SLOPES = [2.0 ** (-8.0 * (h + 1) / ATT_HEADS) for h in range(ATT_HEADS)]

V7X_VMEM_LIMIT = 60 * 1024 * 1024
TOKEN_TILE = 512
FFN_COLS = 256
OUT_COLS = 256
STAGE_ROWS = 128
SAMPLE_BLOCK = 8
CONV_ROWS = 64
CONV_PAD = 32
CONV_SPAN = CONV_ROWS + CONV_PAD - 8


def _dot(a, b):
    return jnp.dot(a, b, preferred_element_type=F32)


def _dot_nt(a, b):
    return lax.dot_general(a, b, (((1,), (1,)), ((), ())), preferred_element_type=F32)


def _dot_tn(a, b):
    return lax.dot_general(a, b, (((0,), (0,)), ((), ())), preferred_element_type=F32)


def _silu(x):
    return x * jax.nn.sigmoid(x)


def _rms(x, g):
    return x * lax.rsqrt(jnp.mean(x * x, axis=-1, keepdims=True) + EPS) * g


def _seg_rms(x, g):
    r = lax.broadcasted_iota(jnp.int32, (128, 128), 0)
    c = lax.broadcasted_iota(jnp.int32, (128, 128), 1)
    seg = jnp.where((r < 64) == (c < 64), 1.0 / HEAD_DIM, 0.0).astype(BF16)
    xx = x * x
    hi = xx.astype(BF16)
    lo = (xx - hi.astype(F32)).astype(BF16)
    ms = _dot(hi, seg) + _dot(lo, seg)
    return x * lax.rsqrt(ms + EPS) * g


def _layer_norm(y, g, b):
    mu = jnp.mean(y, axis=-1, keepdims=True)
    d = y - mu
    var = jnp.mean(d * d, axis=-1, keepdims=True)
    return d * lax.rsqrt(var + EPS) * g + b


def _stage_copy(src, l, j, stage, sem, slot, width):
    return pltpu.make_async_copy(src.at[l, pl.ds(j * STAGE_ROWS, STAGE_ROWS), :],
                                 stage.at[slot, :, pl.ds(0, width)], sem.at[slot])


def _swap_inner_heads(t0, t1):
    low = lax.broadcasted_iota(jnp.int32, t0.shape, 1) < 64
    return (jnp.where(low, t0, pltpu.roll(t1, 64, 1)), jnp.where(low, pltpu.roll(t0, 64, 1), t1))


def _stage_weight(src, l, dst, stage, sem, permute_q_cols=False):
    rows, width = dst.shape
    n = rows // STAGE_ROWS
    _stage_copy(src, l, 0, stage, sem, 0, width).start()

    def body(j, carry):
        slot = lax.rem(j, 2)

        @pl.when(j + 1 < n)
        def _():
            _stage_copy(src, l, j + 1, stage, sem, 1 - slot, width).start()

        _stage_copy(src, l, j, stage, sem, slot, width).wait()
        r0 = pl.multiple_of(j * STAGE_ROWS, STAGE_ROWS)
        dst[pl.ds(r0, STAGE_ROWS), :] = stage[slot, :, 0:width].astype(BF16)
        if permute_q_cols:
            t0, t1 = _swap_inner_heads(stage[slot, :, C_AQ:C_AQ + 128],
                                       stage[slot, :, C_AQ + 128:C_AQ + 256])
            dst[pl.ds(r0, STAGE_ROWS), C_AQ:C_AQ + 128] = t0.astype(BF16)
            dst[pl.ds(r0, STAGE_ROWS), C_AQ + 128:C_AQ + 256] = t1.astype(BF16)
        return carry

    lax.fori_loop(0, n, body, 0)


def _stage_w_out(src, l, dst, stage, sem):
    rows, width = dst.shape
    n = rows // STAGE_ROWS
    _stage_copy(src, l, 0, stage, sem, 0, width).start()
    for j in range(n):
        slot = j % 2
        if j + 1 < n:
            _stage_copy(src, l, j + 1, stage, sem, 1 - slot, width).start()
        _stage_copy(src, l, j, stage, sem, slot, width).wait()
        r0 = j * STAGE_ROWS
        if r0 == M_ATT:
            dst[M_ATT:M_ATT + 64, :] = stage[slot, 0:64, 0:width].astype(BF16)
            dst[M_ATT + 128:M_ATT + 192, :] = stage[slot, 64:128, 0:width].astype(BF16)
        elif r0 == M_ATT + 128:
            dst[M_ATT + 64:M_ATT + 128, :] = stage[slot, 0:64, 0:width].astype(BF16)
            dst[M_ATT + 192:M_ATT + 256, :] = stage[slot, 64:128, 0:width].astype(BF16)
        else:
            dst[r0:r0 + STAGE_ROWS, :] = stage[slot, :, 0:width].astype(BF16)


def _run(steps):
    for _ in steps:
        pass


def _interleave(major, minor, major_total, minor_total):
    major_done = minor_done = 0
    for cost in major:
        major_done += cost
        while minor_done * major_total < major_done * minor_total:
            step = next(minor, None)
            if step is None:
                break
            minor_done += step
    _run(minor)


def _matmul_cost(rows, k, n):
    return rows * k * n // (256 * 1024)


def _ffn_steps(x_ref, g, wg_ref, wu_ref, wd_ref, out_ref, xn_scr, a_scr):
    rows = x_ref.shape[0]
    xn_scr[...] = _rms(x_ref[...], g).astype(BF16)
    for c in range(FFN_DIM // FFN_COLS):
        sl = slice(c * FFN_COLS, (c + 1) * FFN_COLS)
        gate = _dot(xn_scr[...], wg_ref[:, sl])
        up = _dot(xn_scr[...], wu_ref[:, sl])
        a_scr[:, sl] = (_silu(gate) * up).astype(BF16)
        yield 2 * _matmul_cost(rows, D_MODEL, FFN_COLS)
    for c in range(D_MODEL // OUT_COLS):
        sl = slice(c * OUT_COLS, (c + 1) * OUT_COLS)
        out_ref[:, sl] = x_ref[:, sl] + 0.5 * _dot(a_scr[...], wd_ref[:, sl])
        yield _matmul_cost(rows, FFN_DIM, OUT_COLS)


def _out_ffn_steps(l, x1_ref, mix_ref, x3_ref, g_ref, wout_b, wg_b, wu_b, wd_b, x2, xn, a):
    rows = x1_ref.shape[0]
    mix = mix_ref[...].astype(BF16)
    for c in range(D_MODEL // OUT_COLS):
        sl = slice(c * OUT_COLS, (c + 1) * OUT_COLS)
        x2[:, sl] = x1_ref[:, sl] + _dot(mix, wout_b[:, sl])
        yield _matmul_cost(rows, MIX_DIM, OUT_COLS)
    yield from _ffn_steps(x2, g_ref[l:l + 1, :], wg_b, wu_b, wd_b, x3_ref, xn, a)


def _out_ffn_cost(rows):
    return _matmul_cost(rows, MIX_DIM * D_MODEL + 3 * D_MODEL * FFN_DIM, 1)


def _ffn_in_kernel(l, n_tiles, xp_ref, xs_ref, g1_ref, g2_ref, wg_hbm, wu_hbm, wd_hbm, win_hbm,
                   x1p_ref, zp_ref, x1s_ref, zs_ref,
                   wg_b, wu_b, wd_b, win_b, stage, sem, xn_scr, a_scr):
    i = pl.program_id(0)

    @pl.when(i == 0)
    def _():
        _stage_weight(wg_hbm, l, wg_b, stage, sem)
        _stage_weight(wu_hbm, l, wu_b, stage, sem)
        _stage_weight(wd_hbm, l, wd_b, stage, sem)
        _stage_weight(win_hbm, l, win_b, stage, sem, permute_q_cols=True)

    def body(x_ref, x1_ref, z_ref, xn, a):
        _run(_ffn_steps(x_ref, g1_ref[l:l + 1, :], wg_b, wu_b, wd_b, x1_ref, xn, a))
        xn[...] = _rms(x1_ref[...], g2_ref[l:l + 1, :]).astype(BF16)
        for c in range(IN_DIM // OUT_COLS):
            sl = slice(c * OUT_COLS, (c + 1) * OUT_COLS)
            z_ref[:, sl] = _dot(xn[...], win_b[:, sl])

    @pl.when(i < n_tiles)
    def _():
        body(xp_ref, x1p_ref, zp_ref, xn_scr, a_scr)

    @pl.when(i == n_tiles)
    def _():
        ns = xs_ref.shape[0]
        body(xs_ref, x1s_ref, zs_ref, xn_scr.at[0:ns], a_scr.at[0:ns])


def _stage_scratch():
    return [pltpu.VMEM((2, STAGE_ROWS, FFN_DIM), F32), pltpu.SemaphoreType.DMA((2,))]


def _ffn_in(l, xp, xs, g1, g2, wg, wu, wd, win):
    rows, ns = xp.shape[0], xs.shape[0]
    n_tiles = rows // TOKEN_TILE
    tile = lambda w: pl.BlockSpec((TOKEN_TILE, w), lambda i: (jnp.minimum(i, n_tiles - 1), 0))
    once = lambda w: pl.BlockSpec((ns, w), lambda i: (0, 0), pipeline_mode=pl.Buffered(1))
    whole = lambda a: pl.BlockSpec(a.shape, lambda i: (0,) * a.ndim)
    hbm = pl.BlockSpec(memory_space=pl.ANY)
    return pl.pallas_call(
        functools.partial(_ffn_in_kernel, l, n_tiles),
        grid=(n_tiles + 1,),
        in_specs=[tile(D_MODEL), once(D_MODEL), whole(g1), whole(g2), hbm, hbm, hbm, hbm],
        out_specs=[tile(D_MODEL), tile(IN_DIM), once(D_MODEL), once(IN_DIM)],
        out_shape=[jax.ShapeDtypeStruct((rows, D_MODEL), F32),
                   jax.ShapeDtypeStruct((rows, IN_DIM), F32),
                   jax.ShapeDtypeStruct((ns, D_MODEL), F32),
                   jax.ShapeDtypeStruct((ns, IN_DIM), F32)],
        scratch_shapes=[pltpu.VMEM((D_MODEL, FFN_DIM), BF16), pltpu.VMEM((D_MODEL, FFN_DIM), BF16),
                        pltpu.VMEM((FFN_DIM, D_MODEL), BF16), pltpu.VMEM((D_MODEL, IN_DIM), BF16),
                        *_stage_scratch(),
                        pltpu.VMEM((TOKEN_TILE, D_MODEL), BF16),
                        pltpu.VMEM((TOKEN_TILE, FFN_DIM), BF16)],
        compiler_params=pltpu.CompilerParams(dimension_semantics=("arbitrary",),
                                             vmem_limit_bytes=V7X_VMEM_LIMIT),
        name="ffn_in",
    )(xp, xs, g1, g2, wg, wu, wd, win)


def _mix_constants(l, pw_ref, dmask, qdec, kdec, abias, pw_b):
    T = RET_CHUNK
    row_f = lax.broadcasted_iota(jnp.int32, (T, T), 0).astype(F32)
    col_f = lax.broadcasted_iota(jnp.int32, (T, T), 1).astype(F32)
    low = lax.broadcasted_iota(jnp.int32, (T, T), 1) < 64
    for h in range(RET_HEADS):
        lg = LOG_DECAY[h]
        diff = row_f - col_f
        dmask[h] = jnp.where(diff >= 0, jnp.exp(lg * jnp.maximum(diff, 0.0)), 0.0)
        qdec[h] = jnp.exp(lg * (row_f + 1.0))
    for p in range(RET_HEADS // 2):
        lgp = jnp.where(low, LOG_DECAY[2 * p], LOG_DECAY[2 * p + 1])
        kdec[p] = jnp.exp(lgp * (T - 1.0 - row_f))
    qpos = lax.broadcasted_iota(jnp.int32, (T, 2 * T), 0) + T
    kpos = lax.broadcasted_iota(jnp.int32, (T, 2 * T), 1)
    dist = (qpos - kpos).astype(F32)
    for h in range(ATT_HEADS):
        abias[h] = jnp.where((dist >= 0) & (dist < WINDOW), -SLOPES[h] * dist, NEG_INF)
    pw_b[...] = pw_ref[l].astype(BF16)


def _mix_tile_steps(l, first, z_ref, mix_ref, sinks_ref, rng_ref, qg_ref, kg_ref, cw_ref, cb_ref,
                    lng_ref, lnb_ref, s_scr, kbuf, vbuf, qbuf, ubuf, ush, klast, dmask, qdec, kdec,
                    abias, pw_b):
    T = RET_CHUNK
    NCH = TOKEN_TILE // T
    row_i = lax.broadcasted_iota(jnp.int32, (T, T), 0)
    low = lax.broadcasted_iota(jnp.int32, (T, T), 1) < 64

    def zc(r0, nrows, a, w):
        return z_ref[r0:r0 + nrows, a:a + w]

    def seg_mean_sq(x):
        r = lax.broadcasted_iota(jnp.int32, (128, 128), 0)
        c = lax.broadcasted_iota(jnp.int32, (128, 128), 1)
        seg = jnp.where((r < 64) == (c < 64), 1.0 / HEAD_DIM, 0.0).astype(BF16)
        xx = x * x
        hi = xx.astype(BF16)
        lo = (xx - hi.astype(F32)).astype(BF16)
        return _dot(hi, seg) + _dot(lo, seg)

    def retention_unit(p, j):
        s_cur = s_scr[p]
        r0 = T * j
        qp = zc(r0, T, C_RQ + 128 * p, 128)
        kp = zc(r0, T, C_RK + 128 * p, 128) * (RET_DK ** -0.5)
        vb = zc(r0, T, C_RV + 256 * p, 256).astype(BF16)
        kb = kp.astype(BF16)
        sb = s_cur.astype(BF16)
        qm = [jnp.where(low if e == 0 else jnp.logical_not(low), qp, 0.0).astype(BF16)
              for e in range(2)]
        a = [_dot_nt(qm[e], kb) for e in range(2)]
        qs = [_dot(qm[e], sb) for e in range(2)]
        upd = _dot_tn((kp * kdec[p]).astype(BF16), vb)
        yield RET_COST // 4
        ab = [(a[e] * dmask[2 * p + e]).astype(BF16) for e in range(2)]
        yield RET_COST // 4
        o = [_dot(ab[e], vb[:, 128 * e:128 * e + 128]) for e in range(2)]
        yield RET_COST // 4
        for e in range(2):
            h = 2 * p + e
            oh = _rms(o[e] + qs[e] * qdec[h], rng_ref[l, h:h + 1, :])
            mix_ref[r0:r0 + T, M_RET + 128 * h:M_RET + 128 * h + 128] = (
                oh * _silu(zc(r0, T, C_RG + 128 * h, 128))).astype(BF16)
        top = row_i < 64
        new = jnp.where(top, upd[:, 0:128], upd[:, 128:256])
        cd = jnp.where(top, math.exp(LOG_DECAY[2 * p] * T), math.exp(LOG_DECAY[2 * p + 1] * T))
        s_scr[p] = s_cur * cd + new
        yield RET_COST // 4

    def attention_prep():
        ak = zc(0, TOKEN_TILE, C_AK, 128)
        aq = [zc(0, TOKEN_TILE, C_AQ + 128 * t, 128) for t in range(2)]
        ms_k = seg_mean_sq(ak)
        ms_q = [seg_mean_sq(aq[t]) for t in range(2)]
        yield ATT_PREP_COST // 2
        kn = ak * lax.rsqrt(ms_k + EPS) * kg_ref[l:l + 1, :]
        klast[...] = kn[TOKEN_TILE - T:TOKEN_TILE, :]
        kbuf[T:T + TOKEN_TILE, :] = kn.astype(BF16)
        vbuf[T:T + TOKEN_TILE, :] = zc(0, TOKEN_TILE, C_AV, 128).astype(BF16)
        low_t = lax.broadcasted_iota(jnp.int32, (TOKEN_TILE, 128), 1) < 64
        for t in range(2):
            qt = aq[t] * lax.rsqrt(ms_q[t] + EPS) * qg_ref[l:l + 1, :] * (HEAD_DIM ** -0.5)
            qbuf[t, 0] = jnp.where(low_t, qt, 0.0).astype(BF16)
            qbuf[t, 1] = jnp.where(low_t, 0.0, qt).astype(BF16)
        yield ATT_PREP_COST // 2

    def attention_unit(t, j):
        r0 = T * j
        kcat = kbuf[r0:r0 + 2 * T, :]
        vcat = vbuf[r0:r0 + 2 * T, :]
        s = [_dot_nt(qbuf[t, kvh, r0:r0 + T, :], kcat) for kvh in range(KV_HEADS)]
        yield ATT_COST // 3
        probs = []
        for kvh in range(KV_HEADS):
            head = 2 * kvh + t
            sink = sinks_ref[l, head]
            bias = abias[head]
            if j == 0:
                no_prev = lax.broadcasted_iota(jnp.int32, (T, 2 * T), 1) < jnp.where(first, T, 0)
                bias = jnp.where(no_prev, NEG_INF, bias)
            sb = s[kvh] + bias
            m = jnp.maximum(jnp.max(sb, axis=-1, keepdims=True), sink)
            e = jnp.exp(sb - m)
            inv = 1.0 / (jnp.sum(e, axis=-1, keepdims=True) + jnp.exp(sink - m))
            probs.append((e * inv).astype(BF16))
        yield ATT_COST // 3
        halves = [_dot(probs[kvh], vcat) for kvh in range(KV_HEADS)]
        mix_ref[r0:r0 + T, M_ATT + 128 * t:M_ATT + 128 * t + 128] = jnp.where(
            low, halves[0], halves[1]).astype(BF16)
        yield ATT_COST // 3

    def conv_prep():
        ubuf[CONV_PAD:CONV_PAD + TOKEN_TILE, :] = (
            zc(0, TOKEN_TILE, C_CA, CONV_CH) * jax.nn.sigmoid(zc(0, TOKEN_TILE, C_CG, CONV_CH)))
        yield CONV_PREP_COST

    def conv_unit(rb):
        base = rb * CONV_ROWS
        sh = ush.at[rb % 2]
        for r in range(1, 8):
            sh[r - 1] = ubuf[base + r:base + r + CONV_SPAN, :]
        acc = None
        for k in range(CONV_K):
            off = CONV_PAD - (CONV_K - 1) + k
            a8, r = off // 8 * 8, off % 8
            src = (ubuf[base + a8:base + a8 + CONV_ROWS, :] if r == 0
                   else sh[r - 1, a8:a8 + CONV_ROWS, :])
            term = cw_ref[l, k:k + 1, :] * src
            acc = term if acc is None else acc + term
        y = _silu(_layer_norm(acc + cb_ref[l:l + 1, :], lng_ref[l:l + 1, :],
                              lnb_ref[l:l + 1, :])).astype(BF16)
        yield CONV_COST - CONV_COST // 8
        mix_ref[base:base + CONV_ROWS, M_CONV:M_CONV + CONV_CH] = _dot(y, pw_b[...]).astype(BF16)
        yield CONV_COST // 8

    def ret_lane():
        for j in range(NCH):
            for p in range(RET_HEADS // 2):
                yield from retention_unit(p, j)

    def att_lane():
        yield from attention_prep()
        for t in range(2):
            for j in range(NCH):
                yield from attention_unit(t, j)

    def conv_lane():
        yield from conv_prep()
        for rb in range(TOKEN_TILE // CONV_ROWS):
            yield from conv_unit(rb)

    live = [att_lane(), conv_lane(), ret_lane()]
    while live:
        for lane in list(live):
            cost = next(lane, None)
            if cost is None:
                live.remove(lane)
            else:
                yield cost


RET_COST, ATT_PREP_COST, ATT_COST, CONV_PREP_COST, CONV_COST = 160, 250, 220, 200, 410
MIX_COST = ((RET_HEADS // 2) * (TOKEN_TILE // RET_CHUNK) * RET_COST + ATT_PREP_COST
            + 2 * (TOKEN_TILE // RET_CHUNK) * ATT_COST + CONV_PREP_COST
            + (TOKEN_TILE // CONV_ROWS) * CONV_COST)


def _mix_out_ffn_kernel(l, n_tiles, tps, sinks_ref, z_ref, x1p_ref, x1s_ref, mixs_ref, g_ref,
                        rng_ref, qg_ref, kg_ref, cw_ref, cb_ref, lng_ref, lnb_ref, pw_ref,
                        wout_hbm, wg_hbm, wu_hbm, wd_hbm,
                        x3p_ref, x3s_ref, ret_ref, kwin_ref, vwin_ref, conv_ref,
                        wout_b, wg_b, wu_b, wd_b, stage, sem, x2_scr, xn_scr, a_scr, mix_new, mix_old,
                        s_scr, kbuf, vbuf, qbuf, ubuf, ush, klast, dmask, qdec, kdec, abias, pw_b):
    i = pl.program_id(0)
    c = lax.rem(i, tps)
    T = RET_CHUNK
    mix_args = (sinks_ref, rng_ref, qg_ref, kg_ref, cw_ref, cb_ref, lng_ref, lnb_ref,
                s_scr, kbuf, vbuf, qbuf, ubuf, ush, klast, dmask, qdec, kdec, abias, pw_b)
    dense_w =(g_ref, wout_b, wg_b, wu_b, wd_b)

    @pl.when(i == 0)
    def _():
        _stage_w_out(wout_hbm, l, wout_b, stage, sem)
        _stage_weight(wg_hbm, l, wg_b, stage, sem)
        _stage_weight(wu_hbm, l, wu_b, stage, sem)
        _stage_weight(wd_hbm, l, wd_b, stage, sem)
        _mix_constants(l, pw_ref, dmask, qdec, kdec, abias, pw_b)

    @pl.when(c == 0)
    def _():
        s_scr[...] = jnp.zeros_like(s_scr)
        kbuf[0:T, :] = jnp.zeros((T, 128), BF16)
        vbuf[0:T, :] = jnp.zeros((T, 128), BF16)
        ubuf[0:CONV_PAD, :] = jnp.zeros((CONV_PAD, CONV_CH), F32)

    @pl.when(c > 0)
    def _():
        kbuf[0:T, :] = kbuf[TOKEN_TILE:TOKEN_TILE + T, :]
        vbuf[0:T, :] = vbuf[TOKEN_TILE:TOKEN_TILE + T, :]
        ubuf[0:CONV_PAD, :] = ubuf[TOKEN_TILE:TOKEN_TILE + CONV_PAD, :]

    @pl.when(i > 0)
    def _():
        mix_old[...] = mix_new[...]

    def mixers():
        return _mix_tile_steps(l, c == 0, z_ref, mix_new, *mix_args)

    def dense():
        return _out_ffn_steps(l, x1p_ref, mix_old, x3p_ref, *dense_w, x2_scr, xn_scr, a_scr)

    @pl.when(i == 0)
    def _():
        _run(mixers())

    @pl.when((i > 0) & (i < n_tiles))
    def _():
        _interleave(dense(), mixers(), _out_ffn_cost(TOKEN_TILE), MIX_COST)

    @pl.when(i == n_tiles)
    def _():
        _run(dense())
        ns = x1s_ref.shape[0]
        _run(_out_ffn_steps(l, x1s_ref, mixs_ref, x3s_ref, *dense_w,
                            x2_scr.at[0:ns], xn_scr.at[0:ns], a_scr.at[0:ns]))

    @pl.when((c == tps - 1) & (i < n_tiles))
    def _():
        ret_ref[0] = s_scr[...]
        kwin_ref[0] = klast[...].T
        vwin_ref[0] = z_ref[TOKEN_TILE - T:TOKEN_TILE, C_AV:C_AV + 128].T
        conv_ref[0] = ubuf[CONV_PAD + TOKEN_TILE - (CONV_K - 1):CONV_PAD + TOKEN_TILE, :]


def _mix_out_ffn(l, nb, z, x1p, x1s, mixs, g, sinks, rng, qg2, kg2, cw, cb, lng, lnb, pw,
                 wout, wg, wu, wd):
    rows, ns = x1p.shape[0], x1s.shape[0]
    n_tiles = rows // TOKEN_TILE
    tps = n_tiles // nb
    T = RET_CHUNK
    cur = lambda w: pl.BlockSpec((TOKEN_TILE, w), lambda i: (jnp.minimum(i, n_tiles - 1), 0))
    prev = lambda w: pl.BlockSpec((TOKEN_TILE, w), lambda i: (jnp.maximum(i - 1, 0), 0))
    once = lambda w: pl.BlockSpec((ns, w), lambda i: (0, 0), pipeline_mode=pl.Buffered(1))
    whole = lambda a: pl.BlockSpec(a.shape, lambda i: (0,) * a.ndim)
    per_seq = lambda shape: pl.BlockSpec(
        (1,) + shape, lambda i: (jnp.minimum(i, n_tiles - 1) // tps,) + (0,) * len(shape))
    hbm = pl.BlockSpec(memory_space=pl.ANY)
    return pl.pallas_call(
        functools.partial(_mix_out_ffn_kernel, l, n_tiles, tps),
        grid=(n_tiles + 1,),
        in_specs=[pl.BlockSpec(memory_space=pltpu.SMEM), cur(IN_DIM), prev(D_MODEL),
                  once(D_MODEL), once(MIX_DIM), whole(g),
                  whole(rng), whole(qg2), whole(kg2), whole(cw), whole(cb), whole(lng), whole(lnb),
                  whole(pw), hbm, hbm, hbm, hbm],
        out_specs=[prev(D_MODEL), once(D_MODEL),
                   per_seq((2, 128, RET_DV)), per_seq((128, WINDOW)), per_seq((128, WINDOW)),
                   per_seq((CONV_K - 1, CONV_CH))],
        out_shape=[jax.ShapeDtypeStruct((rows, D_MODEL), F32),
                   jax.ShapeDtypeStruct((ns, D_MODEL), F32),
                   jax.ShapeDtypeStruct((nb, 2, 128, RET_DV), F32),
                   jax.ShapeDtypeStruct((nb, 128, WINDOW), F32),
                   jax.ShapeDtypeStruct((nb, 128, WINDOW), F32),
                   jax.ShapeDtypeStruct((nb, CONV_K - 1, CONV_CH), F32)],
        scratch_shapes=[pltpu.VMEM((MIX_DIM, D_MODEL), BF16),
                        pltpu.VMEM((D_MODEL, FFN_DIM), BF16), pltpu.VMEM((D_MODEL, FFN_DIM), BF16),
                        pltpu.VMEM((FFN_DIM, D_MODEL), BF16),
                        *_stage_scratch(),
                        pltpu.VMEM((TOKEN_TILE, D_MODEL), F32),
                        pltpu.VMEM((TOKEN_TILE, D_MODEL), BF16),
                        pltpu.VMEM((TOKEN_TILE, FFN_DIM), BF16),
                        pltpu.VMEM((TOKEN_TILE, MIX_DIM), BF16),
                        pltpu.VMEM((TOKEN_TILE, MIX_DIM), BF16),
                        pltpu.VMEM((2, 128, RET_DV), F32),
                        pltpu.VMEM((T + TOKEN_TILE, 128), BF16),
                        pltpu.VMEM((T + TOKEN_TILE, 128), BF16),
                        pltpu.VMEM((2, KV_HEADS, TOKEN_TILE, 128), BF16),
                        pltpu.VMEM((CONV_PAD + TOKEN_TILE, CONV_CH), F32),
                        pltpu.VMEM((2, 7, CONV_SPAN, CONV_CH), F32),
                        pltpu.VMEM((T, 128), F32),
                        pltpu.VMEM((RET_HEADS, T, T), F32), pltpu.VMEM((RET_HEADS, T, T), F32),
                        pltpu.VMEM((2, T, T), F32), pltpu.VMEM((ATT_HEADS, T, 2 * T), F32),
                        pltpu.VMEM((CONV_CH, CONV_CH), BF16)],
        compiler_params=pltpu.CompilerParams(dimension_semantics=("arbitrary",),
                                             vmem_limit_bytes=V7X_VMEM_LIMIT),
        name="mix_out_ffn",
    )(sinks, z, x1p, x1s, mixs, g, rng, qg2, kg2, cw, cb, lng, lnb, pw, wout, wg, wu, wd)


def _mix_sample_kernel(l, sinks_ref, z_ref, col_ref, s_ref, ck_ref, cv_ref, sc_ref,
                       rng_ref, qg_ref, kg_ref, kgc_ref, cw_ref, cb_ref, lng_ref, lnb_ref, pw_ref,
                       *rest):
    mix_ref, so_ref, cko_ref, cvo_ref, sco_ref, o_scr, oa_scr = rest[-7:]
    NB = SAMPLE_BLOCK
    P = WINDOW

    def put(ref, idx, val):
        for slab in range(ref.shape[0]):
            ref[(slab,) + idx] = val

    for bl in range(NB):
        for h in range(RET_HEADS):
            gamma = math.exp(LOG_DECAY[h])
            r0 = RET_DK * h
            S = s_ref[0, bl, h]
            qc = col_ref[0, r0:r0 + RET_DK, bl:bl + 1]
            kc = col_ref[0, 256 + r0:256 + r0 + RET_DK, bl:bl + 1] * (RET_DK ** -0.5)
            v = z_ref[bl:bl + 1, C_RV + 128 * h:C_RV + 128 * h + 128]
            qk = jnp.sum(qc * kc, axis=0, keepdims=True)
            o_scr[bl:bl + 1, 128 * h:128 * h + 128] = (
                gamma * jnp.sum(qc * S, axis=0, keepdims=True) + qk * v)
            put(so_ref, (bl, h), gamma * S + kc * v)
    for h in range(RET_HEADS):
        o = _rms(o_scr[:, 128 * h:128 * h + 128], rng_ref[l, h:h + 1, :])
        mix_ref[:, M_RET + 128 * h:M_RET + 128 * h + 128] = (
            o * _silu(z_ref[:, C_RG + 128 * h:C_RG + 128 * h + 128]))

    q_tiles = [_seg_rms(z_ref[:, C_AQ + 128 * t:C_AQ + 128 * t + 128], qg_ref[l:l + 1, :])
               * (HEAD_DIM ** -0.5) for t in range(2)]
    kn = _seg_rms(z_ref[:, C_AK:C_AK + 128], kg_ref[l:l + 1, :])
    vn = z_ref[:, C_AV:C_AV + 128]
    akc = col_ref[0, 512:640, :]
    vnc = col_ref[0, 640:768, :]
    ms = jnp.concatenate(
        [jnp.broadcast_to(jnp.mean(akc[64 * s:64 * s + 64, :] ** 2, axis=0, keepdims=True), (64, NB))
         for s in range(KV_HEADS)], axis=0)
    knc = akc * lax.rsqrt(ms + EPS) * kgc_ref[l]
    rid = lax.broadcasted_iota(jnp.int32, (8, 128), 0)
    lane = lax.broadcasted_iota(jnp.int32, (8, 128), 1)
    sel = ((rid % 2 == 0) == (lane < 64)) & (rid < ATT_HEADS)
    rcol = lax.broadcasted_iota(jnp.int32, (8, 1), 0)
    heads = [2 * (r % 2) + r // 2 for r in range(ATT_HEADS)]
    slope_col = jnp.zeros((8, 1), F32)
    sink_col = jnp.zeros((8, 1), F32)
    for r, hd in enumerate(heads):
        slope_col = jnp.where(rcol == r, SLOPES[hd], slope_col)
        sink_col = jnp.where(rcol == r, sinks_ref[l, hd], sink_col)
    dist = float(P) - lane.astype(F32)
    key_ok = (dist < float(WINDOW)) & (lane + (PAST_LEN - P) >= 0)
    newest = lax.broadcasted_iota(jnp.int32, (128, P), 1) == P - 1
    for bl in range(NB):
        KT = ck_ref[0, bl].reshape(128, P)
        VT = cv_ref[0, bl].reshape(128, P)
        qrows = jnp.where(rid < 2, q_tiles[0][bl:bl + 1, :], q_tiles[1][bl:bl + 1, :])
        q4 = jnp.where(sel, qrows, 0.0)
        s = _dot(q4.astype(BF16), KT.astype(BF16))
        s = jnp.where(key_ok, s - slope_col * dist, NEG_INF)
        s_new = jnp.sum(q4 * kn[bl:bl + 1, :], axis=-1, keepdims=True)
        m = jnp.maximum(jnp.maximum(jnp.max(s, axis=-1, keepdims=True), s_new), sink_col)
        e = jnp.exp(s - m)
        e_new = jnp.exp(s_new - m)
        inv = 1.0 / (jnp.sum(e, axis=-1, keepdims=True) + e_new + jnp.exp(sink_col - m))
        o = _dot_nt((e * inv).astype(BF16), VT.astype(BF16)) + (e_new * inv) * vn[bl:bl + 1, :]
        for t in range(2):
            oa_scr[bl:bl + 1, 128 * t:128 * t + 128] = jnp.where(
                lane[0:1, :] < 64, o[2 * t:2 * t + 1, :], o[2 * t + 1:2 * t + 2, :])
        put(cko_ref, (bl,), jnp.where(newest, knc[:, bl:bl + 1], pltpu.roll(KT, P - 1, 1)).reshape(
            KV_HEADS, HEAD_DIM, P))
        put(cvo_ref, (bl,), jnp.where(newest, vnc[:, bl:bl + 1], pltpu.roll(VT, P - 1, 1)).reshape(
            KV_HEADS, HEAD_DIM, P))
    mix_ref[:, M_ATT:M_ATT + 256] = oa_scr[...]

    u = z_ref[:, C_CA:C_CA + CONV_CH] * jax.nn.sigmoid(z_ref[:, C_CG:C_CG + CONV_CH])
    KT1 = CONV_K - 1
    y = cw_ref[l, KT1:KT1 + 1, :] * u
    for k in range(KT1):
        y = y + cw_ref[l, k:k + 1, :] * sc_ref[0, k]
        if k > 0:
            put(sco_ref, (k - 1,), sc_ref[0, k])
    put(sco_ref, (KT1 - 1,), u)
    y = _silu(_layer_norm(y + cb_ref[l:l + 1, :], lng_ref[l:l + 1, :], lnb_ref[l:l + 1, :]))
    mix_ref[:, M_CONV:M_CONV + CONV_CH] = _dot(y.astype(BF16), pw_ref[l].astype(BF16))


def _mix_sample(l, z, cols, s, ckt, cvt, sct, sinks, rng, qg2, kg2, kgc, cw, cb, lng, lnb, pw, prev):
    ns = z.shape[0]
    NB = SAMPLE_BLOCK
    whole = lambda a: pl.BlockSpec(a.shape, lambda i: (0,) * a.ndim)

    def specs(slabs, at):
        return [pl.BlockSpec((slabs, NB, RET_HEADS, RET_DK, RET_DV), lambda i: (at, i, 0, 0, 0)),
                pl.BlockSpec((slabs, NB, KV_HEADS, HEAD_DIM, WINDOW), lambda i: (at, i, 0, 0, 0)),
                pl.BlockSpec((slabs, NB, KV_HEADS, HEAD_DIM, WINDOW), lambda i: (at, i, 0, 0, 0)),
                pl.BlockSpec((slabs, CONV_K - 1, NB, CONV_CH), lambda i: (at, 0, i, 0))]

    state_specs = specs(1, l)
    out_state_specs = specs(s.shape[0], 0) if prev is None else state_specs
    in_specs = [pl.BlockSpec(memory_space=pltpu.SMEM),
                pl.BlockSpec((NB, IN_DIM), lambda i: (i, 0)),
                pl.BlockSpec((1, 768, NB), lambda i: (i, 0, 0)),
                *state_specs,
                whole(rng), whole(qg2), whole(kg2), whole(kgc), whole(cw), whole(cb), whole(lng),
                whole(lnb), whole(pw)]
    args = [sinks, z, cols, s, ckt, cvt, sct, rng, qg2, kg2, kgc, cw, cb, lng, lnb, pw]
    aliases = {}
    if prev is not None:
        for k, a in enumerate(prev):
            aliases[len(args)] = 1 + k
            in_specs.append(pl.BlockSpec(memory_space=pl.ANY))
            args.append(a)
    return pl.pallas_call(
        functools.partial(_mix_sample_kernel, l),
        grid=(ns // NB,),
        in_specs=in_specs,
        out_specs=[pl.BlockSpec((NB, MIX_DIM), lambda i: (i, 0)), *out_state_specs],
        out_shape=[jax.ShapeDtypeStruct((ns, MIX_DIM), F32),
                   jax.ShapeDtypeStruct(s.shape, F32), jax.ShapeDtypeStruct(ckt.shape, F32),
                   jax.ShapeDtypeStruct(cvt.shape, F32), jax.ShapeDtypeStruct(sct.shape, F32)],
        scratch_shapes=[pltpu.VMEM((NB, 512), F32), pltpu.VMEM((NB, 256), F32)],
        input_output_aliases=aliases,
        compiler_params=pltpu.CompilerParams(dimension_semantics=("arbitrary",)),
        name="mix_sample",
    )(*args)


def kernel(x_prompt, x_sample, state_ret, cache_k_win, cache_v_win, state_conv, ffn1_norm, ffn1_wg, ffn1_wu, ffn1_wd, mix_norm, w_in, ret_norm_g, q_norm_g, k_norm_g, sinks, conv_w, conv_b, conv_ln_g, conv_ln_b, conv_pw, w_out, ffn2_norm, ffn2_wg, ffn2_wu, ffn2_wd):
    nb, seq, _ = x_prompt.shape
    ns = x_sample.shape[0]
    assert x_sample.shape[1] == 1 and seq % TOKEN_TILE == 0
    assert ns % SAMPLE_BLOCK == 0 and ns <= TOKEN_TILE and cache_k_win.shape[2] == WINDOW
    hp = x_prompt.reshape(nb * seq, D_MODEL)
    hs = x_sample.reshape(ns, D_MODEL)
    ckt = cache_k_win.transpose(0, 1, 3, 4, 2)
    cvt = cache_v_win.transpose(0, 1, 3, 4, 2)
    sct = state_conv.transpose(0, 2, 1, 3)
    qg2 = jnp.tile(q_norm_g, (1, 2))
    kg2 = jnp.tile(k_norm_g, (1, 2))
    kgc = kg2.reshape(DEPTH, 128, 1)
    mixer_w = (sinks, ret_norm_g, qg2, kg2, conv_w, conv_b, conv_ln_g, conv_ln_b, conv_pw)
    prompt_states = [[] for _ in range(4)]
    sample_states = None
    for l in range(DEPTH):
        hp1, zp, hs1, zs = _ffn_in(l, hp, hs, ffn1_norm, mix_norm, ffn1_wg, ffn1_wu, ffn1_wd, w_in)
        cols = jnp.concatenate([zs[:, C_RQ:C_RQ + 512], zs[:, C_AK:C_AK + 256]], axis=1)
        cols = cols.T.reshape(768, ns // SAMPLE_BLOCK, SAMPLE_BLOCK).transpose(1, 0, 2)
        mixs, *sample_states = _mix_sample(l, zs, cols, state_ret, ckt, cvt, sct, sinks,
                                           ret_norm_g, qg2, kg2, kgc, conv_w, conv_b, conv_ln_g,
                                           conv_ln_b, conv_pw, sample_states)
        hp, hs, r1, k1, v1, c1 = _mix_out_ffn(l, nb, zp, hp1, hs1, mixs, ffn2_norm, *mixer_w,
                                              w_out, ffn2_wg, ffn2_wu, ffn2_wd)
        for lst, val in zip(prompt_states, (r1, k1, v1, c1)):
            lst.append(val)
    ret_p, kwin_p, vwin_p, conv_p = (jnp.stack(s) for s in prompt_states)
    ret_s, kwin_s, vwin_s, conv_s = sample_states
    return (hp.reshape(nb, seq, D_MODEL), hs.reshape(ns, 1, D_MODEL),
            ret_p.reshape(DEPTH, nb, RET_HEADS, RET_DK, RET_DV), ret_s,
            kwin_p.reshape(DEPTH, nb, KV_HEADS, HEAD_DIM, WINDOW).transpose(0, 1, 4, 2, 3),
            kwin_s.transpose(0, 1, 4, 2, 3),
            vwin_p.reshape(DEPTH, nb, KV_HEADS, HEAD_DIM, WINDOW).transpose(0, 1, 4, 2, 3),
            vwin_s.transpose(0, 1, 4, 2, 3),
            conv_p, conv_s.transpose(0, 2, 1, 3))
```

```python
import functools
import math

import jax
import jax.numpy as jnp
from jax import lax
from jax.experimental import pallas as pl
from jax.experimental.pallas import tpu as pltpu

F32 = jnp.float32
BF16 = jnp.bfloat16

D_MODEL = 1024
DEPTH = 2
PAST_LEN = 8192
RET_HEADS = 4
RET_DK = 64
RET_DV = 128
RET_CHUNK = 128
ATT_HEADS = 4
KV_HEADS = 2
HEAD_DIM = 64
WINDOW = 128
CONV_CH = 256
CONV_K = 31
FFN_DIM = 2816
EPS = 1e-6
NEG_INF = -1e30

C_RQ, C_RK, C_RV, C_RG = 0, 256, 512, 1024
C_AQ, C_AK, C_AV, C_CA, C_CG = 1536, 1792, 1920, 2048, 2304
IN_DIM = 2560
MIX_DIM = 1024
M_RET, M_ATT, M_CONV = 0, 512, 768

LOG_DECAY = [math.log1p(-2.0 ** (-5 - h)) for h in range(RET_HEADS)]
SLOPES = [2.0 ** (-8.0 * (h + 1) / ATT_HEADS) for h in range(ATT_HEADS)]

V7X_VMEM_LIMIT = 60 * 1024 * 1024
TOKEN_TILE = 512
FFN_COLS = 256
OUT_COLS = 256
STAGE_ROWS = 128
SAMPLE_BLOCK = 8
CONV_ROWS = 64
CONV_PAD = 32
CONV_SPAN = CONV_ROWS + CONV_PAD - 8


def _dot(a, b):
    return jnp.dot(a, b, preferred_element_type=F32)


def _dot_nt(a, b):
    return lax.dot_general(a, b, (((1,), (1,)), ((), ())), preferred_element_type=F32)


def _dot_tn(a, b):
    return lax.dot_general(a, b, (((0,), (0,)), ((), ())), preferred_element_type=F32)


def _silu(x):
    return x * jax.nn.sigmoid(x)


def _rms(x, g):
    return x * lax.rsqrt(jnp.mean(x * x, axis=-1, keepdims=True) + EPS) * g


def _seg_rms(x, g):
    r = lax.broadcasted_iota(jnp.int32, (128, 128), 0)
    c = lax.broadcasted_iota(jnp.int32, (128, 128), 1)
    seg = jnp.where((r < 64) == (c < 64), 1.0 / HEAD_DIM, 0.0).astype(BF16)
    xx = x * x
    hi = xx.astype(BF16)
    lo = (xx - hi.astype(F32)).astype(BF16)
    ms = _dot(hi, seg) + _dot(lo, seg)
    return x * lax.rsqrt(ms + EPS) * g


def _layer_norm(y, g, b):
    mu = jnp.mean(y, axis=-1, keepdims=True)
    d = y - mu
    var = jnp.mean(d * d, axis=-1, keepdims=True)
    return d * lax.rsqrt(var + EPS) * g + b


def _stage_copy(src, l, j, stage, sem, slot, width):
    return pltpu.make_async_copy(src.at[l, pl.ds(j * STAGE_ROWS, STAGE_ROWS), :],
                                 stage.at[slot, :, pl.ds(0, width)], sem.at[slot])


def _swap_inner_heads(t0, t1):
    low = lax.broadcasted_iota(jnp.int32, t0.shape, 1) < 64
    return (jnp.where(low, t0, pltpu.roll(t1, 64, 1)), jnp.where(low, pltpu.roll(t0, 64, 1), t1))


def _stage_weight(src, l, dst, stage, sem, permute_q_cols=False):
    rows, width = dst.shape
    n = rows // STAGE_ROWS
    _stage_copy(src, l, 0, stage, sem, 0, width).start()

    def body(j, carry):
        slot = lax.rem(j, 2)

        @pl.when(j + 1 < n)
        def _():
            _stage_copy(src, l, j + 1, stage, sem, 1 - slot, width).start()

        _stage_copy(src, l, j, stage, sem, slot, width).wait()
        r0 = pl.multiple_of(j * STAGE_ROWS, STAGE_ROWS)
        dst[pl.ds(r0, STAGE_ROWS), :] = stage[slot, :, 0:width].astype(BF16)
        if permute_q_cols:
            t0, t1 = _swap_inner_heads(stage[slot, :, C_AQ:C_AQ + 128],
                                       stage[slot, :, C_AQ + 128:C_AQ + 256])
            dst[pl.ds(r0, STAGE_ROWS), C_AQ:C_AQ + 128] = t0.astype(BF16)
            dst[pl.ds(r0, STAGE_ROWS), C_AQ + 128:C_AQ + 256] = t1.astype(BF16)
        return carry

    lax.fori_loop(0, n, body, 0)


def _stage_w_out(src, l, dst, stage, sem):
    rows, width = dst.shape
    n = rows // STAGE_ROWS
    _stage_copy(src, l, 0, stage, sem, 0, width).start()
    for j in range(n):
        slot = j % 2
        if j + 1 < n:
            _stage_copy(src, l, j + 1, stage, sem, 1 - slot, width).start()
        _stage_copy(src, l, j, stage, sem, slot, width).wait()
        r0 = j * STAGE_ROWS
        if r0 == M_ATT:
            dst[M_ATT:M_ATT + 64, :] = stage[slot, 0:64, 0:width].astype(BF16)
            dst[M_ATT + 128:M_ATT + 192, :] = stage[slot, 64:128, 0:width].astype(BF16)
        elif r0 == M_ATT + 128:
            dst[M_ATT + 64:M_ATT + 128, :] = stage[slot, 0:64, 0:width].astype(BF16)
            dst[M_ATT + 192:M_ATT + 256, :] = stage[slot, 64:128, 0:width].astype(BF16)
        else:
            dst[r0:r0 + STAGE_ROWS, :] = stage[slot, :, 0:width].astype(BF16)


def _run(steps):
    for _ in steps:
        pass


def _interleave(major, minor, major_total, minor_total):
    major_done = minor_done = 0
    for cost in major:
        major_done += cost
        while minor_done * major_total < major_done * minor_total:
            step = next(minor, None)
            if step is None:
                break
            minor_done += step
    _run(minor)


def _matmul_cost(rows, k, n):
    return rows * k * n // (256 * 1024)


def _ffn_steps(x_ref, g, wg_ref, wu_ref, wd_ref, out_ref, xn_scr, a_scr):
    rows = x_ref.shape[0]
    xn_scr[...] = _rms(x_ref[...], g).astype(BF16)
    for c in range(FFN_DIM // FFN_COLS):
        sl = slice(c * FFN_COLS, (c + 1) * FFN_COLS)
        gate = _dot(xn_scr[...], wg_ref[:, sl])
        up = _dot(xn_scr[...], wu_ref[:, sl])
        a_scr[:, sl] = (_silu(gate) * up).astype(BF16)
        yield 2 * _matmul_cost(rows, D_MODEL, FFN_COLS)
    for c in range(D_MODEL // OUT_COLS):
        sl = slice(c * OUT_COLS, (c + 1) * OUT_COLS)
        out_ref[:, sl] = x_ref[:, sl] + 0.5 * _dot(a_scr[...], wd_ref[:, sl])
        yield _matmul_cost(rows, FFN_DIM, OUT_COLS)


def _out_ffn_steps(l, x1_ref, mix_ref, x3_ref, g_ref, wout_b, wg_b, wu_b, wd_b, x2, xn, a):
    rows = x1_ref.shape[0]
    mix = mix_ref[...].astype(BF16)
    for c in range(D_MODEL // OUT_COLS):
        sl = slice(c * OUT_COLS, (c + 1) * OUT_COLS)
        x2[:, sl] = x1_ref[:, sl] + _dot(mix, wout_b[:, sl])
        yield _matmul_cost(rows, MIX_DIM, OUT_COLS)
    yield from _ffn_steps(x2, g_ref[l:l + 1, :], wg_b, wu_b, wd_b, x3_ref, xn, a)


def _out_ffn_cost(rows):
    return _matmul_cost(rows, MIX_DIM * D_MODEL + 3 * D_MODEL * FFN_DIM, 1)


def _ffn_in_steps(l, x_ref, x1_ref, z_ref, g1_ref, g2_ref, wg_b, wu_b, wd_b, win_b, xn, a):
    rows = x_ref.shape[0]
    yield from _ffn_steps(x_ref, g1_ref[l:l + 1, :], wg_b, wu_b, wd_b, x1_ref, xn, a)
    xn[...] = _rms(x1_ref[...], g2_ref[l:l + 1, :]).astype(BF16)
    for c in range(IN_DIM // OUT_COLS):
        sl = slice(c * OUT_COLS, (c + 1) * OUT_COLS)
        z_ref[:, sl] = _dot(xn[...], win_b[:, sl])
        yield _matmul_cost(rows, D_MODEL, OUT_COLS)


def _ffn_in_cost(rows):
    return _matmul_cost(rows, 3 * D_MODEL * FFN_DIM + D_MODEL * IN_DIM, 1)


def _ffn_in_kernel(l, n_tiles, tps, xp_ref, xs_ref, g1_ref, g2_ref, cw_ref, cb_ref, lng_ref, lnb_ref,
                   pw_ref, wg_hbm, wu_hbm, wd_hbm, win_hbm,
                   x1p_ref, zp_ref, mixc_ref, conv_ref, x1s_ref, zs_ref,
                   wg_b, wu_b, wd_b, win_b, stage, sem, xn_scr, a_scr, zcg, ubuf, ush, pw_b):
    i = pl.program_id(0)
    cprev = lax.rem(i + tps - 1, tps)
    dense_w = (g1_ref, g2_ref, wg_b, wu_b, wd_b, win_b)

    @pl.when(i == 0)
    def _():
        _stage_weight(wg_hbm, l, wg_b, stage, sem)
        _stage_weight(wu_hbm, l, wu_b, stage, sem)
        _stage_weight(wd_hbm, l, wd_b, stage, sem)
        _stage_weight(win_hbm, l, win_b, stage, sem, permute_q_cols=True)
        pw_b[...] = pw_ref[l].astype(BF16)

    @pl.when((i > 0) & (cprev == 0))
    def _():
        ubuf[0:CONV_PAD, :] = jnp.zeros((CONV_PAD, CONV_CH), F32)

    @pl.when((i > 0) & (cprev > 0))
    def _():
        ubuf[0:CONV_PAD, :] = ubuf[TOKEN_TILE:TOKEN_TILE + CONV_PAD, :]

    def dense():
        return _ffn_in_steps(l, xp_ref, x1p_ref, zp_ref, *dense_w, xn_scr, a_scr)

    def conv():
        return _conv_tile_steps(l, zcg, mixc_ref, cw_ref, cb_ref, lng_ref, lnb_ref, ubuf, ush, pw_b)

    @pl.when(i == 0)
    def _():
        _run(dense())

    @pl.when((i > 0) & (i < n_tiles))
    def _():
        _interleave(dense(), conv(), _ffn_in_cost(TOKEN_TILE), CONV_TILE_COST)

    @pl.when(i == n_tiles)
    def _():
        _run(conv())
        ns = xs_ref.shape[0]
        _run(_ffn_in_steps(l, xs_ref, x1s_ref, zs_ref, *dense_w, xn_scr.at[0:ns], a_scr.at[0:ns]))

    @pl.when(i < n_tiles)
    def _():
        zcg[...] = zp_ref[:, C_CA:C_CA + 2 * CONV_CH]

    @pl.when((i > 0) & (cprev == tps - 1))
    def _():
        conv_ref[0] = ubuf[CONV_PAD + TOKEN_TILE - (CONV_K - 1):CONV_PAD + TOKEN_TILE, :]


def _stage_scratch():
    return [pltpu.VMEM((2, STAGE_ROWS, FFN_DIM), F32), pltpu.SemaphoreType.DMA((2,))]


def _ffn_in(l, nb, xp, xs, g1, g2, cw, cb, lng, lnb, pw, wg, wu, wd, win):
    rows, ns = xp.shape[0], xs.shape[0]
    n_tiles = rows // TOKEN_TILE
    tps = n_tiles // nb
    cur = lambda w: pl.BlockSpec((TOKEN_TILE, w), lambda i: (jnp.minimum(i, n_tiles - 1), 0))
    prev = lambda w: pl.BlockSpec((TOKEN_TILE, w), lambda i: (jnp.maximum(i - 1, 0), 0))
    once = lambda w: pl.BlockSpec((ns, w), lambda i: (0, 0), pipeline_mode=pl.Buffered(1))
    whole = lambda a: pl.BlockSpec(a.shape, lambda i: (0,) * a.ndim)
    hbm = pl.BlockSpec(memory_space=pl.ANY)
    return pl.pallas_call(
        functools.partial(_ffn_in_kernel, l, n_tiles, tps),
        grid=(n_tiles + 1,),
        in_specs=[cur(D_MODEL), once(D_MODEL), whole(g1), whole(g2), whole(cw), whole(cb),
                  whole(lng), whole(lnb), whole(pw), hbm, hbm, hbm, hbm],
        out_specs=[cur(D_MODEL), cur(IN_DIM), prev(CONV_CH),
                   pl.BlockSpec((1, CONV_K - 1, CONV_CH),
                                lambda i: (jnp.maximum(i - 1, 0) // tps, 0, 0)),
                   once(D_MODEL), once(IN_DIM)],
        out_shape=[jax.ShapeDtypeStruct((rows, D_MODEL), F32),
                   jax.ShapeDtypeStruct((rows, IN_DIM), F32),
                   jax.ShapeDtypeStruct((rows, CONV_CH), BF16),
                   jax.ShapeDtypeStruct((nb, CONV_K - 1, CONV_CH), F32),
                   jax.ShapeDtypeStruct((ns, D_MODEL), F32),
                   jax.ShapeDtypeStruct((ns, IN_DIM), F32)],
        scratch_shapes=[pltpu.VMEM((D_MODEL, FFN_DIM), BF16), pltpu.VMEM((D_MODEL, FFN_DIM), BF16),
                        pltpu.VMEM((FFN_DIM, D_MODEL), BF16), pltpu.VMEM((D_MODEL, IN_DIM), BF16),
                        *_stage_scratch(),
                        pltpu.VMEM((TOKEN_TILE, D_MODEL), BF16),
                        pltpu.VMEM((TOKEN_TILE, FFN_DIM), BF16),
                        pltpu.VMEM((TOKEN_TILE, 2 * CONV_CH), F32),
                        pltpu.VMEM((CONV_PAD + TOKEN_TILE, CONV_CH), F32),
                        pltpu.VMEM((2, 7, CONV_SPAN, CONV_CH), F32),
                        pltpu.VMEM((CONV_CH, CONV_CH), BF16)],
        compiler_params=pltpu.CompilerParams(dimension_semantics=("arbitrary",),
                                             vmem_limit_bytes=V7X_VMEM_LIMIT),
        name="ffn_in",
    )(xp, xs, g1, g2, cw, cb, lng, lnb, pw, wg, wu, wd, win)


def _mix_constants(dmask, qdec, kdec, abias):
    T = RET_CHUNK
    row_f = lax.broadcasted_iota(jnp.int32, (T, T), 0).astype(F32)
    col_f = lax.broadcasted_iota(jnp.int32, (T, T), 1).astype(F32)
    low = lax.broadcasted_iota(jnp.int32, (T, T), 1) < 64
    for h in range(RET_HEADS):
        lg = LOG_DECAY[h]
        diff = row_f - col_f
        dmask[h] = jnp.where(diff >= 0, jnp.exp(lg * jnp.maximum(diff, 0.0)), 0.0)
        qdec[h] = jnp.exp(lg * (row_f + 1.0))
    for p in range(RET_HEADS // 2):
        lgp = jnp.where(low, LOG_DECAY[2 * p], LOG_DECAY[2 * p + 1])
        kdec[p] = jnp.exp(lgp * (T - 1.0 - row_f))
    qpos = lax.broadcasted_iota(jnp.int32, (T, 2 * T), 0) + T
    kpos = lax.broadcasted_iota(jnp.int32, (T, 2 * T), 1)
    dist = (qpos - kpos).astype(F32)
    for h in range(ATT_HEADS):
        abias[h] = jnp.where((dist >= 0) & (dist < WINDOW), -SLOPES[h] * dist, NEG_INF)


def _mix_tile_steps(l, first, z_ref, mix_ref, sinks_ref, rng_ref, qg_ref, kg_ref,
                    s_scr, kbuf, vbuf, qbuf, klast, dmask, qdec, kdec, abias):
    T = RET_CHUNK
    NCH = TOKEN_TILE // T
    row_i = lax.broadcasted_iota(jnp.int32, (T, T), 0)
    low = lax.broadcasted_iota(jnp.int32, (T, T), 1) < 64

    def zc(r0, nrows, a, w):
        return z_ref[r0:r0 + nrows, a:a + w]

    def seg_mean_sq(x):
        low_x = lax.broadcasted_iota(jnp.int32, x.shape, 1) < 64
        xx = x * x
        s_lo = jnp.sum(jnp.where(low_x, xx, 0.0), axis=-1, keepdims=True)
        s_hi = jnp.sum(jnp.where(low_x, 0.0, xx), axis=-1, keepdims=True)
        return jnp.where(low_x, s_lo, s_hi) * (1.0 / HEAD_DIM)

    def retention_unit(p, j):
        s_cur = s_scr[p]
        r0 = T * j
        qp = zc(r0, T, C_RQ + 128 * p, 128)
        kp = zc(r0, T, C_RK + 128 * p, 128) * (RET_DK ** -0.5)
        vb = zc(r0, T, C_RV + 256 * p, 256).astype(BF16)
        kb = kp.astype(BF16)
        sb = s_cur.astype(BF16)
        qm = jnp.concatenate([jnp.where(low, qp, 0.0), jnp.where(low, 0.0, qp)], axis=0).astype(BF16)
        a = _dot_nt(qm, kb)
        qs = _dot(qm, sb)
        upd = _dot_tn((kp * kdec[p]).astype(BF16), vb)
        yield RET_COST // 4
        ab = (a * dmask[2 * p:2 * p + 2].reshape(2 * T, T)).astype(BF16)
        yield RET_COST // 4
        o = _dot(ab, vb)
        yield RET_COST // 4
        for e in range(2):
            h = 2 * p + e
            oh = _rms(o[T * e:T * e + T, 128 * e:128 * e + 128] + qs[T * e:T * e + T, :] * qdec[h],
                      rng_ref[l, h:h + 1, :])
            mix_ref[r0:r0 + T, M_RET + 128 * h:M_RET + 128 * h + 128] = (
                oh * _silu(zc(r0, T, C_RG + 128 * h, 128))).astype(BF16)
        top = row_i < 64
        new = jnp.where(top, upd[:, 0:128], upd[:, 128:256])
        cd = jnp.where(top, math.exp(LOG_DECAY[2 * p] * T), math.exp(LOG_DECAY[2 * p + 1] * T))
        s_scr[p] = s_cur * cd + new
        yield RET_COST // 4

    def attention_prep():
        ak = zc(0, TOKEN_TILE, C_AK, 128)
        aq = [zc(0, TOKEN_TILE, C_AQ + 128 * t, 128) for t in range(2)]
        ms_k = seg_mean_sq(ak)
        ms_q = [seg_mean_sq(aq[t]) for t in range(2)]
        yield ATT_PREP_COST // 2
        kn = ak * lax.rsqrt(ms_k + EPS) * kg_ref[l:l + 1, :]
        klast[...] = kn[TOKEN_TILE - T:TOKEN_TILE, :]
        kbuf[T:T + TOKEN_TILE, :] = kn.astype(BF16)
        vbuf[T:T + TOKEN_TILE, :] = zc(0, TOKEN_TILE, C_AV, 128).astype(BF16)
        for t in range(2):
            qt = aq[t] * lax.rsqrt(ms_q[t] + EPS) * qg_ref[l:l + 1, :] * (HEAD_DIM ** -0.5)
            for j in range(NCH):
                qj = qt[T * j:T * j + T, :]
                qbuf[j, (2 * t) * T:(2 * t + 1) * T, :] = jnp.where(low, qj, 0.0).astype(BF16)
                qbuf[j, (2 * t + 1) * T:(2 * t + 2) * T, :] = jnp.where(low, 0.0, qj).astype(BF16)
        yield ATT_PREP_COST // 2

    def attention_unit(j):
        r0 = T * j
        kcat = kbuf[r0:r0 + 2 * T, :]
        vcat = vbuf[r0:r0 + 2 * T, :]
        s = _dot_nt(qbuf[j], kcat)
        yield ATT_COST // 2
        probs = []
        for t in range(2):
            for kvh in range(KV_HEADS):
                head = 2 * kvh + t
                sink = sinks_ref[l, head]
                bias = abias[head]
                if j == 0:
                    no_prev = lax.broadcasted_iota(jnp.int32, (T, 2 * T), 1) < jnp.where(first, T, 0)
                    bias = jnp.where(no_prev, NEG_INF, bias)
                blk = 2 * t + kvh
                sb = s[T * blk:T * blk + T, :] + bias
                m = jnp.maximum(jnp.max(sb, axis=-1, keepdims=True), sink)
                e = jnp.exp(sb - m)
                inv = 1.0 / (jnp.sum(e, axis=-1, keepdims=True) + jnp.exp(sink - m))
                probs.append((e * inv).astype(BF16))
            yield ATT_COST
        o = _dot(jnp.concatenate(probs, axis=0), vcat)
        for t in range(2):
            mix_ref[r0:r0 + T, M_ATT + 128 * t:M_ATT + 128 * t + 128] = jnp.where(
                low, o[T * 2 * t:T * (2 * t + 1), :], o[T * (2 * t + 1):T * (2 * t + 2), :]
            ).astype(BF16)
        yield ATT_COST // 2

    def ret_lane():
        for j in range(NCH):
            for p in range(RET_HEADS // 2):
                yield from retention_unit(p, j)

    def att_lane():
        yield from attention_prep()
        for j in range(NCH):
            yield from attention_unit(j)

    live = [att_lane(), ret_lane()]
    while live:
        for lane in list(live):
            cost = next(lane, None)
            if cost is None:
                live.remove(lane)
            else:
                yield cost


def _conv_tile_steps(l, zcg_ref, out_ref, cw_ref, cb_ref, lng_ref, lnb_ref, ubuf, ush, pw_b):
    ubuf[CONV_PAD:CONV_PAD + TOKEN_TILE, :] = (
        zcg_ref[:, 0:CONV_CH] * jax.nn.sigmoid(zcg_ref[:, CONV_CH:2 * CONV_CH]))
    yield CONV_PREP_COST
    for rb in range(TOKEN_TILE // CONV_ROWS):
        base = rb * CONV_ROWS
        sh = ush.at[rb % 2]
        for r in range(1, 8):
            sh[r - 1] = ubuf[base + r:base + r + CONV_SPAN, :]
        acc = None
        for k in range(CONV_K):
            off = CONV_PAD - (CONV_K - 1) + k
            a8, r = off // 8 * 8, off % 8
            src = (ubuf[base + a8:base + a8 + CONV_ROWS, :] if r == 0
                   else sh[r - 1, a8:a8 + CONV_ROWS, :])
            term = cw_ref[l, k:k + 1, :] * src
            acc = term if acc is None else acc + term
        y = _silu(_layer_norm(acc + cb_ref[l:l + 1, :], lng_ref[l:l + 1, :],
                              lnb_ref[l:l + 1, :])).astype(BF16)
        yield CONV_COST - CONV_COST // 8
        out_ref[base:base + CONV_ROWS, :] = _dot(y, pw_b[...]).astype(BF16)
        yield CONV_COST // 8


RET_COST, ATT_PREP_COST, ATT_COST, CONV_PREP_COST, CONV_COST = 160, 250, 220, 200, 410
MIX_COST = ((RET_HEADS // 2) * (TOKEN_TILE // RET_CHUNK) * RET_COST + ATT_PREP_COST
            + 3 * (TOKEN_TILE // RET_CHUNK) * ATT_COST)
CONV_TILE_COST = CONV_PREP_COST + (TOKEN_TILE // CONV_ROWS) * CONV_COST


def _mix_out_ffn_kernel(l, n_tiles, tps, sinks_ref, z_ref, x1p_ref, mixc_ref, x1s_ref, mixs_ref, g_ref,
                        rng_ref, qg_ref, kg_ref, wout_hbm, wg_hbm, wu_hbm, wd_hbm,
                        x3p_ref, x3s_ref, ret_ref, kwin_ref, vwin_ref,
                        wout_b, wg_b, wu_b, wd_b, stage, sem, x2_scr, xn_scr, a_scr, mix_new, mix_old,
                        s_scr, kbuf, vbuf, qbuf, klast, dmask, qdec, kdec, abias):
    i = pl.program_id(0)
    c = lax.rem(i, tps)
    T = RET_CHUNK
    mix_args = (sinks_ref, rng_ref, qg_ref, kg_ref,
                s_scr, kbuf, vbuf, qbuf, klast, dmask, qdec, kdec, abias)
    dense_w = (g_ref, wout_b, wg_b, wu_b, wd_b)

    @pl.when(i == 0)
    def _():
        _stage_w_out(wout_hbm, l, wout_b, stage, sem)
        _stage_weight(wg_hbm, l, wg_b, stage, sem)
        _stage_weight(wu_hbm, l, wu_b, stage, sem)
        _stage_weight(wd_hbm, l, wd_b, stage, sem)
        _mix_constants(dmask, qdec, kdec, abias)

    @pl.when(c == 0)
    def _():
        s_scr[...] = jnp.zeros_like(s_scr)
        kbuf[0:T, :] = jnp.zeros((T, 128), BF16)
        vbuf[0:T, :] = jnp.zeros((T, 128), BF16)

    @pl.when(c > 0)
    def _():
        kbuf[0:T, :] = kbuf[TOKEN_TILE:TOKEN_TILE + T, :]
        vbuf[0:T, :] = vbuf[TOKEN_TILE:TOKEN_TILE + T, :]

    @pl.when(i > 0)
    def _():
        mix_old[:, 0:M_CONV] = mix_new[:, 0:M_CONV]
        mix_old[:, M_CONV:MIX_DIM] = mixc_ref[...]

    def mixers():
        return _mix_tile_steps(l, c == 0, z_ref, mix_new, *mix_args)

    def dense():
        return _out_ffn_steps(l, x1p_ref, mix_old, x3p_ref, *dense_w, x2_scr, xn_scr, a_scr)

    @pl.when(i == 0)
    def _():
        _run(mixers())

    @pl.when((i > 0) & (i < n_tiles))
    def _():
        _interleave(dense(), mixers(), _out_ffn_cost(TOKEN_TILE), MIX_COST)

    @pl.when(i == n_tiles)
    def _():
        _run(dense())
        ns = x1s_ref.shape[0]
        _run(_out_ffn_steps(l, x1s_ref, mixs_ref, x3s_ref, *dense_w,
                            x2_scr.at[0:ns], xn_scr.at[0:ns], a_scr.at[0:ns]))

    @pl.when((c == tps - 1) & (i < n_tiles))
    def _():
        ret_ref[0] = s_scr[...]
        kwin_ref[0] = klast[...].T
        vwin_ref[0] = z_ref[TOKEN_TILE - T:TOKEN_TILE, C_AV:C_AV + 128].T


def _mix_out_ffn(l, nb, z, x1p, mixc, x1s, mixs, g, sinks, rng, qg2, kg2, wout, wg, wu, wd):
    rows, ns = x1p.shape[0], x1s.shape[0]
    n_tiles = rows // TOKEN_TILE
    tps = n_tiles // nb
    T = RET_CHUNK
    cur = lambda w: pl.BlockSpec((TOKEN_TILE, w), lambda i: (jnp.minimum(i, n_tiles - 1), 0))
    prev = lambda w: pl.BlockSpec((TOKEN_TILE, w), lambda i: (jnp.maximum(i - 1, 0), 0))
    once = lambda w: pl.BlockSpec((ns, w), lambda i: (0, 0), pipeline_mode=pl.Buffered(1))
    whole = lambda a: pl.BlockSpec(a.shape, lambda i: (0,) * a.ndim)
    per_seq = lambda shape: pl.BlockSpec(
        (1,) + shape, lambda i: (jnp.minimum(i, n_tiles - 1) // tps,) + (0,) * len(shape))
    hbm = pl.BlockSpec(memory_space=pl.ANY)
    return pl.pallas_call(
        functools.partial(_mix_out_ffn_kernel, l, n_tiles, tps),
        grid=(n_tiles + 1,),
        in_specs=[pl.BlockSpec(memory_space=pltpu.SMEM), cur(IN_DIM), prev(D_MODEL), prev(CONV_CH),
                  once(D_MODEL), once(MIX_DIM), whole(g),
                  whole(rng), whole(qg2), whole(kg2), hbm, hbm, hbm, hbm],
        out_specs=[prev(D_MODEL), once(D_MODEL),
                   per_seq((2, 128, RET_DV)), per_seq((128, WINDOW)), per_seq((128, WINDOW))],
        out_shape=[jax.ShapeDtypeStruct((rows, D_MODEL), F32),
                   jax.ShapeDtypeStruct((ns, D_MODEL), F32),
                   jax.ShapeDtypeStruct((nb, 2, 128, RET_DV), F32),
                   jax.ShapeDtypeStruct((nb, 128, WINDOW), F32),
                   jax.ShapeDtypeStruct((nb, 128, WINDOW), F32)],
        scratch_shapes=[pltpu.VMEM((MIX_DIM, D_MODEL), BF16),
                        pltpu.VMEM((D_MODEL, FFN_DIM), BF16), pltpu.VMEM((D_MODEL, FFN_DIM), BF16),
                        pltpu.VMEM((FFN_DIM, D_MODEL), BF16),
                        *_stage_scratch(),
                        pltpu.VMEM((TOKEN_TILE, D_MODEL), F32),
                        pltpu.VMEM((TOKEN_TILE, D_MODEL), BF16),
                        pltpu.VMEM((TOKEN_TILE, FFN_DIM), BF16),
                        pltpu.VMEM((TOKEN_TILE, MIX_DIM), BF16),
                        pltpu.VMEM((TOKEN_TILE, MIX_DIM), BF16),
                        pltpu.VMEM((2, 128, RET_DV), F32),
                        pltpu.VMEM((T + TOKEN_TILE, 128), BF16),
                        pltpu.VMEM((T + TOKEN_TILE, 128), BF16),
                        pltpu.VMEM((TOKEN_TILE // T, 2 * KV_HEADS * T, 128), BF16),
                        pltpu.VMEM((T, 128), F32),
                        pltpu.VMEM((RET_HEADS, T, T), F32), pltpu.VMEM((RET_HEADS, T, T), F32),
                        pltpu.VMEM((2, T, T), F32), pltpu.VMEM((ATT_HEADS, T, 2 * T), F32)],
        compiler_params=pltpu.CompilerParams(dimension_semantics=("arbitrary",),
                                             vmem_limit_bytes=V7X_VMEM_LIMIT),
        name="mix_out_ffn",
    )(sinks, z, x1p, mixc, x1s, mixs, g, rng, qg2, kg2, wout, wg, wu, wd)


def _mix_sample_kernel(l, sinks_ref, z_ref, col_ref, s_ref, ck_ref, cv_ref, sc_ref,
                       rng_ref, qg_ref, kg_ref, kgc_ref, cw_ref, cb_ref, lng_ref, lnb_ref, pw_ref,
                       *rest):
    mix_ref, so_ref, cko_ref, cvo_ref, sco_ref, o_scr, oa_scr = rest[-7:]
    NB = SAMPLE_BLOCK
    P = WINDOW

    def put(ref, idx, val):
        for slab in range(ref.shape[0]):
            ref[(slab,) + idx] = val

    for bl in range(NB):
        for h in range(RET_HEADS):
            gamma = math.exp(LOG_DECAY[h])
            r0 = RET_DK * h
            S = s_ref[0, bl, h]
            qc = col_ref[0, r0:r0 + RET_DK, bl:bl + 1]
            kc = col_ref[0, 256 + r0:256 + r0 + RET_DK, bl:bl + 1] * (RET_DK ** -0.5)
            v = z_ref[bl:bl + 1, C_RV + 128 * h:C_RV + 128 * h + 128]
            qk = jnp.sum(qc * kc, axis=0, keepdims=True)
            o_scr[bl:bl + 1, 128 * h:128 * h + 128] = (
                gamma * jnp.sum(qc * S, axis=0, keepdims=True) + qk * v)
            put(so_ref, (bl, h), gamma * S + kc * v)
    for h in range(RET_HEADS):
        o = _rms(o_scr[:, 128 * h:128 * h + 128], rng_ref[l, h:h + 1, :])
        mix_ref[:, M_RET + 128 * h:M_RET + 128 * h + 128] = (
            o * _silu(z_ref[:, C_RG + 128 * h:C_RG + 128 * h + 128]))

    q_tiles = [_seg_rms(z_ref[:, C_AQ + 128 * t:C_AQ + 128 * t + 128], qg_ref[l:l + 1, :])
               * (HEAD_DIM ** -0.5) for t in range(2)]
    kn = _seg_rms(z_ref[:, C_AK:C_AK + 128], kg_ref[l:l + 1, :])
    vn = z_ref[:, C_AV:C_AV + 128]
    akc = col_ref[0, 512:640, :]
    vnc = col_ref[0, 640:768, :]
    ms = jnp.concatenate(
        [jnp.broadcast_to(jnp.mean(akc[64 * s:64 * s + 64, :] ** 2, axis=0, keepdims=True), (64, NB))
         for s in range(KV_HEADS)], axis=0)
    knc = akc * lax.rsqrt(ms + EPS) * kgc_ref[l]
    rid = lax.broadcasted_iota(jnp.int32, (8, 128), 0)
    lane = lax.broadcasted_iota(jnp.int32, (8, 128), 1)
    sel = ((rid % 2 == 0) == (lane < 64)) & (rid < ATT_HEADS)
    rcol = lax.broadcasted_iota(jnp.int32, (8, 1), 0)
    heads = [2 * (r % 2) + r // 2 for r in range(ATT_HEADS)]
    slope_col = jnp.zeros((8, 1), F32)
    sink_col = jnp.zeros((8, 1), F32)
    for r, hd in enumerate(heads):
        slope_col = jnp.where(rcol == r, SLOPES[hd], slope_col)
        sink_col = jnp.where(rcol == r, sinks_ref[l, hd], sink_col)
    dist = float(P) - lane.astype(F32)
    key_ok = (dist < float(WINDOW)) & (lane + (PAST_LEN - P) >= 0)
    newest = lax.broadcasted_iota(jnp.int32, (128, P), 1) == P - 1
    for bl in range(NB):
        KT = ck_ref[0, bl].reshape(128, P)
        VT = cv_ref[0, bl].reshape(128, P)
        qrows = jnp.where(rid < 2, q_tiles[0][bl:bl + 1, :], q_tiles[1][bl:bl + 1, :])
        q4 = jnp.where(sel, qrows, 0.0)
        s = _dot(q4.astype(BF16), KT.astype(BF16))
        s = jnp.where(key_ok, s - slope_col * dist, NEG_INF)
        s_new = jnp.sum(q4 * kn[bl:bl + 1, :], axis=-1, keepdims=True)
        m = jnp.maximum(jnp.maximum(jnp.max(s, axis=-1, keepdims=True), s_new), sink_col)
        e = jnp.exp(s - m)
        e_new = jnp.exp(s_new - m)
        inv = 1.0 / (jnp.sum(e, axis=-1, keepdims=True) + e_new + jnp.exp(sink_col - m))
        o = _dot_nt((e * inv).astype(BF16), VT.astype(BF16)) + (e_new * inv) * vn[bl:bl + 1, :]
        for t in range(2):
            oa_scr[bl:bl + 1, 128 * t:128 * t + 128] = jnp.where(
                lane[0:1, :] < 64, o[2 * t:2 * t + 1, :], o[2 * t + 1:2 * t + 2, :])
        put(cko_ref, (bl,), jnp.where(newest, knc[:, bl:bl + 1], pltpu.roll(KT, P - 1, 1)).reshape(
            KV_HEADS, HEAD_DIM, P))
        put(cvo_ref, (bl,), jnp.where(newest, vnc[:, bl:bl + 1], pltpu.roll(VT, P - 1, 1)).reshape(
            KV_HEADS, HEAD_DIM, P))
    mix_ref[:, M_ATT:M_ATT + 256] = oa_scr[...]

    u = z_ref[:, C_CA:C_CA + CONV_CH] * jax.nn.sigmoid(z_ref[:, C_CG:C_CG + CONV_CH])
    KT1 = CONV_K - 1
    y = cw_ref[l, KT1:KT1 + 1, :] * u
    for k in range(KT1):
        y = y + cw_ref[l, k:k + 1, :] * sc_ref[0, k]
        if k > 0:
            put(sco_ref, (k - 1,), sc_ref[0, k])
    put(sco_ref, (KT1 - 1,), u)
    y = _silu(_layer_norm(y + cb_ref[l:l + 1, :], lng_ref[l:l + 1, :], lnb_ref[l:l + 1, :]))
    mix_ref[:, M_CONV:M_CONV + CONV_CH] = _dot(y.astype(BF16), pw_ref[l].astype(BF16))


def _mix_sample(l, z, cols, s, ckt, cvt, sct, sinks, rng, qg2, kg2, kgc, cw, cb, lng, lnb, pw, prev):
    ns = z.shape[0]
    NB = SAMPLE_BLOCK
    whole = lambda a: pl.BlockSpec(a.shape, lambda i: (0,) * a.ndim)

    def specs(slabs, at):
        return [pl.BlockSpec((slabs, NB, RET_HEADS, RET_DK, RET_DV), lambda i: (at, i, 0, 0, 0)),
                pl.BlockSpec((slabs, NB, KV_HEADS, HEAD_DIM, WINDOW), lambda i: (at, i, 0, 0, 0)),
                pl.BlockSpec((slabs, NB, KV_HEADS, HEAD_DIM, WINDOW), lambda i: (at, i, 0, 0, 0)),
                pl.BlockSpec((slabs, CONV_K - 1, NB, CONV_CH), lambda i: (at, 0, i, 0))]

    state_specs = specs(1, l)
    out_state_specs = specs(s.shape[0], 0) if prev is None else state_specs
    in_specs = [pl.BlockSpec(memory_space=pltpu.SMEM),
                pl.BlockSpec((NB, IN_DIM), lambda i: (i, 0)),
                pl.BlockSpec((1, 768, NB), lambda i: (i, 0, 0)),
                *state_specs,
                whole(rng), whole(qg2), whole(kg2), whole(kgc), whole(cw), whole(cb), whole(lng),
                whole(lnb), whole(pw)]
    args = [sinks, z, cols, s, ckt, cvt, sct, rng, qg2, kg2, kgc, cw, cb, lng, lnb, pw]
    aliases = {}
    if prev is not None:
        for k, a in enumerate(prev):
            aliases[len(args)] = 1 + k
            in_specs.append(pl.BlockSpec(memory_space=pl.ANY))
            args.append(a)
    return pl.pallas_call(
        functools.partial(_mix_sample_kernel, l),
        grid=(ns // NB,),
        in_specs=in_specs,
        out_specs=[pl.BlockSpec((NB, MIX_DIM), lambda i: (i, 0)), *out_state_specs],
        out_shape=[jax.ShapeDtypeStruct((ns, MIX_DIM), F32),
                   jax.ShapeDtypeStruct(s.shape, F32), jax.ShapeDtypeStruct(ckt.shape, F32),
                   jax.ShapeDtypeStruct(cvt.shape, F32), jax.ShapeDtypeStruct(sct.shape, F32)],
        scratch_shapes=[pltpu.VMEM((NB, 512), F32), pltpu.VMEM((NB, 256), F32)],
        input_output_aliases=aliases,
        compiler_params=pltpu.CompilerParams(dimension_semantics=("arbitrary",)),
        name="mix_sample",
    )(*args)


def kernel(x_prompt, x_sample, state_ret, cache_k_win, cache_v_win, state_conv, ffn1_norm, ffn1_wg, ffn1_wu, ffn1_wd, mix_norm, w_in, ret_norm_g, q_norm_g, k_norm_g, sinks, conv_w, conv_b, conv_ln_g, conv_ln_b, conv_pw, w_out, ffn2_norm, ffn2_wg, ffn2_wu, ffn2_wd):
    nb, seq, _ = x_prompt.shape
    ns = x_sample.shape[0]
    assert x_sample.shape[1] == 1 and seq % TOKEN_TILE == 0
    assert ns % SAMPLE_BLOCK == 0 and ns <= TOKEN_TILE and cache_k_win.shape[2] == WINDOW
    hp = x_prompt.reshape(nb * seq, D_MODEL)
    hs = x_sample.reshape(ns, D_MODEL)
    ckt = cache_k_win.transpose(0, 1, 3, 4, 2)
    cvt = cache_v_win.transpose(0, 1, 3, 4, 2)
    sct = state_conv.transpose(0, 2, 1, 3)
    qg2 = jnp.tile(q_norm_g, (1, 2))
    kg2 = jnp.tile(k_norm_g, (1, 2))
    kgc = kg2.reshape(DEPTH, 128, 1)
    conv_params = (conv_w, conv_b, conv_ln_g, conv_ln_b, conv_pw)
    prompt_states = [[] for _ in range(4)]
    sample_states = None
    for l in range(DEPTH):
        hp1, zp, mixc, c1, hs1, zs = _ffn_in(l, nb, hp, hs, ffn1_norm, mix_norm, *conv_params,
                                             ffn1_wg, ffn1_wu, ffn1_wd, w_in)
        cols = jnp.concatenate([zs[:, C_RQ:C_RQ + 512], zs[:, C_AK:C_AK + 256]], axis=1)
        cols = cols.T.reshape(768, ns // SAMPLE_BLOCK, SAMPLE_BLOCK).transpose(1, 0, 2)
        mixs, *sample_states = _mix_sample(l, zs, cols, state_ret, ckt, cvt, sct, sinks,
                                           ret_norm_g, qg2, kg2, kgc, conv_w, conv_b, conv_ln_g,
                                           conv_ln_b, conv_pw, sample_states)
        hp, hs, r1, k1, v1 = _mix_out_ffn(l, nb, zp, hp1, mixc, hs1, mixs, ffn2_norm, sinks,
                                          ret_norm_g, qg2, kg2, w_out, ffn2_wg, ffn2_wu, ffn2_wd)
        for lst, val in zip(prompt_states, (r1, k1, v1, c1)):
            lst.append(val)
    ret_p, kwin_p, vwin_p, conv_p = (jnp.stack(s) for s in prompt_states)
    ret_s, kwin_s, vwin_s, conv_s = sample_states
    return (hp.reshape(nb, seq, D_MODEL), hs.reshape(ns, 1, D_MODEL),
            ret_p.reshape(DEPTH, nb, RET_HEADS, RET_DK, RET_DV), ret_s,
            kwin_p.reshape(DEPTH, nb, KV_HEADS, HEAD_DIM, WINDOW).transpose(0, 1, 4, 2, 3),
            kwin_s.transpose(0, 1, 4, 2, 3),
            vwin_p.reshape(DEPTH, nb, KV_HEADS, HEAD_DIM, WINDOW).transpose(0, 1, 4, 2, 3),
            vwin_s.transpose(0, 1, 4, 2, 3),
            conv_p, conv_s.transpose(0, 2, 1, 3))
```

```python
import functools
import math

import jax
import jax.numpy as jnp
from jax import lax
from jax.experimental import pallas as pl
from jax.experimental.pallas import tpu as pltpu

F32 = jnp.float32
BF16 = jnp.bfloat16

D_MODEL = 1024
DEPTH = 2
PAST_LEN = 8192
RET_HEADS = 4
RET_DK = 64
RET_DV = 128
RET_CHUNK = 128
ATT_HEADS = 4
KV_HEADS = 2
HEAD_DIM = 64
WINDOW = 128
CONV_CH = 256
CONV_K = 31
FFN_DIM = 2816
EPS = 1e-6
NEG_INF = -1e30

C_RQ, C_RK, C_RV, C_RG = 0, 256, 512, 1024
C_AQ, C_AK, C_AV, C_CA, C_CG = 1536, 1792, 1920, 2048, 2304
IN_DIM = 2560
MIX_DIM = 1024
M_RET, M_ATT, M_CONV = 0, 512, 768

LOG_DECAY = [math.log1p(-2.0 ** (-5 - h)) for h in range(RET_HEADS)]
SLOPES = [2.0 ** (-8.0 * (h + 1) / ATT_HEADS) for h in range(ATT_HEADS)]

V7X_VMEM_LIMIT = 60 * 1024 * 1024
TOKEN_TILE = 512
FFN_COLS = 256
OUT_COLS = 256
STAGE_ROWS = 128
SAMPLE_BLOCK = 8
CONV_ROWS = 64
CONV_PAD = 32
CONV_SPAN = CONV_ROWS + CONV_PAD - 8


def _dot(a, b):
    return jnp.dot(a, b, preferred_element_type=F32)


def _dot_nt(a, b):
    return lax.dot_general(a, b, (((1,), (1,)), ((), ())), preferred_element_type=F32)


def _dot_tn(a, b):
    return lax.dot_general(a, b, (((0,), (0,)), ((), ())), preferred_element_type=F32)


def _silu(x):
    return x * jax.nn.sigmoid(x)


def _rms(x, g):
    return x * lax.rsqrt(jnp.mean(x * x, axis=-1, keepdims=True) + EPS) * g


def _seg_rms(x, g):
    r = lax.broadcasted_iota(jnp.int32, (128, 128), 0)
    c = lax.broadcasted_iota(jnp.int32, (128, 128), 1)
    seg = jnp.where((r < 64) == (c < 64), 1.0 / HEAD_DIM, 0.0).astype(BF16)
    xx = x * x
    hi = xx.astype(BF16)
    lo = (xx - hi.astype(F32)).astype(BF16)
    ms = _dot(hi, seg) + _dot(lo, seg)
    return x * lax.rsqrt(ms + EPS) * g


def _layer_norm(y, g, b):
    mu = jnp.mean(y, axis=-1, keepdims=True)
    d = y - mu
    var = jnp.mean(d * d, axis=-1, keepdims=True)
    return d * lax.rsqrt(var + EPS) * g + b


def _stage_copy(src, l, j, stage, sem, slot, width):
    return pltpu.make_async_copy(src.at[l, pl.ds(j * STAGE_ROWS, STAGE_ROWS), :],
                                 stage.at[slot, :, pl.ds(0, width)], sem.at[slot])


def _swap_inner_heads(t0, t1):
    low = lax.broadcasted_iota(jnp.int32, t0.shape, 1) < 64
    return (jnp.where(low, t0, pltpu.roll(t1, 64, 1)), jnp.where(low, pltpu.roll(t0, 64, 1), t1))


def _stage_weight(src, l, dst, stage, sem, permute_q_cols=False):
    rows, width = dst.shape
    n = rows // STAGE_ROWS
    _stage_copy(src, l, 0, stage, sem, 0, width).start()

    def body(j, carry):
        slot = lax.rem(j, 2)

        @pl.when(j + 1 < n)
        def _():
            _stage_copy(src, l, j + 1, stage, sem, 1 - slot, width).start()

        _stage_copy(src, l, j, stage, sem, slot, width).wait()
        r0 = pl.multiple_of(j * STAGE_ROWS, STAGE_ROWS)
        dst[pl.ds(r0, STAGE_ROWS), :] = stage[slot, :, 0:width].astype(BF16)
        if permute_q_cols:
            t0, t1 = _swap_inner_heads(stage[slot, :, C_AQ:C_AQ + 128],
                                       stage[slot, :, C_AQ + 128:C_AQ + 256])
            dst[pl.ds(r0, STAGE_ROWS), C_AQ:C_AQ + 128] = t0.astype(BF16)
            dst[pl.ds(r0, STAGE_ROWS), C_AQ + 128:C_AQ + 256] = t1.astype(BF16)
        return carry

    lax.fori_loop(0, n, body, 0)


def _stage_w_out(src, l, dst, stage, sem):
    rows, width = dst.shape
    n = rows // STAGE_ROWS
    _stage_copy(src, l, 0, stage, sem, 0, width).start()
    for j in range(n):
        slot = j % 2
        if j + 1 < n:
            _stage_copy(src, l, j + 1, stage, sem, 1 - slot, width).start()
        _stage_copy(src, l, j, stage, sem, slot, width).wait()
        r0 = j * STAGE_ROWS
        if r0 == M_ATT:
            dst[M_ATT:M_ATT + 64, :] = stage[slot, 0:64, 0:width].astype(BF16)
            dst[M_ATT + 128:M_ATT + 192, :] = stage[slot, 64:128, 0:width].astype(BF16)
        elif r0 == M_ATT + 128:
            dst[M_ATT + 64:M_ATT + 128, :] = stage[slot, 0:64, 0:width].astype(BF16)
            dst[M_ATT + 192:M_ATT + 256, :] = stage[slot, 64:128, 0:width].astype(BF16)
        else:
            dst[r0:r0 + STAGE_ROWS, :] = stage[slot, :, 0:width].astype(BF16)


def _run(steps):
    for _ in steps:
        pass


def _interleave(major, minor, major_total, minor_total):
    major_done = minor_done = 0
    for cost in major:
        major_done += cost
        while minor_done * major_total < major_done * minor_total:
            step = next(minor, None)
            if step is None:
                break
            minor_done += step
    _run(minor)


def _matmul_cost(rows, k, n):
    return rows * k * n // (256 * 1024)


def _ffn_steps(x_ref, g, wg_ref, wu_ref, wd_ref, out_ref, xn_scr, a_scr):
    rows = x_ref.shape[0]
    xn_scr[...] = _rms(x_ref[...], g).astype(BF16)
    for c in range(FFN_DIM // FFN_COLS):
        sl = slice(c * FFN_COLS, (c + 1) * FFN_COLS)
        gate = _dot(xn_scr[...], wg_ref[:, sl])
        up = _dot(xn_scr[...], wu_ref[:, sl])
        a_scr[:, sl] = (_silu(gate) * up).astype(BF16)
        yield 2 * _matmul_cost(rows, D_MODEL, FFN_COLS)
    for c in range(D_MODEL // OUT_COLS):
        sl = slice(c * OUT_COLS, (c + 1) * OUT_COLS)
        out_ref[:, sl] = x_ref[:, sl] + 0.5 * _dot(a_scr[...], wd_ref[:, sl])
        yield _matmul_cost(rows, FFN_DIM, OUT_COLS)


def _out_ffn_steps(l, x1_ref, mix_ref, x3_ref, g_ref, wout_b, wg_b, wu_b, wd_b, x2, xn, a):
    rows = x1_ref.shape[0]
    mix = mix_ref[...].astype(BF16)
    for c in range(D_MODEL // OUT_COLS):
        sl = slice(c * OUT_COLS, (c + 1) * OUT_COLS)
        x2[:, sl] = x1_ref[:, sl] + _dot(mix, wout_b[:, sl])
        yield _matmul_cost(rows, MIX_DIM, OUT_COLS)
    yield from _ffn_steps(x2, g_ref[l:l + 1, :], wg_b, wu_b, wd_b, x3_ref, xn, a)


def _out_ffn_cost(rows):
    return _matmul_cost(rows, MIX_DIM * D_MODEL + 3 * D_MODEL * FFN_DIM, 1)


def _ffn_in_steps(l, x_ref, x1_ref, z_ref, g1_ref, g2_ref, wg_b, wu_b, wd_b, win_b, xn, a):
    rows = x_ref.shape[0]
    yield from _ffn_steps(x_ref, g1_ref[l:l + 1, :], wg_b, wu_b, wd_b, x1_ref, xn, a)
    xn[...] = _rms(x1_ref[...], g2_ref[l:l + 1, :]).astype(BF16)
    for c in range(IN_DIM // OUT_COLS):
        sl = slice(c * OUT_COLS, (c + 1) * OUT_COLS)
        z_ref[:, sl] = _dot(xn[...], win_b[:, sl])
        yield _matmul_cost(rows, D_MODEL, OUT_COLS)


def _ffn_in_kernel(l, n_tiles, xp_ref, xs_ref, g1_ref, g2_ref, wg_hbm, wu_hbm, wd_hbm, win_hbm,
                   x1p_ref, zp_ref, x1s_ref, zs_ref,
                   wg_b, wu_b, wd_b, win_b, stage, sem, xn_scr, a_scr):
    i = pl.program_id(0)
    dense_w = (g1_ref, g2_ref, wg_b, wu_b, wd_b, win_b)

    @pl.when(i == 0)
    def _():
        _stage_weight(wg_hbm, l, wg_b, stage, sem)
        _stage_weight(wu_hbm, l, wu_b, stage, sem)
        _stage_weight(wd_hbm, l, wd_b, stage, sem)
        _stage_weight(win_hbm, l, win_b, stage, sem, permute_q_cols=True)

    @pl.when(i < n_tiles)
    def _():
        _run(_ffn_in_steps(l, xp_ref, x1p_ref, zp_ref, *dense_w, xn_scr, a_scr))

    @pl.when(i == n_tiles)
    def _():
        ns = xs_ref.shape[0]
        _run(_ffn_in_steps(l, xs_ref, x1s_ref, zs_ref, *dense_w, xn_scr.at[0:ns], a_scr.at[0:ns]))


def _stage_scratch():
    return [pltpu.VMEM((2, STAGE_ROWS, FFN_DIM), F32), pltpu.SemaphoreType.DMA((2,))]


def _ffn_in(l, xp, xs, g1, g2, wg, wu, wd, win):
    rows, ns = xp.shape[0], xs.shape[0]
    n_tiles = rows // TOKEN_TILE
    cur = lambda w: pl.BlockSpec((TOKEN_TILE, w), lambda i: (jnp.minimum(i, n_tiles - 1), 0))
    once = lambda w: pl.BlockSpec((ns, w), lambda i: (0, 0), pipeline_mode=pl.Buffered(1))
    whole = lambda a: pl.BlockSpec(a.shape, lambda i: (0,) * a.ndim)
    hbm = pl.BlockSpec(memory_space=pl.ANY)
    return pl.pallas_call(
        functools.partial(_ffn_in_kernel, l, n_tiles),
        grid=(n_tiles + 1,),
        in_specs=[cur(D_MODEL), once(D_MODEL), whole(g1), whole(g2), hbm, hbm, hbm, hbm],
        out_specs=[cur(D_MODEL), cur(IN_DIM), once(D_MODEL), once(IN_DIM)],
        out_shape=[jax.ShapeDtypeStruct((rows, D_MODEL), F32),
                   jax.ShapeDtypeStruct((rows, IN_DIM), F32),
                   jax.ShapeDtypeStruct((ns, D_MODEL), F32),
                   jax.ShapeDtypeStruct((ns, IN_DIM), F32)],
        scratch_shapes=[pltpu.VMEM((D_MODEL, FFN_DIM), BF16), pltpu.VMEM((D_MODEL, FFN_DIM), BF16),
                        pltpu.VMEM((FFN_DIM, D_MODEL), BF16), pltpu.VMEM((D_MODEL, IN_DIM), BF16),
                        *_stage_scratch(),
                        pltpu.VMEM((TOKEN_TILE, D_MODEL), BF16),
                        pltpu.VMEM((TOKEN_TILE, FFN_DIM), BF16)],
        compiler_params=pltpu.CompilerParams(dimension_semantics=("arbitrary",),
                                             vmem_limit_bytes=V7X_VMEM_LIMIT),
        name="ffn_in",
    )(xp, xs, g1, g2, wg, wu, wd, win)


def _mix_constants(dmask, qdec, kdec, abias):
    T = RET_CHUNK
    row_f = lax.broadcasted_iota(jnp.int32, (T, T), 0).astype(F32)
    col_f = lax.broadcasted_iota(jnp.int32, (T, T), 1).astype(F32)
    low = lax.broadcasted_iota(jnp.int32, (T, T), 1) < 64
    for h in range(RET_HEADS):
        lg = LOG_DECAY[h]
        diff = row_f - col_f
        dmask[h] = jnp.where(diff >= 0, jnp.exp(lg * jnp.maximum(diff, 0.0)), 0.0)
        qdec[h] = jnp.exp(lg * (row_f + 1.0))
    for p in range(RET_HEADS // 2):
        lgp = jnp.where(low, LOG_DECAY[2 * p], LOG_DECAY[2 * p + 1])
        kdec[p] = jnp.exp(lgp * (T - 1.0 - row_f))
    qpos = lax.broadcasted_iota(jnp.int32, (T, 2 * T), 0) + T
    kpos = lax.broadcasted_iota(jnp.int32, (T, 2 * T), 1)
    dist = (qpos - kpos).astype(F32)
    for h in range(ATT_HEADS):
        abias[h] = jnp.where((dist >= 0) & (dist < WINDOW), -SLOPES[h] * dist, NEG_INF)


def _mix_tile_steps(l, first, z_ref, mix_ref, sinks_ref, rng_ref, qg_ref, kg_ref,
                    s_scr, kbuf, vbuf, qbuf, klast, dmask, qdec, kdec, abias, conv_lane):
    T = RET_CHUNK
    NCH = TOKEN_TILE // T
    row_i = lax.broadcasted_iota(jnp.int32, (T, T), 0)
    low = lax.broadcasted_iota(jnp.int32, (T, T), 1) < 64

    def zc(r0, nrows, a, w):
        return z_ref[r0:r0 + nrows, a:a + w]

    def seg_mean_sq(x):
        low_x = lax.broadcasted_iota(jnp.int32, x.shape, 1) < 64
        xx = x * x
        s_lo = jnp.sum(jnp.where(low_x, xx, 0.0), axis=-1, keepdims=True)
        s_hi = jnp.sum(jnp.where(low_x, 0.0, xx), axis=-1, keepdims=True)
        return jnp.where(low_x, s_lo, s_hi) * (1.0 / HEAD_DIM)

    def retention_unit(p, j):
        s_cur = s_scr[p]
        r0 = T * j
        qp = zc(r0, T, C_RQ + 128 * p, 128)
        kp = zc(r0, T, C_RK + 128 * p, 128) * (RET_DK ** -0.5)
        vb = zc(r0, T, C_RV + 256 * p, 256).astype(BF16)
        kb = kp.astype(BF16)
        sb = s_cur.astype(BF16)
        qm = jnp.concatenate([jnp.where(low, qp, 0.0), jnp.where(low, 0.0, qp)], axis=0).astype(BF16)
        a = _dot_nt(qm, kb)
        qs = _dot(qm, sb)
        upd = _dot_tn((kp * kdec[p]).astype(BF16), vb)
        yield RET_COST // 4
        ab = (a * dmask[2 * p:2 * p + 2].reshape(2 * T, T)).astype(BF16)
        yield RET_COST // 4
        o = _dot(ab, vb)
        yield RET_COST // 4
        for e in range(2):
            h = 2 * p + e
            oh = _rms(o[T * e:T * e + T, 128 * e:128 * e + 128] + qs[T * e:T * e + T, :] * qdec[h],
                      rng_ref[l, h:h + 1, :])
            mix_ref[r0:r0 + T, M_RET + 128 * h:M_RET + 128 * h + 128] = (
                oh * _silu(zc(r0, T, C_RG + 128 * h, 128))).astype(BF16)
        top = row_i < 64
        new = jnp.where(top, upd[:, 0:128], upd[:, 128:256])
        cd = jnp.where(top, math.exp(LOG_DECAY[2 * p] * T), math.exp(LOG_DECAY[2 * p + 1] * T))
        s_scr[p] = s_cur * cd + new
        yield RET_COST // 4

    def attention_prep():
        ak = zc(0, TOKEN_TILE, C_AK, 128)
        aq = [zc(0, TOKEN_TILE, C_AQ + 128 * t, 128) for t in range(2)]
        ms_k = seg_mean_sq(ak)
        ms_q = [seg_mean_sq(aq[t]) for t in range(2)]
        yield ATT_PREP_COST // 2
        kn = ak * lax.rsqrt(ms_k + EPS) * kg_ref[l:l + 1, :]
        klast[...] = kn[TOKEN_TILE - T:TOKEN_TILE, :]
        kbuf[T:T + TOKEN_TILE, :] = kn.astype(BF16)
        vbuf[T:T + TOKEN_TILE, :] = zc(0, TOKEN_TILE, C_AV, 128).astype(BF16)
        for t in range(2):
            qt = aq[t] * lax.rsqrt(ms_q[t] + EPS) * qg_ref[l:l + 1, :] * (HEAD_DIM ** -0.5)
            for j in range(NCH):
                qj = qt[T * j:T * j + T, :]
                qbuf[j, (2 * t) * T:(2 * t + 1) * T, :] = jnp.where(low, qj, 0.0).astype(BF16)
                qbuf[j, (2 * t + 1) * T:(2 * t + 2) * T, :] = jnp.where(low, 0.0, qj).astype(BF16)
        yield ATT_PREP_COST // 2

    def attention_unit(j):
        r0 = T * j
        kcat = kbuf[r0:r0 + 2 * T, :]
        vcat = vbuf[r0:r0 + 2 * T, :]
        s = _dot_nt(qbuf[j], kcat)
        yield ATT_COST // 2
        probs = []
        for t in range(2):
            for kvh in range(KV_HEADS):
                head = 2 * kvh + t
                sink = sinks_ref[l, head]
                bias = abias[head]
                if j == 0:
                    no_prev = lax.broadcasted_iota(jnp.int32, (T, 2 * T), 1) < jnp.where(first, T, 0)
                    bias = jnp.where(no_prev, NEG_INF, bias)
                blk = 2 * t + kvh
                sb = s[T * blk:T * blk + T, :] + bias
                m = jnp.maximum(jnp.max(sb, axis=-1, keepdims=True), sink)
                e = jnp.exp(sb - m)
                inv = 1.0 / (jnp.sum(e, axis=-1, keepdims=True) + jnp.exp(sink - m))
                probs.append((e * inv).astype(BF16))
            yield ATT_COST
        o = _dot(jnp.concatenate(probs, axis=0), vcat)
        for t in range(2):
            mix_ref[r0:r0 + T, M_ATT + 128 * t:M_ATT + 128 * t + 128] = jnp.where(
                low, o[T * 2 * t:T * (2 * t + 1), :], o[T * (2 * t + 1):T * (2 * t + 2), :]
            ).astype(BF16)
        yield ATT_COST // 2

    def ret_lane():
        for j in range(NCH):
            for p in range(RET_HEADS // 2):
                yield from retention_unit(p, j)

    def att_lane():
        yield from attention_prep()
        for j in range(NCH):
            yield from attention_unit(j)

    live = [att_lane(), conv_lane, ret_lane()]
    while live:
        for lane in list(live):
            cost = next(lane, None)
            if cost is None:
                live.remove(lane)
            else:
                yield cost


def _conv_tile_steps(l, zcg_ref, out_ref, cwb, cb_ref, lng_ref, lnb_ref, ubuf, ush, pw_b):
    ubuf[CONV_PAD:CONV_PAD + TOKEN_TILE, :] = (
        zcg_ref[:, 0:CONV_CH] * jax.nn.sigmoid(zcg_ref[:, CONV_CH:2 * CONV_CH]))
    yield CONV_PREP_COST
    for rb in range(TOKEN_TILE // CONV_ROWS):
        base = rb * CONV_ROWS
        sh = ush.at[rb % 2]
        for r in range(1, 8):
            sh[r - 1] = ubuf[base + r:base + r + CONV_SPAN, :]
        acc = None
        for k in range(CONV_K):
            off = CONV_PAD - (CONV_K - 1) + k
            a8, r = off // 8 * 8, off % 8
            src = (ubuf[base + a8:base + a8 + CONV_ROWS, :] if r == 0
                   else sh[r - 1, a8:a8 + CONV_ROWS, :])
            term = cwb[k][None] * src.reshape(CONV_ROWS // 8, 8, CONV_CH)
            acc = term if acc is None else acc + term
        y = _silu(_layer_norm(acc.reshape(CONV_ROWS, CONV_CH) + cb_ref[l:l + 1, :],
                              lng_ref[l:l + 1, :], lnb_ref[l:l + 1, :])).astype(BF16)
        yield CONV_COST - CONV_COST // 8
        out_ref[base:base + CONV_ROWS, :] = _dot(y, pw_b[...]).astype(BF16)
        yield CONV_COST // 8


RET_COST, ATT_PREP_COST, ATT_COST, CONV_PREP_COST, CONV_COST = 160, 250, 220, 200, 410
MIX_COST = ((RET_HEADS // 2) * (TOKEN_TILE // RET_CHUNK) * RET_COST + ATT_PREP_COST
            + 3 * (TOKEN_TILE // RET_CHUNK) * ATT_COST)
CONV_TILE_COST = CONV_PREP_COST + (TOKEN_TILE // CONV_ROWS) * CONV_COST


def _mix_out_ffn_kernel(l, n_tiles, tps, sinks_ref, z_ref, x1p_ref, x1s_ref, mixs_ref, g_ref,
                        rng_ref, qg_ref, kg_ref, cw_ref, cb_ref, lng_ref, lnb_ref, pw_ref,
                        wout_hbm, wg_hbm, wu_hbm, wd_hbm,
                        x3p_ref, x3s_ref, ret_ref, kwin_ref, vwin_ref, conv_ref,
                        wout_b, wg_b, wu_b, wd_b, stage, sem, x2_scr, xn_scr, a_scr, mix_new, mix_old,
                        s_scr, kbuf, vbuf, qbuf, klast, dmask, qdec, kdec, abias, ubuf, ush, pw_b,
                        cwb):
    i = pl.program_id(0)
    c = lax.rem(i, tps)
    T = RET_CHUNK
    mix_args = (sinks_ref, rng_ref, qg_ref, kg_ref,
                s_scr, kbuf, vbuf, qbuf, klast, dmask, qdec, kdec, abias)
    dense_w = (g_ref, wout_b, wg_b, wu_b, wd_b)

    @pl.when(i == 0)
    def _():
        _stage_w_out(wout_hbm, l, wout_b, stage, sem)
        _stage_weight(wg_hbm, l, wg_b, stage, sem)
        _stage_weight(wu_hbm, l, wu_b, stage, sem)
        _stage_weight(wd_hbm, l, wd_b, stage, sem)
        _mix_constants(dmask, qdec, kdec, abias)
        pw_b[...] = pw_ref[l].astype(BF16)
        for k in range(CONV_K):
            cwb[k] = jnp.broadcast_to(cw_ref[l, k:k + 1, :], (8, CONV_CH))

    @pl.when(c == 0)
    def _():
        s_scr[...] = jnp.zeros_like(s_scr)
        kbuf[0:T, :] = jnp.zeros((T, 128), BF16)
        vbuf[0:T, :] = jnp.zeros((T, 128), BF16)
        ubuf[0:CONV_PAD, :] = jnp.zeros((CONV_PAD, CONV_CH), F32)

    @pl.when(c > 0)
    def _():
        kbuf[0:T, :] = kbuf[TOKEN_TILE:TOKEN_TILE + T, :]
        vbuf[0:T, :] = vbuf[TOKEN_TILE:TOKEN_TILE + T, :]
        ubuf[0:CONV_PAD, :] = ubuf[TOKEN_TILE:TOKEN_TILE + CONV_PAD, :]

    @pl.when(i > 0)
    def _():
        mix_old[...] = mix_new[...]

    def mixers():
        conv = _conv_tile_steps(l, z_ref.at[:, C_CA:C_CA + 2 * CONV_CH],
                                mix_new.at[:, M_CONV:M_CONV + CONV_CH],
                                cwb, cb_ref, lng_ref, lnb_ref, ubuf, ush, pw_b)
        return _mix_tile_steps(l, c == 0, z_ref, mix_new, *mix_args, conv)

    def dense():
        return _out_ffn_steps(l, x1p_ref, mix_old, x3p_ref, *dense_w, x2_scr, xn_scr, a_scr)

    @pl.when(i == 0)
    def _():
        _run(mixers())

    @pl.when((i > 0) & (i < n_tiles))
    def _():
        _interleave(dense(), mixers(), _out_ffn_cost(TOKEN_TILE), MIX_COST + CONV_TILE_COST)

    @pl.when(i == n_tiles)
    def _():
        _run(dense())
        ns = x1s_ref.shape[0]
        _run(_out_ffn_steps(l, x1s_ref, mixs_ref, x3s_ref, *dense_w,
                            x2_scr.at[0:ns], xn_scr.at[0:ns], a_scr.at[0:ns]))

    @pl.when((c == tps - 1) & (i < n_tiles))
    def _():
        ret_ref[0] = s_scr[...]
        kwin_ref[0] = klast[...].T
        vwin_ref[0] = z_ref[TOKEN_TILE - T:TOKEN_TILE, C_AV:C_AV + 128].T
        conv_ref[0] = ubuf[CONV_PAD + TOKEN_TILE - (CONV_K - 1):CONV_PAD + TOKEN_TILE, :]


def _mix_out_ffn(l, nb, z, x1p, x1s, mixs, g, sinks, rng, qg2, kg2, cw, cb, lng, lnb, pw,
                 wout, wg, wu, wd):
    rows, ns = x1p.shape[0], x1s.shape[0]
    n_tiles = rows // TOKEN_TILE
    tps = n_tiles // nb
    T = RET_CHUNK
    cur = lambda w: pl.BlockSpec((TOKEN_TILE, w), lambda i: (jnp.minimum(i, n_tiles - 1), 0))
    prev = lambda w: pl.BlockSpec((TOKEN_TILE, w), lambda i: (jnp.maximum(i - 1, 0), 0))
    once = lambda w: pl.BlockSpec((ns, w), lambda i: (0, 0), pipeline_mode=pl.Buffered(1))
    whole = lambda a: pl.BlockSpec(a.shape, lambda i: (0,) * a.ndim)
    per_seq = lambda shape: pl.BlockSpec(
        (1,) + shape, lambda i: (jnp.minimum(i, n_tiles - 1) // tps,) + (0,) * len(shape))
    hbm = pl.BlockSpec(memory_space=pl.ANY)
    return pl.pallas_call(
        functools.partial(_mix_out_ffn_kernel, l, n_tiles, tps),
        grid=(n_tiles + 1,),
        in_specs=[pl.BlockSpec(memory_space=pltpu.SMEM), cur(IN_DIM), prev(D_MODEL),
                  once(D_MODEL), once(MIX_DIM), whole(g),
                  whole(rng), whole(qg2), whole(kg2), whole(cw), whole(cb), whole(lng), whole(lnb),
                  whole(pw), hbm, hbm, hbm, hbm],
        out_specs=[prev(D_MODEL), once(D_MODEL),
                   per_seq((2, 128, RET_DV)), per_seq((128, WINDOW)), per_seq((128, WINDOW)),
                   per_seq((CONV_K - 1, CONV_CH))],
        out_shape=[jax.ShapeDtypeStruct((rows, D_MODEL), F32),
                   jax.ShapeDtypeStruct((ns, D_MODEL), F32),
                   jax.ShapeDtypeStruct((nb, 2, 128, RET_DV), F32),
                   jax.ShapeDtypeStruct((nb, 128, WINDOW), F32),
                   jax.ShapeDtypeStruct((nb, 128, WINDOW), F32),
                   jax.ShapeDtypeStruct((nb, CONV_K - 1, CONV_CH), F32)],
        scratch_shapes=[pltpu.VMEM((MIX_DIM, D_MODEL), BF16),
                        pltpu.VMEM((D_MODEL, FFN_DIM), BF16), pltpu.VMEM((D_MODEL, FFN_DIM), BF16),
                        pltpu.VMEM((FFN_DIM, D_MODEL), BF16),
                        *_stage_scratch(),
                        pltpu.VMEM((TOKEN_TILE, D_MODEL), F32),
                        pltpu.VMEM((TOKEN_TILE, D_MODEL), BF16),
                        pltpu.VMEM((TOKEN_TILE, FFN_DIM), BF16),
                        pltpu.VMEM((TOKEN_TILE, MIX_DIM), BF16),
                        pltpu.VMEM((TOKEN_TILE, MIX_DIM), BF16),
                        pltpu.VMEM((2, 128, RET_DV), F32),
                        pltpu.VMEM((T + TOKEN_TILE, 128), BF16),
                        pltpu.VMEM((T + TOKEN_TILE, 128), BF16),
                        pltpu.VMEM((TOKEN_TILE // T, 2 * KV_HEADS * T, 128), BF16),
                        pltpu.VMEM((T, 128), F32),
                        pltpu.VMEM((RET_HEADS, T, T), F32), pltpu.VMEM((RET_HEADS, T, T), F32),
                        pltpu.VMEM((2, T, T), F32), pltpu.VMEM((ATT_HEADS, T, 2 * T), F32),
                        pltpu.VMEM((CONV_PAD + TOKEN_TILE, CONV_CH), F32),
                        pltpu.VMEM((2, 7, CONV_SPAN, CONV_CH), F32),
                        pltpu.VMEM((CONV_CH, CONV_CH), BF16),
                        pltpu.VMEM((CONV_K, 8, CONV_CH), F32)],
        compiler_params=pltpu.CompilerParams(dimension_semantics=("arbitrary",),
                                             vmem_limit_bytes=V7X_VMEM_LIMIT),
        name="mix_out_ffn",
    )(sinks, z, x1p, x1s, mixs, g, rng, qg2, kg2, cw, cb, lng, lnb, pw, wout, wg, wu, wd)


def _mix_sample_kernel(l, sinks_ref, z_ref, col_ref, s_ref, ck_ref, cv_ref, sc_ref,
                       rng_ref, qg_ref, kg_ref, kgc_ref, cw_ref, cb_ref, lng_ref, lnb_ref, pw_ref,
                       *rest):
    mix_ref, so_ref, cko_ref, cvo_ref, sco_ref, o_scr, oa_scr = rest[-7:]
    NB = SAMPLE_BLOCK
    P = WINDOW

    def put(ref, idx, val):
        for slab in range(ref.shape[0]):
            ref[(slab,) + idx] = val

    for bl in range(NB):
        for h in range(RET_HEADS):
            gamma = math.exp(LOG_DECAY[h])
            r0 = RET_DK * h
            S = s_ref[0, bl, h]
            qc = col_ref[0, r0:r0 + RET_DK, bl:bl + 1]
            kc = col_ref[0, 256 + r0:256 + r0 + RET_DK, bl:bl + 1] * (RET_DK ** -0.5)
            v = z_ref[bl:bl + 1, C_RV + 128 * h:C_RV + 128 * h + 128]
            qk = jnp.sum(qc * kc, axis=0, keepdims=True)
            o_scr[bl:bl + 1, 128 * h:128 * h + 128] = (
                gamma * jnp.sum(qc * S, axis=0, keepdims=True) + qk * v)
            put(so_ref, (bl, h), gamma * S + kc * v)
    for h in range(RET_HEADS):
        o = _rms(o_scr[:, 128 * h:128 * h + 128], rng_ref[l, h:h + 1, :])
        mix_ref[:, M_RET + 128 * h:M_RET + 128 * h + 128] = (
            o * _silu(z_ref[:, C_RG + 128 * h:C_RG + 128 * h + 128]))

    q_tiles = [_seg_rms(z_ref[:, C_AQ + 128 * t:C_AQ + 128 * t + 128], qg_ref[l:l + 1, :])
               * (HEAD_DIM ** -0.5) for t in range(2)]
    kn = _seg_rms(z_ref[:, C_AK:C_AK + 128], kg_ref[l:l + 1, :])
    vn = z_ref[:, C_AV:C_AV + 128]
    akc = col_ref[0, 512:640, :]
    vnc = col_ref[0, 640:768, :]
    ms = jnp.concatenate(
        [jnp.broadcast_to(jnp.mean(akc[64 * s:64 * s + 64, :] ** 2, axis=0, keepdims=True), (64, NB))
         for s in range(KV_HEADS)], axis=0)
    knc = akc * lax.rsqrt(ms + EPS) * kgc_ref[l]
    rid = lax.broadcasted_iota(jnp.int32, (8, 128), 0)
    lane = lax.broadcasted_iota(jnp.int32, (8, 128), 1)
    sel = ((rid % 2 == 0) == (lane < 64)) & (rid < ATT_HEADS)
    rcol = lax.broadcasted_iota(jnp.int32, (8, 1), 0)
    heads = [2 * (r % 2) + r // 2 for r in range(ATT_HEADS)]
    slope_col = jnp.zeros((8, 1), F32)
    sink_col = jnp.zeros((8, 1), F32)
    for r, hd in enumerate(heads):
        slope_col = jnp.where(rcol == r, SLOPES[hd], slope_col)
        sink_col = jnp.where(rcol == r, sinks_ref[l, hd], sink_col)
    dist = float(P) - lane.astype(F32)
    key_ok = (dist < float(WINDOW)) & (lane + (PAST_LEN - P) >= 0)
    newest = lax.broadcasted_iota(jnp.int32, (128, P), 1) == P - 1
    for bl in range(NB):
        KT = ck_ref[0, bl].reshape(128, P)
        VT = cv_ref[0, bl].reshape(128, P)
        qrows = jnp.where(rid < 2, q_tiles[0][bl:bl + 1, :], q_tiles[1][bl:bl + 1, :])
        q4 = jnp.where(sel, qrows, 0.0)
        s = _dot(q4.astype(BF16), KT.astype(BF16))
        s = jnp.where(key_ok, s - slope_col * dist, NEG_INF)
        s_new = jnp.sum(q4 * kn[bl:bl + 1, :], axis=-1, keepdims=True)
        m = jnp.maximum(jnp.maximum(jnp.max(s, axis=-1, keepdims=True), s_new), sink_col)
        e = jnp.exp(s - m)
        e_new = jnp.exp(s_new - m)
        inv = 1.0 / (jnp.sum(e, axis=-1, keepdims=True) + e_new + jnp.exp(sink_col - m))
        o = _dot_nt((e * inv).astype(BF16), VT.astype(BF16)) + (e_new * inv) * vn[bl:bl + 1, :]
        for t in range(2):
            oa_scr[bl:bl + 1, 128 * t:128 * t + 128] = jnp.where(
                lane[0:1, :] < 64, o[2 * t:2 * t + 1, :], o[2 * t + 1:2 * t + 2, :])
        put(cko_ref, (bl,), jnp.where(newest, knc[:, bl:bl + 1], pltpu.roll(KT, P - 1, 1)).reshape(
            KV_HEADS, HEAD_DIM, P))
        put(cvo_ref, (bl,), jnp.where(newest, vnc[:, bl:bl + 1], pltpu.roll(VT, P - 1, 1)).reshape(
            KV_HEADS, HEAD_DIM, P))
    mix_ref[:, M_ATT:M_ATT + 256] = oa_scr[...]

    u = z_ref[:, C_CA:C_CA + CONV_CH] * jax.nn.sigmoid(z_ref[:, C_CG:C_CG + CONV_CH])
    KT1 = CONV_K - 1
    y = cw_ref[l, KT1:KT1 + 1, :] * u
    for k in range(KT1):
        y = y + cw_ref[l, k:k + 1, :] * sc_ref[0, k]
        if k > 0:
            put(sco_ref, (k - 1,), sc_ref[0, k])
    put(sco_ref, (KT1 - 1,), u)
    y = _silu(_layer_norm(y + cb_ref[l:l + 1, :], lng_ref[l:l + 1, :], lnb_ref[l:l + 1, :]))
    mix_ref[:, M_CONV:M_CONV + CONV_CH] = _dot(y.astype(BF16), pw_ref[l].astype(BF16))


def _mix_sample(l, z, cols, s, ckt, cvt, sct, sinks, rng, qg2, kg2, kgc, cw, cb, lng, lnb, pw, prev):
    ns = z.shape[0]
    NB = SAMPLE_BLOCK
    whole = lambda a: pl.BlockSpec(a.shape, lambda i: (0,) * a.ndim)

    def specs(slabs, at):
        return [pl.BlockSpec((slabs, NB, RET_HEADS, RET_DK, RET_DV), lambda i: (at, i, 0, 0, 0)),
                pl.BlockSpec((slabs, NB, KV_HEADS, HEAD_DIM, WINDOW), lambda i: (at, i, 0, 0, 0)),
                pl.BlockSpec((slabs, NB, KV_HEADS, HEAD_DIM, WINDOW), lambda i: (at, i, 0, 0, 0)),
                pl.BlockSpec((slabs, CONV_K - 1, NB, CONV_CH), lambda i: (at, 0, i, 0))]

    state_specs = specs(1, l)
    out_state_specs = specs(s.shape[0], 0) if prev is None else state_specs
    in_specs = [pl.BlockSpec(memory_space=pltpu.SMEM),
                pl.BlockSpec((NB, IN_DIM), lambda i: (i, 0)),
                pl.BlockSpec((1, 768, NB), lambda i: (i, 0, 0)),
                *state_specs,
                whole(rng), whole(qg2), whole(kg2), whole(kgc), whole(cw), whole(cb), whole(lng),
                whole(lnb), whole(pw)]
    args = [sinks, z, cols, s, ckt, cvt, sct, rng, qg2, kg2, kgc, cw, cb, lng, lnb, pw]
    aliases = {}
    if prev is not None:
        for k, a in enumerate(prev):
            aliases[len(args)] = 1 + k
            in_specs.append(pl.BlockSpec(memory_space=pl.ANY))
            args.append(a)
    return pl.pallas_call(
        functools.partial(_mix_sample_kernel, l),
        grid=(ns // NB,),
        in_specs=in_specs,
        out_specs=[pl.BlockSpec((NB, MIX_DIM), lambda i: (i, 0)), *out_state_specs],
        out_shape=[jax.ShapeDtypeStruct((ns, MIX_DIM), F32),
                   jax.ShapeDtypeStruct(s.shape, F32), jax.ShapeDtypeStruct(ckt.shape, F32),
                   jax.ShapeDtypeStruct(cvt.shape, F32), jax.ShapeDtypeStruct(sct.shape, F32)],
        scratch_shapes=[pltpu.VMEM((NB, 512), F32), pltpu.VMEM((NB, 256), F32)],
        input_output_aliases=aliases,
        compiler_params=pltpu.CompilerParams(dimension_semantics=("arbitrary",)),
        name="mix_sample",
    )(*args)


def kernel(x_prompt, x_sample, state_ret, cache_k_win, cache_v_win, state_conv, ffn1_norm, ffn1_wg, ffn1_wu, ffn1_wd, mix_norm, w_in, ret_norm_g, q_norm_g, k_norm_g, sinks, conv_w, conv_b, conv_ln_g, conv_ln_b, conv_pw, w_out, ffn2_norm, ffn2_wg, ffn2_wu, ffn2_wd):
    nb, seq, _ = x_prompt.shape
    ns = x_sample.shape[0]
    assert x_sample.shape[1] == 1 and seq % TOKEN_TILE == 0
    assert ns % SAMPLE_BLOCK == 0 and ns <= TOKEN_TILE and cache_k_win.shape[2] == WINDOW
    hp = x_prompt.reshape(nb * seq, D_MODEL)
    hs = x_sample.reshape(ns, D_MODEL)
    ckt = cache_k_win.transpose(0, 1, 3, 4, 2)
    cvt = cache_v_win.transpose(0, 1, 3, 4, 2)
    sct = state_conv.transpose(0, 2, 1, 3)
    qg2 = jnp.tile(q_norm_g, (1, 2))
    kg2 = jnp.tile(k_norm_g, (1, 2))
    kgc = kg2.reshape(DEPTH, 128, 1)
    conv_params = (conv_w, conv_b, conv_ln_g, conv_ln_b, conv_pw)
    prompt_states = [[] for _ in range(4)]
    sample_states = None
    for l in range(DEPTH):
        hp1, zp, hs1, zs = _ffn_in(l, hp, hs, ffn1_norm, mix_norm, ffn1_wg, ffn1_wu, ffn1_wd, w_in)
        cols = jnp.concatenate([zs[:, C_RQ:C_RQ + 512], zs[:, C_AK:C_AK + 256]], axis=1)
        cols = cols.T.reshape(768, ns // SAMPLE_BLOCK, SAMPLE_BLOCK).transpose(1, 0, 2)
        mixs, *sample_states = _mix_sample(l, zs, cols, state_ret, ckt, cvt, sct, sinks,
                                           ret_norm_g, qg2, kg2, kgc, conv_w, conv_b, conv_ln_g,
                                           conv_ln_b, conv_pw, sample_states)
        hp, hs, r1, k1, v1, c1 = _mix_out_ffn(l, nb, zp, hp1, hs1, mixs, ffn2_norm, sinks,
                                              ret_norm_g, qg2, kg2, *conv_params,
                                              w_out, ffn2_wg, ffn2_wu, ffn2_wd)
        for lst, val in zip(prompt_states, (r1, k1, v1, c1)):
            lst.append(val)
    ret_p, kwin_p, vwin_p, conv_p = (jnp.stack(s) for s in prompt_states)
    ret_s, kwin_s, vwin_s, conv_s = sample_states
    return (hp.reshape(nb, seq, D_MODEL), hs.reshape(ns, 1, D_MODEL),
            ret_p.reshape(DEPTH, nb, RET_HEADS, RET_DK, RET_DV), ret_s,
            kwin_p.reshape(DEPTH, nb, KV_HEADS, HEAD_DIM, WINDOW).transpose(0, 1, 4, 2, 3),
            kwin_s.transpose(0, 1, 4, 2, 3),
            vwin_p.reshape(DEPTH, nb, KV_HEADS, HEAD_DIM, WINDOW).transpose(0, 1, 4, 2, 3),
            vwin_s.transpose(0, 1, 4, 2, 3),
            conv_p, conv_s.transpose(0, 2, 1, 3))
```

```python
import functools
import math

import jax
import jax.numpy as jnp
from jax import lax
from jax.experimental import pallas as pl
from jax.experimental.pallas import tpu as pltpu

F32 = jnp.float32
BF16 = jnp.bfloat16

D_MODEL = 1024
DEPTH = 2
PAST_LEN = 8192
RET_HEADS = 4
RET_DK = 64
RET_DV = 128
RET_CHUNK = 128
ATT_HEADS = 4
KV_HEADS = 2
HEAD_DIM = 64
WINDOW = 128
CONV_CH = 256
CONV_K = 31
FFN_DIM = 2816
EPS = 1e-6
NEG_INF = -1e30

C_RQ, C_RK, C_RV, C_RG = 0, 256, 512, 1024
C_AQ, C_AK, C_AV, C_CA, C_CG = 1536, 1792, 1920, 2048, 2304
IN_DIM = 2560
MIX_DIM = 1024
M_RET, M_ATT, M_CONV = 0, 512, 768

LOG_DECAY = [math.log1p(-2.0 ** (-5 - h)) for h in range(RET_HEADS)]
SLOPES = [2.0 ** (-8.0 * (h + 1) / ATT_HEADS) for h in range(ATT_HEADS)]

V7X_VMEM_LIMIT = 60 * 1024 * 1024
TOKEN_TILE = 512
FFN_COLS = 256
OUT_COLS = 256
STAGE_ROWS = 128
SAMPLE_BLOCK = 8
CONV_ROWS = 64
CONV_PAD = 32
CONV_SPAN = CONV_ROWS + CONV_PAD - 8


def _dot(a, b):
    return jnp.dot(a, b, preferred_element_type=F32)


def _dot_nt(a, b):
    return lax.dot_general(a, b, (((1,), (1,)), ((), ())), preferred_element_type=F32)


def _dot_tn(a, b):
    return lax.dot_general(a, b, (((0,), (0,)), ((), ())), preferred_element_type=F32)


def _silu(x):
    return x * jax.nn.sigmoid(x)


def _rms(x, g):
    return x * lax.rsqrt(jnp.mean(x * x, axis=-1, keepdims=True) + EPS) * g


def _seg_rms(x, g):
    r = lax.broadcasted_iota(jnp.int32, (128, 128), 0)
    c = lax.broadcasted_iota(jnp.int32, (128, 128), 1)
    seg = jnp.where((r < 64) == (c < 64), 1.0 / HEAD_DIM, 0.0).astype(BF16)
    xx = x * x
    hi = xx.astype(BF16)
    lo = (xx - hi.astype(F32)).astype(BF16)
    ms = _dot(hi, seg) + _dot(lo, seg)
    return x * lax.rsqrt(ms + EPS) * g


def _layer_norm(y, g, b):
    mu = jnp.mean(y, axis=-1, keepdims=True)
    d = y - mu
    var = jnp.mean(d * d, axis=-1, keepdims=True)
    return d * lax.rsqrt(var + EPS) * g + b


def _stage_copy(src, l, j, stage, sem, slot, width):
    return pltpu.make_async_copy(src.at[l, pl.ds(j * STAGE_ROWS, STAGE_ROWS), :],
                                 stage.at[slot, :, pl.ds(0, width)], sem.at[slot])


def _swap_inner_heads(t0, t1):
    low = lax.broadcasted_iota(jnp.int32, t0.shape, 1) < 64
    return (jnp.where(low, t0, pltpu.roll(t1, 64, 1)), jnp.where(low, pltpu.roll(t0, 64, 1), t1))


def _stage_weight(src, l, dst, stage, sem, permute_q_cols=False):
    rows, width = dst.shape
    n = rows // STAGE_ROWS
    _stage_copy(src, l, 0, stage, sem, 0, width).start()

    def body(j, carry):
        slot = lax.rem(j, 2)

        @pl.when(j + 1 < n)
        def _():
            _stage_copy(src, l, j + 1, stage, sem, 1 - slot, width).start()

        _stage_copy(src, l, j, stage, sem, slot, width).wait()
        r0 = pl.multiple_of(j * STAGE_ROWS, STAGE_ROWS)
        dst[pl.ds(r0, STAGE_ROWS), :] = stage[slot, :, 0:width].astype(BF16)
        if permute_q_cols:
            t0, t1 = _swap_inner_heads(stage[slot, :, C_AQ:C_AQ + 128],
                                       stage[slot, :, C_AQ + 128:C_AQ + 256])
            dst[pl.ds(r0, STAGE_ROWS), C_AQ:C_AQ + 128] = t0.astype(BF16)
            dst[pl.ds(r0, STAGE_ROWS), C_AQ + 128:C_AQ + 256] = t1.astype(BF16)
        return carry

    lax.fori_loop(0, n, body, 0)


def _stage_w_out(src, l, dst, stage, sem):
    rows, width = dst.shape
    n = rows // STAGE_ROWS
    _stage_copy(src, l, 0, stage, sem, 0, width).start()
    for j in range(n):
        slot = j % 2
        if j + 1 < n:
            _stage_copy(src, l, j + 1, stage, sem, 1 - slot, width).start()
        _stage_copy(src, l, j, stage, sem, slot, width).wait()
        r0 = j * STAGE_ROWS
        if r0 == M_ATT:
            dst[M_ATT:M_ATT + 64, :] = stage[slot, 0:64, 0:width].astype(BF16)
            dst[M_ATT + 128:M_ATT + 192, :] = stage[slot, 64:128, 0:width].astype(BF16)
        elif r0 == M_ATT + 128:
            dst[M_ATT + 64:M_ATT + 128, :] = stage[slot, 0:64, 0:width].astype(BF16)
            dst[M_ATT + 192:M_ATT + 256, :] = stage[slot, 64:128, 0:width].astype(BF16)
        else:
            dst[r0:r0 + STAGE_ROWS, :] = stage[slot, :, 0:width].astype(BF16)


def _run(steps):
    for _ in steps:
        pass


def _interleave(major, minor, major_total, minor_total):
    major_done = minor_done = 0
    for cost in major:
        major_done += cost
        while minor_done * major_total < major_done * minor_total:
            step = next(minor, None)
            if step is None:
                break
            minor_done += step
    _run(minor)


def _matmul_cost(rows, k, n):
    return rows * k * n // (256 * 1024)


def _ffn_steps(x_ref, g, wg_ref, wu_ref, wd_ref, out_ref, xn_scr, a_scr):
    rows = x_ref.shape[0]
    xn_scr[...] = _rms(x_ref[...], g).astype(BF16)
    for c in range(FFN_DIM // FFN_COLS):
        sl = slice(c * FFN_COLS, (c + 1) * FFN_COLS)
        gate = _dot(xn_scr[...], wg_ref[:, sl])
        up = _dot(xn_scr[...], wu_ref[:, sl])
        a_scr[:, sl] = (_silu(gate) * up).astype(BF16)
        yield 2 * _matmul_cost(rows, D_MODEL, FFN_COLS)
    for c in range(D_MODEL // OUT_COLS):
        sl = slice(c * OUT_COLS, (c + 1) * OUT_COLS)
        out_ref[:, sl] = x_ref[:, sl] + 0.5 * _dot(a_scr[...], wd_ref[:, sl])
        yield _matmul_cost(rows, FFN_DIM, OUT_COLS)


def _out_ffn_steps(l, x1_ref, mix_ref, x3_ref, g_ref, wout_b, wg_b, wu_b, wd_b, x2, xn, a):
    rows = x1_ref.shape[0]
    mix = mix_ref[...].astype(BF16)
    for c in range(D_MODEL // OUT_COLS):
        sl = slice(c * OUT_COLS, (c + 1) * OUT_COLS)
        x2[:, sl] = x1_ref[:, sl] + _dot(mix, wout_b[:, sl])
        yield _matmul_cost(rows, MIX_DIM, OUT_COLS)
    yield from _ffn_steps(x2, g_ref[l:l + 1, :], wg_b, wu_b, wd_b, x3_ref, xn, a)


def _out_ffn_cost(rows):
    return _matmul_cost(rows, MIX_DIM * D_MODEL + 3 * D_MODEL * FFN_DIM, 1)


def _ffn_in_steps(l, x_ref, x1_ref, z_ref, g1_ref, g2_ref, wg_b, wu_b, wd_b, win_b, xn, a):
    rows = x_ref.shape[0]
    yield from _ffn_steps(x_ref, g1_ref[l:l + 1, :], wg_b, wu_b, wd_b, x1_ref, xn, a)
    xn[...] = _rms(x1_ref[...], g2_ref[l:l + 1, :]).astype(BF16)
    for c in range(IN_DIM // OUT_COLS):
        sl = slice(c * OUT_COLS, (c + 1) * OUT_COLS)
        z_ref[:, sl] = _dot(xn[...], win_b[:, sl])
        yield _matmul_cost(rows, D_MODEL, OUT_COLS)


def _ffn_in_cost(rows):
    return _matmul_cost(rows, 3 * D_MODEL * FFN_DIM + D_MODEL * IN_DIM, 1)


def _ffn_in_kernel(l, n_tiles, tps, xp_ref, xs_ref, g1_ref, g2_ref, cw_ref, cb_ref, lng_ref, lnb_ref,
                   pw_ref, wg_hbm, wu_hbm, wd_hbm, win_hbm,
                   x1p_ref, zp_ref, mixc_ref, conv_ref, x1s_ref, zs_ref,
                   wg_b, wu_b, wd_b, win_b, stage, sem, xn_scr, a_scr, zcg, ubuf, ush, pw_b, cwb):
    i = pl.program_id(0)
    cprev = lax.rem(i + tps - 1, tps)
    dense_w = (g1_ref, g2_ref, wg_b, wu_b, wd_b, win_b)

    @pl.when(i == 0)
    def _():
        _stage_weight(wg_hbm, l, wg_b, stage, sem)
        _stage_weight(wu_hbm, l, wu_b, stage, sem)
        _stage_weight(wd_hbm, l, wd_b, stage, sem)
        _stage_weight(win_hbm, l, win_b, stage, sem, permute_q_cols=True)
        pw_b[...] = pw_ref[l].astype(BF16)
        for k in range(CONV_K):
            cwb[k] = jnp.broadcast_to(cw_ref[l, k:k + 1, :], (8, CONV_CH))

    @pl.when((i > 0) & (cprev == 0))
    def _():
        ubuf[0:CONV_PAD, :] = jnp.zeros((CONV_PAD, CONV_CH), F32)

    @pl.when((i > 0) & (cprev > 0))
    def _():
        ubuf[0:CONV_PAD, :] = ubuf[TOKEN_TILE:TOKEN_TILE + CONV_PAD, :]

    def dense():
        return _ffn_in_steps(l, xp_ref, x1p_ref, zp_ref, *dense_w, xn_scr, a_scr)

    def conv():
        return _conv_tile_steps(l, zcg, mixc_ref, cwb, cb_ref, lng_ref, lnb_ref, ubuf, ush, pw_b)

    @pl.when(i == 0)
    def _():
        _run(dense())

    @pl.when((i > 0) & (i < n_tiles))
    def _():
        _interleave(dense(), conv(), _ffn_in_cost(TOKEN_TILE), CONV_TILE_COST)

    @pl.when(i == n_tiles)
    def _():
        _run(conv())
        ns = xs_ref.shape[0]
        _run(_ffn_in_steps(l, xs_ref, x1s_ref, zs_ref, *dense_w, xn_scr.at[0:ns], a_scr.at[0:ns]))

    @pl.when(i < n_tiles)
    def _():
        zcg[...] = zp_ref[:, C_CA:C_CA + 2 * CONV_CH]

    @pl.when((i > 0) & (cprev == tps - 1))
    def _():
        conv_ref[0] = ubuf[CONV_PAD + TOKEN_TILE - (CONV_K - 1):CONV_PAD + TOKEN_TILE, :]


def _stage_scratch():
    return [pltpu.VMEM((2, STAGE_ROWS, FFN_DIM), F32), pltpu.SemaphoreType.DMA((2,))]


def _ffn_in(l, nb, xp, xs, g1, g2, cw, cb, lng, lnb, pw, wg, wu, wd, win):
    rows, ns = xp.shape[0], xs.shape[0]
    n_tiles = rows // TOKEN_TILE
    tps = n_tiles // nb
    cur = lambda w: pl.BlockSpec((TOKEN_TILE, w), lambda i: (jnp.minimum(i, n_tiles - 1), 0))
    prev = lambda w: pl.BlockSpec((TOKEN_TILE, w), lambda i: (jnp.maximum(i - 1, 0), 0))
    once = lambda w: pl.BlockSpec((ns, w), lambda i: (0, 0), pipeline_mode=pl.Buffered(1))
    whole = lambda a: pl.BlockSpec(a.shape, lambda i: (0,) * a.ndim)
    hbm = pl.BlockSpec(memory_space=pl.ANY)
    return pl.pallas_call(
        functools.partial(_ffn_in_kernel, l, n_tiles, tps),
        grid=(n_tiles + 1,),
        in_specs=[cur(D_MODEL), once(D_MODEL), whole(g1), whole(g2), whole(cw), whole(cb),
                  whole(lng), whole(lnb), whole(pw), hbm, hbm, hbm, hbm],
        out_specs=[cur(D_MODEL), cur(IN_DIM), prev(CONV_CH),
                   pl.BlockSpec((1, CONV_K - 1, CONV_CH),
                                lambda i: (jnp.maximum(i - 1, 0) // tps, 0, 0)),
                   once(D_MODEL), once(IN_DIM)],
        out_shape=[jax.ShapeDtypeStruct((rows, D_MODEL), F32),
                   jax.ShapeDtypeStruct((rows, IN_DIM), F32),
                   jax.ShapeDtypeStruct((rows, CONV_CH), BF16),
                   jax.ShapeDtypeStruct((nb, CONV_K - 1, CONV_CH), F32),
                   jax.ShapeDtypeStruct((ns, D_MODEL), F32),
                   jax.ShapeDtypeStruct((ns, IN_DIM), F32)],
        scratch_shapes=[pltpu.VMEM((D_MODEL, FFN_DIM), BF16), pltpu.VMEM((D_MODEL, FFN_DIM), BF16),
                        pltpu.VMEM((FFN_DIM, D_MODEL), BF16), pltpu.VMEM((D_MODEL, IN_DIM), BF16),
                        *_stage_scratch(),
                        pltpu.VMEM((TOKEN_TILE, D_MODEL), BF16),
                        pltpu.VMEM((TOKEN_TILE, FFN_DIM), BF16),
                        pltpu.VMEM((TOKEN_TILE, 2 * CONV_CH), F32),
                        pltpu.VMEM((CONV_PAD + TOKEN_TILE, CONV_CH), F32),
                        pltpu.VMEM((2, 7, CONV_SPAN, CONV_CH), F32),
                        pltpu.VMEM((CONV_CH, CONV_CH), BF16),
                        pltpu.VMEM((CONV_K, 8, CONV_CH), F32)],
        compiler_params=pltpu.CompilerParams(dimension_semantics=("arbitrary",),
                                             vmem_limit_bytes=V7X_VMEM_LIMIT),
        name="ffn_in",
    )(xp, xs, g1, g2, cw, cb, lng, lnb, pw, wg, wu, wd, win)


def _mix_constants(dmask, qdec, kdec, abias):
    T = RET_CHUNK
    row_f = lax.broadcasted_iota(jnp.int32, (T, T), 0).astype(F32)
    col_f = lax.broadcasted_iota(jnp.int32, (T, T), 1).astype(F32)
    low = lax.broadcasted_iota(jnp.int32, (T, T), 1) < 64
    for h in range(RET_HEADS):
        lg = LOG_DECAY[h]
        diff = row_f - col_f
        dmask[h] = jnp.where(diff >= 0, jnp.exp(lg * jnp.maximum(diff, 0.0)), 0.0)
        qdec[h] = jnp.exp(lg * (row_f + 1.0))
    for p in range(RET_HEADS // 2):
        lgp = jnp.where(low, LOG_DECAY[2 * p], LOG_DECAY[2 * p + 1])
        kdec[p] = jnp.exp(lgp * (T - 1.0 - row_f))
    qpos = lax.broadcasted_iota(jnp.int32, (T, 2 * T), 0) + T
    kpos = lax.broadcasted_iota(jnp.int32, (T, 2 * T), 1)
    dist = (qpos - kpos).astype(F32)
    for h in range(ATT_HEADS):
        abias[h] = jnp.where((dist >= 0) & (dist < WINDOW), -SLOPES[h] * dist, NEG_INF)


def _mix_tile_steps(l, first, z_ref, mix_ref, sinks_ref, rng_ref, qg_ref, kg_ref,
                    s_scr, kbuf, vbuf, qbuf, klast, dmask, qdec, kdec, abias):
    T = RET_CHUNK
    NCH = TOKEN_TILE // T
    row_i = lax.broadcasted_iota(jnp.int32, (T, T), 0)
    low = lax.broadcasted_iota(jnp.int32, (T, T), 1) < 64

    def zc(r0, nrows, a, w):
        return z_ref[r0:r0 + nrows, a:a + w]

    def seg_mean_sq(x):
        low_x = lax.broadcasted_iota(jnp.int32, x.shape, 1) < 64
        xx = x * x
        s_lo = jnp.sum(jnp.where(low_x, xx, 0.0), axis=-1, keepdims=True)
        s_hi = jnp.sum(jnp.where(low_x, 0.0, xx), axis=-1, keepdims=True)
        return jnp.where(low_x, s_lo, s_hi) * (1.0 / HEAD_DIM)

    def retention_unit(p, j):
        s_cur = s_scr[p]
        r0 = T * j
        qp = zc(r0, T, C_RQ + 128 * p, 128)
        kp = zc(r0, T, C_RK + 128 * p, 128) * (RET_DK ** -0.5)
        vb = zc(r0, T, C_RV + 256 * p, 256).astype(BF16)
        kb = kp.astype(BF16)
        sb = s_cur.astype(BF16)
        qm = jnp.concatenate([jnp.where(low, qp, 0.0), jnp.where(low, 0.0, qp)], axis=0).astype(BF16)
        a = _dot_nt(qm, kb)
        qs = _dot(qm, sb)
        upd = _dot_tn((kp * kdec[p]).astype(BF16), vb)
        yield RET_COST // 4
        ab = (a * dmask[2 * p:2 * p + 2].reshape(2 * T, T)).astype(BF16)
        yield RET_COST // 4
        o = _dot(ab, vb)
        yield RET_COST // 4
        for e in range(2):
            h = 2 * p + e
            oh = _rms(o[T * e:T * e + T, 128 * e:128 * e + 128] + qs[T * e:T * e + T, :] * qdec[h],
                      rng_ref[l, h:h + 1, :])
            mix_ref[r0:r0 + T, M_RET + 128 * h:M_RET + 128 * h + 128] = (
                oh * _silu(zc(r0, T, C_RG + 128 * h, 128))).astype(BF16)
        top = row_i < 64
        new = jnp.where(top, upd[:, 0:128], upd[:, 128:256])
        cd = jnp.where(top, math.exp(LOG_DECAY[2 * p] * T), math.exp(LOG_DECAY[2 * p + 1] * T))
        s_scr[p] = s_cur * cd + new
        yield RET_COST // 4

    def attention_prep():
        ak = zc(0, TOKEN_TILE, C_AK, 128)
        aq = [zc(0, TOKEN_TILE, C_AQ + 128 * t, 128) for t in range(2)]
        ms_k = seg_mean_sq(ak)
        ms_q = [seg_mean_sq(aq[t]) for t in range(2)]
        yield ATT_PREP_COST // 2
        kn = ak * lax.rsqrt(ms_k + EPS) * kg_ref[l:l + 1, :]
        klast[...] = kn[TOKEN_TILE - T:TOKEN_TILE, :]
        kbuf[T:T + TOKEN_TILE, :] = kn.astype(BF16)
        vbuf[T:T + TOKEN_TILE, :] = zc(0, TOKEN_TILE, C_AV, 128).astype(BF16)
        for t in range(2):
            qt = aq[t] * lax.rsqrt(ms_q[t] + EPS) * qg_ref[l:l + 1, :] * (HEAD_DIM ** -0.5)
            for j in range(NCH):
                qj = qt[T * j:T * j + T, :]
                qbuf[j, (2 * t) * T:(2 * t + 1) * T, :] = jnp.where(low, qj, 0.0).astype(BF16)
                qbuf[j, (2 * t + 1) * T:(2 * t + 2) * T, :] = jnp.where(low, 0.0, qj).astype(BF16)
        yield ATT_PREP_COST // 2

    def attention_unit(j):
        r0 = T * j
        kcat = kbuf[r0:r0 + 2 * T, :]
        vcat = vbuf[r0:r0 + 2 * T, :]
        s = _dot_nt(qbuf[j], kcat)
        yield ATT_COST // 2
        probs = []
        for t in range(2):
            for kvh in range(KV_HEADS):
                head = 2 * kvh + t
                sink = sinks_ref[l, head]
                bias = abias[head]
                if j == 0:
                    no_prev = lax.broadcasted_iota(jnp.int32, (T, 2 * T), 1) < jnp.where(first, T, 0)
                    bias = jnp.where(no_prev, NEG_INF, bias)
                blk = 2 * t + kvh
                sb = s[T * blk:T * blk + T, :] + bias
                m = jnp.maximum(jnp.max(sb, axis=-1, keepdims=True), sink)
                e = jnp.exp(sb - m)
                inv = 1.0 / (jnp.sum(e, axis=-1, keepdims=True) + jnp.exp(sink - m))
                probs.append((e * inv).astype(BF16))
            yield ATT_COST
        o = _dot(jnp.concatenate(probs, axis=0), vcat)
        for t in range(2):
            mix_ref[r0:r0 + T, M_ATT + 128 * t:M_ATT + 128 * t + 128] = jnp.where(
                low, o[T * 2 * t:T * (2 * t + 1), :], o[T * (2 * t + 1):T * (2 * t + 2), :]
            ).astype(BF16)
        yield ATT_COST // 2

    def ret_lane():
        for j in range(NCH):
            for p in range(RET_HEADS // 2):
                yield from retention_unit(p, j)

    def att_lane():
        yield from attention_prep()
        for j in range(NCH):
            yield from attention_unit(j)

    live = [att_lane(), ret_lane()]
    while live:
        for lane in list(live):
            cost = next(lane, None)
            if cost is None:
                live.remove(lane)
            else:
                yield cost


def _conv_tile_steps(l, zcg_ref, out_ref, cwb, cb_ref, lng_ref, lnb_ref, ubuf, ush, pw_b):
    ubuf[CONV_PAD:CONV_PAD + TOKEN_TILE, :] = (
        zcg_ref[:, 0:CONV_CH] * jax.nn.sigmoid(zcg_ref[:, CONV_CH:2 * CONV_CH]))
    yield CONV_PREP_COST
    for rb in range(TOKEN_TILE // CONV_ROWS):
        base = rb * CONV_ROWS
        sh = ush.at[rb % 2]
        for r in range(1, 8):
            sh[r - 1] = ubuf[base + r:base + r + CONV_SPAN, :]
        acc = None
        for k in range(CONV_K):
            off = CONV_PAD - (CONV_K - 1) + k
            a8, r = off // 8 * 8, off % 8
            src = (ubuf[base + a8:base + a8 + CONV_ROWS, :] if r == 0
                   else sh[r - 1, a8:a8 + CONV_ROWS, :])
            term = cwb[k][None] * src.reshape(CONV_ROWS // 8, 8, CONV_CH)
            acc = term if acc is None else acc + term
        y = _silu(_layer_norm(acc.reshape(CONV_ROWS, CONV_CH) + cb_ref[l:l + 1, :],
                              lng_ref[l:l + 1, :], lnb_ref[l:l + 1, :])).astype(BF16)
        yield CONV_COST - CONV_COST // 8
        out_ref[base:base + CONV_ROWS, :] = _dot(y, pw_b[...]).astype(BF16)
        yield CONV_COST // 8


RET_COST, ATT_PREP_COST, ATT_COST, CONV_PREP_COST, CONV_COST = 160, 250, 220, 200, 410
MIX_COST = ((RET_HEADS // 2) * (TOKEN_TILE // RET_CHUNK) * RET_COST + ATT_PREP_COST
            + 3 * (TOKEN_TILE // RET_CHUNK) * ATT_COST)
CONV_TILE_COST = CONV_PREP_COST + (TOKEN_TILE // CONV_ROWS) * CONV_COST


def _mix_out_ffn_kernel(l, n_tiles, tps, sinks_ref, z_ref, x1p_ref, mixc_ref, x1s_ref, mixs_ref, g_ref,
                        rng_ref, qg_ref, kg_ref, wout_hbm, wg_hbm, wu_hbm, wd_hbm,
                        x3p_ref, x3s_ref, ret_ref, kwin_ref, vwin_ref,
                        wout_b, wg_b, wu_b, wd_b, stage, sem, x2_scr, xn_scr, a_scr, mix_new, mix_old,
                        s_scr, kbuf, vbuf, qbuf, klast, dmask, qdec, kdec, abias):
    i = pl.program_id(0)
    c = lax.rem(i, tps)
    T = RET_CHUNK
    mix_args = (sinks_ref, rng_ref, qg_ref, kg_ref,
                s_scr, kbuf, vbuf, qbuf, klast, dmask, qdec, kdec, abias)
    dense_w = (g_ref, wout_b, wg_b, wu_b, wd_b)

    @pl.when(i == 0)
    def _():
        _stage_w_out(wout_hbm, l, wout_b, stage, sem)
        _stage_weight(wg_hbm, l, wg_b, stage, sem)
        _stage_weight(wu_hbm, l, wu_b, stage, sem)
        _stage_weight(wd_hbm, l, wd_b, stage, sem)
        _mix_constants(dmask, qdec, kdec, abias)

    @pl.when(c == 0)
    def _():
        s_scr[...] = jnp.zeros_like(s_scr)
        kbuf[0:T, :] = jnp.zeros((T, 128), BF16)
        vbuf[0:T, :] = jnp.zeros((T, 128), BF16)

    @pl.when(c > 0)
    def _():
        kbuf[0:T, :] = kbuf[TOKEN_TILE:TOKEN_TILE + T, :]
        vbuf[0:T, :] = vbuf[TOKEN_TILE:TOKEN_TILE + T, :]

    @pl.when(i > 0)
    def _():
        mix_old[:, 0:M_CONV] = mix_new[:, 0:M_CONV]
        mix_old[:, M_CONV:MIX_DIM] = mixc_ref[...]

    def mixers():
        return _mix_tile_steps(l, c == 0, z_ref, mix_new, *mix_args)

    def dense():
        return _out_ffn_steps(l, x1p_ref, mix_old, x3p_ref, *dense_w, x2_scr, xn_scr, a_scr)

    @pl.when(i == 0)
    def _():
        _run(mixers())

    @pl.when((i > 0) & (i < n_tiles))
    def _():
        _interleave(dense(), mixers(), _out_ffn_cost(TOKEN_TILE), MIX_COST)

    @pl.when(i == n_tiles)
    def _():
        _run(dense())
        ns = x1s_ref.shape[0]
        _run(_out_ffn_steps(l, x1s_ref, mixs_ref, x3s_ref, *dense_w,
                            x2_scr.at[0:ns], xn_scr.at[0:ns], a_scr.at[0:ns]))

    @pl.when((c == tps - 1) & (i < n_tiles))
    def _():
        ret_ref[0] = s_scr[...]
        kwin_ref[0] = klast[...].T
        vwin_ref[0] = z_ref[TOKEN_TILE - T:TOKEN_TILE, C_AV:C_AV + 128].T


def _mix_out_ffn(l, nb, z, x1p, mixc, x1s, mixs, g, sinks, rng, qg2, kg2, wout, wg, wu, wd):
    rows, ns = x1p.shape[0], x1s.shape[0]
    n_tiles = rows // TOKEN_TILE
    tps = n_tiles // nb
    T = RET_CHUNK
    cur = lambda w: pl.BlockSpec((TOKEN_TILE, w), lambda i: (jnp.minimum(i, n_tiles - 1), 0))
    prev = lambda w: pl.BlockSpec((TOKEN_TILE, w), lambda i: (jnp.maximum(i - 1, 0), 0))
    once = lambda w: pl.BlockSpec((ns, w), lambda i: (0, 0), pipeline_mode=pl.Buffered(1))
    whole = lambda a: pl.BlockSpec(a.shape, lambda i: (0,) * a.ndim)
    per_seq = lambda shape: pl.BlockSpec(
        (1,) + shape, lambda i: (jnp.minimum(i, n_tiles - 1) // tps,) + (0,) * len(shape))
    hbm = pl.BlockSpec(memory_space=pl.ANY)
    return pl.pallas_call(
        functools.partial(_mix_out_ffn_kernel, l, n_tiles, tps),
        grid=(n_tiles + 1,),
        in_specs=[pl.BlockSpec(memory_space=pltpu.SMEM), cur(IN_DIM), prev(D_MODEL), prev(CONV_CH),
                  once(D_MODEL), once(MIX_DIM), whole(g),
                  whole(rng), whole(qg2), whole(kg2), hbm, hbm, hbm, hbm],
        out_specs=[prev(D_MODEL), once(D_MODEL),
                   per_seq((2, 128, RET_DV)), per_seq((128, WINDOW)), per_seq((128, WINDOW))],
        out_shape=[jax.ShapeDtypeStruct((rows, D_MODEL), F32),
                   jax.ShapeDtypeStruct((ns, D_MODEL), F32),
                   jax.ShapeDtypeStruct((nb, 2, 128, RET_DV), F32),
                   jax.ShapeDtypeStruct((nb, 128, WINDOW), F32),
                   jax.ShapeDtypeStruct((nb, 128, WINDOW), F32)],
        scratch_shapes=[pltpu.VMEM((MIX_DIM, D_MODEL), BF16),
                        pltpu.VMEM((D_MODEL, FFN_DIM), BF16), pltpu.VMEM((D_MODEL, FFN_DIM), BF16),
                        pltpu.VMEM((FFN_DIM, D_MODEL), BF16),
                        *_stage_scratch(),
                        pltpu.VMEM((TOKEN_TILE, D_MODEL), F32),
                        pltpu.VMEM((TOKEN_TILE, D_MODEL), BF16),
                        pltpu.VMEM((TOKEN_TILE, FFN_DIM), BF16),
                        pltpu.VMEM((TOKEN_TILE, MIX_DIM), BF16),
                        pltpu.VMEM((TOKEN_TILE, MIX_DIM), BF16),
                        pltpu.VMEM((2, 128, RET_DV), F32),
                        pltpu.VMEM((T + TOKEN_TILE, 128), BF16),
                        pltpu.VMEM((T + TOKEN_TILE, 128), BF16),
                        pltpu.VMEM((TOKEN_TILE // T, 2 * KV_HEADS * T, 128), BF16),
                        pltpu.VMEM((T, 128), F32),
                        pltpu.VMEM((RET_HEADS, T, T), F32), pltpu.VMEM((RET_HEADS, T, T), F32),
                        pltpu.VMEM((2, T, T), F32), pltpu.VMEM((ATT_HEADS, T, 2 * T), F32)],
        compiler_params=pltpu.CompilerParams(dimension_semantics=("arbitrary",),
                                             vmem_limit_bytes=V7X_VMEM_LIMIT),
        name="mix_out_ffn",
    )(sinks, z, x1p, mixc, x1s, mixs, g, rng, qg2, kg2, wout, wg, wu, wd)


def _mix_sample_kernel(l, sinks_ref, z_ref, col_ref, s_ref, ck_ref, cv_ref, sc_ref,
                       rng_ref, qg_ref, kg_ref, kgc_ref, cw_ref, cb_ref, lng_ref, lnb_ref, pw_ref,
                       *rest):
    mix_ref, so_ref, cko_ref, cvo_ref, sco_ref, o_scr, oa_scr = rest[-7:]
    NB = SAMPLE_BLOCK
    P = WINDOW

    def put(ref, idx, val):
        for slab in range(ref.shape[0]):
            ref[(slab,) + idx] = val

    for bl in range(NB):
        for h in range(RET_HEADS):
            gamma = math.exp(LOG_DECAY[h])
            r0 = RET_DK * h
            S = s_ref[0, bl, h]
            qc = col_ref[0, r0:r0 + RET_DK, bl:bl + 1]
            kc = col_ref[0, 256 + r0:256 + r0 + RET_DK, bl:bl + 1] * (RET_DK ** -0.5)
            v = z_ref[bl:bl + 1, C_RV + 128 * h:C_RV + 128 * h + 128]
            qk = jnp.sum(qc * kc, axis=0, keepdims=True)
            o_scr[bl:bl + 1, 128 * h:128 * h + 128] = (
                gamma * jnp.sum(qc * S, axis=0, keepdims=True) + qk * v)
            put(so_ref, (bl, h), gamma * S + kc * v)
    for h in range(RET_HEADS):
        o = _rms(o_scr[:, 128 * h:128 * h + 128], rng_ref[l, h:h + 1, :])
        mix_ref[:, M_RET + 128 * h:M_RET + 128 * h + 128] = (
            o * _silu(z_ref[:, C_RG + 128 * h:C_RG + 128 * h + 128]))

    q_tiles = [_seg_rms(z_ref[:, C_AQ + 128 * t:C_AQ + 128 * t + 128], qg_ref[l:l + 1, :])
               * (HEAD_DIM ** -0.5) for t in range(2)]
    kn = _seg_rms(z_ref[:, C_AK:C_AK + 128], kg_ref[l:l + 1, :])
    vn = z_ref[:, C_AV:C_AV + 128]
    akc = col_ref[0, 512:640, :]
    vnc = col_ref[0, 640:768, :]
    ms = jnp.concatenate(
        [jnp.broadcast_to(jnp.mean(akc[64 * s:64 * s + 64, :] ** 2, axis=0, keepdims=True), (64, NB))
         for s in range(KV_HEADS)], axis=0)
    knc = akc * lax.rsqrt(ms + EPS) * kgc_ref[l]
    rid = lax.broadcasted_iota(jnp.int32, (8, 128), 0)
    lane = lax.broadcasted_iota(jnp.int32, (8, 128), 1)
    sel = ((rid % 2 == 0) == (lane < 64)) & (rid < ATT_HEADS)
    rcol = lax.broadcasted_iota(jnp.int32, (8, 1), 0)
    heads = [2 * (r % 2) + r // 2 for r in range(ATT_HEADS)]
    slope_col = jnp.zeros((8, 1), F32)
    sink_col = jnp.zeros((8, 1), F32)
    for r, hd in enumerate(heads):
        slope_col = jnp.where(rcol == r, SLOPES[hd], slope_col)
        sink_col = jnp.where(rcol == r, sinks_ref[l, hd], sink_col)
    dist = float(P) - lane.astype(F32)
    key_ok = (dist < float(WINDOW)) & (lane + (PAST_LEN - P) >= 0)
    newest = lax.broadcasted_iota(jnp.int32, (128, P), 1) == P - 1
    for bl in range(NB):
        KT = ck_ref[0, bl].reshape(128, P)
        VT = cv_ref[0, bl].reshape(128, P)
        qrows = jnp.where(rid < 2, q_tiles[0][bl:bl + 1, :], q_tiles[1][bl:bl + 1, :])
        q4 = jnp.where(sel, qrows, 0.0)
        s = _dot(q4.astype(BF16), KT.astype(BF16))
        s = jnp.where(key_ok, s - slope_col * dist, NEG_INF)
        s_new = jnp.sum(q4 * kn[bl:bl + 1, :], axis=-1, keepdims=True)
        m = jnp.maximum(jnp.maximum(jnp.max(s, axis=-1, keepdims=True), s_new), sink_col)
        e = jnp.exp(s - m)
        e_new = jnp.exp(s_new - m)
        inv = 1.0 / (jnp.sum(e, axis=-1, keepdims=True) + e_new + jnp.exp(sink_col - m))
        o = _dot_nt((e * inv).astype(BF16), VT.astype(BF16)) + (e_new * inv) * vn[bl:bl + 1, :]
        for t in range(2):
            oa_scr[bl:bl + 1, 128 * t:128 * t + 128] = jnp.where(
                lane[0:1, :] < 64, o[2 * t:2 * t + 1, :], o[2 * t + 1:2 * t + 2, :])
        put(cko_ref, (bl,), jnp.where(newest, knc[:, bl:bl + 1], pltpu.roll(KT, P - 1, 1)).reshape(
            KV_HEADS, HEAD_DIM, P))
        put(cvo_ref, (bl,), jnp.where(newest, vnc[:, bl:bl + 1], pltpu.roll(VT, P - 1, 1)).reshape(
            KV_HEADS, HEAD_DIM, P))
    mix_ref[:, M_ATT:M_ATT + 256] = oa_scr[...]

    u = z_ref[:, C_CA:C_CA + CONV_CH] * jax.nn.sigmoid(z_ref[:, C_CG:C_CG + CONV_CH])
    KT1 = CONV_K - 1
    y = cw_ref[l, KT1:KT1 + 1, :] * u
    for k in range(KT1):
        y = y + cw_ref[l, k:k + 1, :] * sc_ref[0, k]
        if k > 0:
            put(sco_ref, (k - 1,), sc_ref[0, k])
    put(sco_ref, (KT1 - 1,), u)
    y = _silu(_layer_norm(y + cb_ref[l:l + 1, :], lng_ref[l:l + 1, :], lnb_ref[l:l + 1, :]))
    mix_ref[:, M_CONV:M_CONV + CONV_CH] = _dot(y.astype(BF16), pw_ref[l].astype(BF16))


def _mix_sample(l, z, cols, s, ckt, cvt, sct, sinks, rng, qg2, kg2, kgc, cw, cb, lng, lnb, pw, prev):
    ns = z.shape[0]
    NB = SAMPLE_BLOCK
    whole = lambda a: pl.BlockSpec(a.shape, lambda i: (0,) * a.ndim)

    def specs(slabs, at):
        return [pl.BlockSpec((slabs, NB, RET_HEADS, RET_DK, RET_DV), lambda i: (at, i, 0, 0, 0)),
                pl.BlockSpec((slabs, NB, KV_HEADS, HEAD_DIM, WINDOW), lambda i: (at, i, 0, 0, 0)),
                pl.BlockSpec((slabs, NB, KV_HEADS, HEAD_DIM, WINDOW), lambda i: (at, i, 0, 0, 0)),
                pl.BlockSpec((slabs, CONV_K - 1, NB, CONV_CH), lambda i: (at, 0, i, 0))]

    state_specs = specs(1, l)
    out_state_specs = specs(s.shape[0], 0) if prev is None else state_specs
    in_specs = [pl.BlockSpec(memory_space=pltpu.SMEM),
                pl.BlockSpec((NB, IN_DIM), lambda i: (i, 0)),
                pl.BlockSpec((1, 768, NB), lambda i: (i, 0, 0)),
                *state_specs,
                whole(rng), whole(qg2), whole(kg2), whole(kgc), whole(cw), whole(cb), whole(lng),
                whole(lnb), whole(pw)]
    args = [sinks, z, cols, s, ckt, cvt, sct, rng, qg2, kg2, kgc, cw, cb, lng, lnb, pw]
    aliases = {}
    if prev is not None:
        for k, a in enumerate(prev):
            aliases[len(args)] = 1 + k
            in_specs.append(pl.BlockSpec(memory_space=pl.ANY))
            args.append(a)
    return pl.pallas_call(
        functools.partial(_mix_sample_kernel, l),
        grid=(ns // NB,),
        in_specs=in_specs,
        out_specs=[pl.BlockSpec((NB, MIX_DIM), lambda i: (i, 0)), *out_state_specs],
        out_shape=[jax.ShapeDtypeStruct((ns, MIX_DIM), F32),
                   jax.ShapeDtypeStruct(s.shape, F32), jax.ShapeDtypeStruct(ckt.shape, F32),
                   jax.ShapeDtypeStruct(cvt.shape, F32), jax.ShapeDtypeStruct(sct.shape, F32)],
        scratch_shapes=[pltpu.VMEM((NB, 512), F32), pltpu.VMEM((NB, 256), F32)],
        input_output_aliases=aliases,
        compiler_params=pltpu.CompilerParams(dimension_semantics=("arbitrary",)),
        name="mix_sample",
    )(*args)


def kernel(x_prompt, x_sample, state_ret, cache_k_win, cache_v_win, state_conv, ffn1_norm, ffn1_wg, ffn1_wu, ffn1_wd, mix_norm, w_in, ret_norm_g, q_norm_g, k_norm_g, sinks, conv_w, conv_b, conv_ln_g, conv_ln_b, conv_pw, w_out, ffn2_norm, ffn2_wg, ffn2_wu, ffn2_wd):
    nb, seq, _ = x_prompt.shape
    ns = x_sample.shape[0]
    assert x_sample.shape[1] == 1 and seq % TOKEN_TILE == 0
    assert ns % SAMPLE_BLOCK == 0 and ns <= TOKEN_TILE and cache_k_win.shape[2] == WINDOW
    hp = x_prompt.reshape(nb * seq, D_MODEL)
    hs = x_sample.reshape(ns, D_MODEL)
    ckt = cache_k_win.transpose(0, 1, 3, 4, 2)
    cvt = cache_v_win.transpose(0, 1, 3, 4, 2)
    sct = state_conv.transpose(0, 2, 1, 3)
    qg2 = jnp.tile(q_norm_g, (1, 2))
    kg2 = jnp.tile(k_norm_g, (1, 2))
    kgc = kg2.reshape(DEPTH, 128, 1)
    conv_params = (conv_w, conv_b, conv_ln_g, conv_ln_b, conv_pw)
    prompt_states = [[] for _ in range(4)]
    sample_states = None
    for l in range(DEPTH):
        hp1, zp, mixc, c1, hs1, zs = _ffn_in(l, nb, hp, hs, ffn1_norm, mix_norm, *conv_params,
                                             ffn1_wg, ffn1_wu, ffn1_wd, w_in)
        cols = jnp.concatenate([zs[:, C_RQ:C_RQ + 512], zs[:, C_AK:C_AK + 256]], axis=1)
        cols = cols.T.reshape(768, ns // SAMPLE_BLOCK, SAMPLE_BLOCK).transpose(1, 0, 2)
        mixs, *sample_states = _mix_sample(l, zs, cols, state_ret, ckt, cvt, sct, sinks,
                                           ret_norm_g, qg2, kg2, kgc, conv_w, conv_b, conv_ln_g,
                                           conv_ln_b, conv_pw, sample_states)
        hp, hs, r1, k1, v1 = _mix_out_ffn(l, nb, zp, hp1, mixc, hs1, mixs, ffn2_norm, sinks,
                                          ret_norm_g, qg2, kg2, w_out, ffn2_wg, ffn2_wu, ffn2_wd)
        for lst, val in zip(prompt_states, (r1, k1, v1, c1)):
            lst.append(val)
    ret_p, kwin_p, vwin_p, conv_p = (jnp.stack(s) for s in prompt_states)
    ret_s, kwin_s, vwin_s, conv_s = sample_states
    return (hp.reshape(nb, seq, D_MODEL), hs.reshape(ns, 1, D_MODEL),
            ret_p.reshape(DEPTH, nb, RET_HEADS, RET_DK, RET_DV), ret_s,
            kwin_p.reshape(DEPTH, nb, KV_HEADS, HEAD_DIM, WINDOW).transpose(0, 1, 4, 2, 3),
            kwin_s.transpose(0, 1, 4, 2, 3),
            vwin_p.reshape(DEPTH, nb, KV_HEADS, HEAD_DIM, WINDOW).transpose(0, 1, 4, 2, 3),
            vwin_s.transpose(0, 1, 4, 2, 3),
            conv_p, conv_s.transpose(0, 2, 1, 3))
```

```python
import functools
import math

import jax
import jax.numpy as jnp
from jax import lax
from jax.experimental import pallas as pl
from jax.experimental.pallas import tpu as pltpu

F32 = jnp.float32
BF16 = jnp.bfloat16

D_MODEL = 1024
DEPTH = 2
PAST_LEN = 8192
RET_HEADS = 4
RET_DK = 64
RET_DV = 128
RET_CHUNK = 128
ATT_HEADS = 4
KV_HEADS = 2
HEAD_DIM = 64
WINDOW = 128
CONV_CH = 256
CONV_K = 31
FFN_DIM = 2816
EPS = 1e-6
NEG_INF = -1e30

C_RQ, C_RK, C_RV, C_RG = 0, 256, 512, 1024
C_AQ, C_AK, C_AV, C_CA, C_CG = 1536, 1792, 1920, 2048, 2304
IN_DIM = 2560
MIX_DIM = 1024
M_RET, M_ATT, M_CONV = 0, 512, 768

LOG_DECAY = [math.log1p(-2.0 ** (-5 - h)) for h in range(RET_HEADS)]
SLOPES = [2.0 ** (-8.0 * (h + 1) / ATT_HEADS) for h in range(ATT_HEADS)]

V7X_VMEM_LIMIT = 60 * 1024 * 1024
TOKEN_TILE = 512
FFN_COLS = 256
OUT_COLS = 256
STAGE_ROWS = 128
SAMPLE_BLOCK = 8
CONV_ROWS = 32
CONV_PAD = 32
CONV_SPAN = CONV_ROWS + CONV_PAD - 8


def _dot(a, b):
    return jnp.dot(a, b, preferred_element_type=F32)


def _dot_nt(a, b):
    return lax.dot_general(a, b, (((1,), (1,)), ((), ())), preferred_element_type=F32)


def _dot_tn(a, b):
    return lax.dot_general(a, b, (((0,), (0,)), ((), ())), preferred_element_type=F32)


def _silu(x):
    return x * jax.nn.sigmoid(x)


def _rms(x, g):
    return x * lax.rsqrt(jnp.mean(x * x, axis=-1, keepdims=True) + EPS) * g


def _seg_rms(x, g):
    r = lax.broadcasted_iota(jnp.int32, (128, 128), 0)
    c = lax.broadcasted_iota(jnp.int32, (128, 128), 1)
    seg = jnp.where((r < 64) == (c < 64), 1.0 / HEAD_DIM, 0.0).astype(BF16)
    xx = x * x
    hi = xx.astype(BF16)
    lo = (xx - hi.astype(F32)).astype(BF16)
    ms = _dot(hi, seg) + _dot(lo, seg)
    return x * lax.rsqrt(ms + EPS) * g


def _layer_norm(y, g, b):
    mu = jnp.mean(y, axis=-1, keepdims=True)
    d = y - mu
    var = jnp.mean(d * d, axis=-1, keepdims=True)
    return d * lax.rsqrt(var + EPS) * g + b


def _stage_copy(src, l, j, stage, sem, slot, width):
    return pltpu.make_async_copy(src.at[l, pl.ds(j * STAGE_ROWS, STAGE_ROWS), :],
                                 stage.at[slot, :, pl.ds(0, width)], sem.at[slot])


def _swap_inner_heads(t0, t1):
    low = lax.broadcasted_iota(jnp.int32, t0.shape, 1) < 64
    return (jnp.where(low, t0, pltpu.roll(t1, 64, 1)), jnp.where(low, pltpu.roll(t0, 64, 1), t1))


def _stage_weight(src, l, dst, stage, sem, permute_q_cols=False):
    rows, width = dst.shape
    n = rows // STAGE_ROWS
    _stage_copy(src, l, 0, stage, sem, 0, width).start()

    def body(j, carry):
        slot = lax.rem(j, 2)

        @pl.when(j + 1 < n)
        def _():
            _stage_copy(src, l, j + 1, stage, sem, 1 - slot, width).start()

        _stage_copy(src, l, j, stage, sem, slot, width).wait()
        r0 = pl.multiple_of(j * STAGE_ROWS, STAGE_ROWS)
        dst[pl.ds(r0, STAGE_ROWS), :] = stage[slot, :, 0:width].astype(BF16)
        if permute_q_cols:
            t0, t1 = _swap_inner_heads(stage[slot, :, C_AQ:C_AQ + 128],
                                       stage[slot, :, C_AQ + 128:C_AQ + 256])
            dst[pl.ds(r0, STAGE_ROWS), C_AQ:C_AQ + 128] = t0.astype(BF16)
            dst[pl.ds(r0, STAGE_ROWS), C_AQ + 128:C_AQ + 256] = t1.astype(BF16)
        return carry

    lax.fori_loop(0, n, body, 0)


def _stage_w_out(src, l, dst, stage, sem):
    rows, width = dst.shape
    n = rows // STAGE_ROWS
    _stage_copy(src, l, 0, stage, sem, 0, width).start()
    for j in range(n):
        slot = j % 2
        if j + 1 < n:
            _stage_copy(src, l, j + 1, stage, sem, 1 - slot, width).start()
        _stage_copy(src, l, j, stage, sem, slot, width).wait()
        r0 = j * STAGE_ROWS
        if r0 == M_ATT:
            dst[M_ATT:M_ATT + 64, :] = stage[slot, 0:64, 0:width].astype(BF16)
            dst[M_ATT + 128:M_ATT + 192, :] = stage[slot, 64:128, 0:width].astype(BF16)
        elif r0 == M_ATT + 128:
            dst[M_ATT + 64:M_ATT + 128, :] = stage[slot, 0:64, 0:width].astype(BF16)
            dst[M_ATT + 192:M_ATT + 256, :] = stage[slot, 64:128, 0:width].astype(BF16)
        else:
            dst[r0:r0 + STAGE_ROWS, :] = stage[slot, :, 0:width].astype(BF16)


def _run(steps):
    for _ in steps:
        pass


def _interleave(major, minor, major_total, minor_total):
    major_done = minor_done = 0
    for cost in major:
        major_done += cost
        while minor_done * major_total < major_done * minor_total:
            step = next(minor, None)
            if step is None:
                break
            minor_done += step
    _run(minor)


def _matmul_cost(rows, k, n):
    return rows * k * n // (256 * 1024)


def _ffn_steps(x_ref, g, wg_ref, wu_ref, wd_ref, out_ref, xn_scr, a_scr):
    rows = x_ref.shape[0]
    xn_scr[...] = _rms(x_ref[...], g).astype(BF16)
    for c in range(FFN_DIM // FFN_COLS):
        sl = slice(c * FFN_COLS, (c + 1) * FFN_COLS)
        gate = _dot(xn_scr[...], wg_ref[:, sl])
        up = _dot(xn_scr[...], wu_ref[:, sl])
        a_scr[:, sl] = (_silu(gate) * up).astype(BF16)
        yield 2 * _matmul_cost(rows, D_MODEL, FFN_COLS)
    for c in range(D_MODEL // OUT_COLS):
        sl = slice(c * OUT_COLS, (c + 1) * OUT_COLS)
        out_ref[:, sl] = x_ref[:, sl] + 0.5 * _dot(a_scr[...], wd_ref[:, sl])
        yield _matmul_cost(rows, FFN_DIM, OUT_COLS)


def _out_ffn_steps(l, x1_ref, mix_ref, x3_ref, g_ref, wout_b, wg_b, wu_b, wd_b, x2, xn, a):
    rows = x1_ref.shape[0]
    mix = mix_ref[...].astype(BF16)
    for c in range(D_MODEL // OUT_COLS):
        sl = slice(c * OUT_COLS, (c + 1) * OUT_COLS)
        x2[:, sl] = x1_ref[:, sl] + _dot(mix, wout_b[:, sl])
        yield _matmul_cost(rows, MIX_DIM, OUT_COLS)
    yield from _ffn_steps(x2, g_ref[l:l + 1, :], wg_b, wu_b, wd_b, x3_ref, xn, a)


def _out_ffn_cost(rows):
    return _matmul_cost(rows, MIX_DIM * D_MODEL + 3 * D_MODEL * FFN_DIM, 1)


def _ffn_in_steps(l, x_ref, x1_ref, z_ref, g1_ref, g2_ref, wg_b, wu_b, wd_b, win_b, xn, a):
    rows = x_ref.shape[0]
    yield from _ffn_steps(x_ref, g1_ref[l:l + 1, :], wg_b, wu_b, wd_b, x1_ref, xn, a)
    xn[...] = _rms(x1_ref[...], g2_ref[l:l + 1, :]).astype(BF16)
    for c in range(IN_DIM // OUT_COLS):
        sl = slice(c * OUT_COLS, (c + 1) * OUT_COLS)
        z_ref[:, sl] = _dot(xn[...], win_b[:, sl])
        yield _matmul_cost(rows, D_MODEL, OUT_COLS)


def _ffn_in_cost(rows):
    return _matmul_cost(rows, 3 * D_MODEL * FFN_DIM + D_MODEL * IN_DIM, 1)


def _ffn_in_kernel(l, n_tiles, tps, xp_ref, xs_ref, g1_ref, g2_ref, cw_ref, cb_ref, lng_ref, lnb_ref,
                   pw_ref, wg_hbm, wu_hbm, wd_hbm, win_hbm,
                   x1p_ref, zp_ref, mixc_ref, conv_ref, x1s_ref, zs_ref,
                   wg_b, wu_b, wd_b, win_b, stage, sem, xn_scr, a_scr, zcg, ubuf, ush, pw_b, cwb):
    i = pl.program_id(0)
    cprev = lax.rem(i + tps - 1, tps)
    dense_w = (g1_ref, g2_ref, wg_b, wu_b, wd_b, win_b)

    @pl.when(i == 0)
    def _():
        _stage_weight(wg_hbm, l, wg_b, stage, sem)
        _stage_weight(wu_hbm, l, wu_b, stage, sem)
        _stage_weight(wd_hbm, l, wd_b, stage, sem)
        _stage_weight(win_hbm, l, win_b, stage, sem, permute_q_cols=True)
        pw_b[...] = pw_ref[l].astype(BF16)
        for k in range(CONV_K):
            cwb[k] = jnp.broadcast_to(cw_ref[l, k:k + 1, :], (8, CONV_CH))

    @pl.when((i > 0) & (cprev == 0))
    def _():
        ubuf[0:CONV_PAD, :] = jnp.zeros((CONV_PAD, CONV_CH), F32)

    @pl.when((i > 0) & (cprev > 0))
    def _():
        ubuf[0:CONV_PAD, :] = ubuf[TOKEN_TILE:TOKEN_TILE + CONV_PAD, :]

    def dense():
        return _ffn_in_steps(l, xp_ref, x1p_ref, zp_ref, *dense_w, xn_scr, a_scr)

    def conv():
        return _conv_tile_steps(l, zcg, mixc_ref, cwb, cb_ref, lng_ref, lnb_ref, ubuf, ush, pw_b)

    @pl.when(i == 0)
    def _():
        _run(dense())

    @pl.when((i > 0) & (i < n_tiles))
    def _():
        _interleave(dense(), conv(), _ffn_in_cost(TOKEN_TILE), CONV_TILE_COST)

    @pl.when(i == n_tiles)
    def _():
        _run(conv())
        ns = xs_ref.shape[0]
        _run(_ffn_in_steps(l, xs_ref, x1s_ref, zs_ref, *dense_w, xn_scr.at[0:ns], a_scr.at[0:ns]))

    @pl.when(i < n_tiles)
    def _():
        zcg[...] = zp_ref[:, C_CA:C_CA + 2 * CONV_CH]

    @pl.when((i > 0) & (cprev == tps - 1))
    def _():
        conv_ref[0] = ubuf[CONV_PAD + TOKEN_TILE - (CONV_K - 1):CONV_PAD + TOKEN_TILE, :]


def _stage_scratch():
    return [pltpu.VMEM((2, STAGE_ROWS, FFN_DIM), F32), pltpu.SemaphoreType.DMA((2,))]


def _ffn_in(l, nb, xp, xs, g1, g2, cw, cb, lng, lnb, pw, wg, wu, wd, win):
    rows, ns = xp.shape[0], xs.shape[0]
    n_tiles = rows // TOKEN_TILE
    tps = n_tiles // nb
    cur = lambda w: pl.BlockSpec((TOKEN_TILE, w), lambda i: (jnp.minimum(i, n_tiles - 1), 0))
    prev = lambda w: pl.BlockSpec((TOKEN_TILE, w), lambda i: (jnp.maximum(i - 1, 0), 0))
    once = lambda w: pl.BlockSpec((ns, w), lambda i: (0, 0), pipeline_mode=pl.Buffered(1))
    whole = lambda a: pl.BlockSpec(a.shape, lambda i: (0,) * a.ndim)
    hbm = pl.BlockSpec(memory_space=pl.ANY)
    return pl.pallas_call(
        functools.partial(_ffn_in_kernel, l, n_tiles, tps),
        grid=(n_tiles + 1,),
        in_specs=[cur(D_MODEL), once(D_MODEL), whole(g1), whole(g2), whole(cw), whole(cb),
                  whole(lng), whole(lnb), whole(pw), hbm, hbm, hbm, hbm],
        out_specs=[cur(D_MODEL), cur(IN_DIM), prev(CONV_CH),
                   pl.BlockSpec((1, CONV_K - 1, CONV_CH),
                                lambda i: (jnp.maximum(i - 1, 0) // tps, 0, 0)),
                   once(D_MODEL), once(IN_DIM)],
        out_shape=[jax.ShapeDtypeStruct((rows, D_MODEL), F32),
                   jax.ShapeDtypeStruct((rows, IN_DIM), F32),
                   jax.ShapeDtypeStruct((rows, CONV_CH), BF16),
                   jax.ShapeDtypeStruct((nb, CONV_K - 1, CONV_CH), F32),
                   jax.ShapeDtypeStruct((ns, D_MODEL), F32),
                   jax.ShapeDtypeStruct((ns, IN_DIM), F32)],
        scratch_shapes=[pltpu.VMEM((D_MODEL, FFN_DIM), BF16), pltpu.VMEM((D_MODEL, FFN_DIM), BF16),
                        pltpu.VMEM((FFN_DIM, D_MODEL), BF16), pltpu.VMEM((D_MODEL, IN_DIM), BF16),
                        *_stage_scratch(),
                        pltpu.VMEM((TOKEN_TILE, D_MODEL), BF16),
                        pltpu.VMEM((TOKEN_TILE, FFN_DIM), BF16),
                        pltpu.VMEM((TOKEN_TILE, 2 * CONV_CH), F32),
                        pltpu.VMEM((CONV_PAD + TOKEN_TILE, CONV_CH), F32),
                        pltpu.VMEM((2, 7, CONV_SPAN, CONV_CH), F32),
                        pltpu.VMEM((CONV_CH, CONV_CH), BF16),
                        pltpu.VMEM((CONV_K, 8, CONV_CH), F32)],
        compiler_params=pltpu.CompilerParams(dimension_semantics=("arbitrary",),
                                             vmem_limit_bytes=V7X_VMEM_LIMIT),
        name="ffn_in",
    )(xp, xs, g1, g2, cw, cb, lng, lnb, pw, wg, wu, wd, win)


def _mix_constants(dmask, qdec, kdec, abias):
    T = RET_CHUNK
    row_f = lax.broadcasted_iota(jnp.int32, (T, T), 0).astype(F32)
    col_f = lax.broadcasted_iota(jnp.int32, (T, T), 1).astype(F32)
    low = lax.broadcasted_iota(jnp.int32, (T, T), 1) < 64
    for h in range(RET_HEADS):
        lg = LOG_DECAY[h]
        diff = row_f - col_f
        dmask[h] = jnp.where(diff >= 0, jnp.exp(lg * jnp.maximum(diff, 0.0)), 0.0)
        qdec[h] = jnp.exp(lg * (row_f + 1.0))
    for p in range(RET_HEADS // 2):
        lgp = jnp.where(low, LOG_DECAY[2 * p], LOG_DECAY[2 * p + 1])
        kdec[p] = jnp.exp(lgp * (T - 1.0 - row_f))
    qpos = lax.broadcasted_iota(jnp.int32, (T, 2 * T), 0) + T
    kpos = lax.broadcasted_iota(jnp.int32, (T, 2 * T), 1)
    dist = (qpos - kpos).astype(F32)
    for h in range(ATT_HEADS):
        abias[h] = jnp.where((dist >= 0) & (dist < WINDOW), -SLOPES[h] * dist, NEG_INF)


def _mix_tile_steps(l, first, z_ref, mix_ref, sinks_ref, rng_ref, qg_ref, kg_ref,
                    s_scr, kbuf, vbuf, qbuf, klast, dmask, qdec, kdec, abias):
    T = RET_CHUNK
    NCH = TOKEN_TILE // T
    row_i = lax.broadcasted_iota(jnp.int32, (T, T), 0)
    low = lax.broadcasted_iota(jnp.int32, (T, T), 1) < 64

    def zc(r0, nrows, a, w):
        return z_ref[r0:r0 + nrows, a:a + w]

    def seg_mean_sq(x):
        low_x = lax.broadcasted_iota(jnp.int32, x.shape, 1) < 64
        xx = x * x
        s_lo = jnp.sum(jnp.where(low_x, xx, 0.0), axis=-1, keepdims=True)
        s_hi = jnp.sum(jnp.where(low_x, 0.0, xx), axis=-1, keepdims=True)
        return jnp.where(low_x, s_lo, s_hi) * (1.0 / HEAD_DIM)

    def retention_unit(p, j):
        s_cur = s_scr[p]
        r0 = T * j
        qp = zc(r0, T, C_RQ + 128 * p, 128)
        kp = zc(r0, T, C_RK + 128 * p, 128) * (RET_DK ** -0.5)
        vb = zc(r0, T, C_RV + 256 * p, 256).astype(BF16)
        kb = kp.astype(BF16)
        sb = s_cur.astype(BF16)
        qm = jnp.concatenate([jnp.where(low, qp, 0.0), jnp.where(low, 0.0, qp)], axis=0).astype(BF16)
        a = _dot_nt(qm, kb)
        qs = _dot(qm, sb)
        upd = _dot_tn((kp * kdec[p]).astype(BF16), vb)
        yield RET_COST // 4
        ab = (a * dmask[2 * p:2 * p + 2].reshape(2 * T, T)).astype(BF16)
        yield RET_COST // 4
        o = _dot(ab, vb)
        yield RET_COST // 4
        for e in range(2):
            h = 2 * p + e
            oh = _rms(o[T * e:T * e + T, 128 * e:128 * e + 128] + qs[T * e:T * e + T, :] * qdec[h],
                      rng_ref[l, h:h + 1, :])
            mix_ref[r0:r0 + T, M_RET + 128 * h:M_RET + 128 * h + 128] = (
                oh * _silu(zc(r0, T, C_RG + 128 * h, 128))).astype(BF16)
        top = row_i < 64
        new = jnp.where(top, upd[:, 0:128], upd[:, 128:256])
        cd = jnp.where(top, math.exp(LOG_DECAY[2 * p] * T), math.exp(LOG_DECAY[2 * p + 1] * T))
        s_scr[p] = s_cur * cd + new
        yield RET_COST // 4

    def attention_prep():
        ak = zc(0, TOKEN_TILE, C_AK, 128)
        aq = [zc(0, TOKEN_TILE, C_AQ + 128 * t, 128) for t in range(2)]
        ms_k = seg_mean_sq(ak)
        ms_q = [seg_mean_sq(aq[t]) for t in range(2)]
        yield ATT_PREP_COST // 2
        kn = ak * lax.rsqrt(ms_k + EPS) * kg_ref[l:l + 1, :]
        klast[...] = kn[TOKEN_TILE - T:TOKEN_TILE, :]
        kbuf[T:T + TOKEN_TILE, :] = kn.astype(BF16)
        vbuf[T:T + TOKEN_TILE, :] = zc(0, TOKEN_TILE, C_AV, 128).astype(BF16)
        for t in range(2):
            qt = aq[t] * lax.rsqrt(ms_q[t] + EPS) * qg_ref[l:l + 1, :] * (HEAD_DIM ** -0.5)
            for j in range(NCH):
                qj = qt[T * j:T * j + T, :]
                qbuf[j, (2 * t) * T:(2 * t + 1) * T, :] = jnp.where(low, qj, 0.0).astype(BF16)
                qbuf[j, (2 * t + 1) * T:(2 * t + 2) * T, :] = jnp.where(low, 0.0, qj).astype(BF16)
        yield ATT_PREP_COST // 2

    def attention_unit(j):
        r0 = T * j
        kcat = kbuf[r0:r0 + 2 * T, :]
        vcat = vbuf[r0:r0 + 2 * T, :]
        s = _dot_nt(qbuf[j], kcat)
        yield ATT_COST // 2
        probs = []
        for t in range(2):
            for kvh in range(KV_HEADS):
                head = 2 * kvh + t
                sink = sinks_ref[l, head]
                bias = abias[head]
                if j == 0:
                    no_prev = lax.broadcasted_iota(jnp.int32, (T, 2 * T), 1) < jnp.where(first, T, 0)
                    bias = jnp.where(no_prev, NEG_INF, bias)
                blk = 2 * t + kvh
                sb = s[T * blk:T * blk + T, :] + bias
                m = jnp.maximum(jnp.max(sb, axis=-1, keepdims=True), sink)
                e = jnp.exp(sb - m)
                inv = 1.0 / (jnp.sum(e, axis=-1, keepdims=True) + jnp.exp(sink - m))
                probs.append((e * inv).astype(BF16))
            yield ATT_COST
        o = _dot(jnp.concatenate(probs, axis=0), vcat)
        for t in range(2):
            mix_ref[r0:r0 + T, M_ATT + 128 * t:M_ATT + 128 * t + 128] = jnp.where(
                low, o[T * 2 * t:T * (2 * t + 1), :], o[T * (2 * t + 1):T * (2 * t + 2), :]
            ).astype(BF16)
        yield ATT_COST // 2

    def ret_lane():
        for j in range(NCH):
            for p in range(RET_HEADS // 2):
                yield from retention_unit(p, j)

    def att_lane():
        yield from attention_prep()
        for j in range(NCH):
            yield from attention_unit(j)

    live = [att_lane(), ret_lane()]
    while live:
        for lane in list(live):
            cost = next(lane, None)
            if cost is None:
                live.remove(lane)
            else:
                yield cost


def _conv_tile_steps(l, zcg_ref, out_ref, cwb, cb_ref, lng_ref, lnb_ref, ubuf, ush, pw_b):
    ubuf[CONV_PAD:CONV_PAD + TOKEN_TILE, :] = (
        zcg_ref[:, 0:CONV_CH] * jax.nn.sigmoid(zcg_ref[:, CONV_CH:2 * CONV_CH]))
    yield CONV_PREP_COST
    for rb in range(TOKEN_TILE // CONV_ROWS):
        base = rb * CONV_ROWS
        sh = ush.at[rb % 2]
        for r in range(1, 8):
            sh[r - 1] = ubuf[base + r:base + r + CONV_SPAN, :]
        acc = None
        for k in range(CONV_K):
            off = CONV_PAD - (CONV_K - 1) + k
            a8, r = off // 8 * 8, off % 8
            src = (ubuf[base + a8:base + a8 + CONV_ROWS, :] if r == 0
                   else sh[r - 1, a8:a8 + CONV_ROWS, :])
            term = cwb[k][None] * src.reshape(CONV_ROWS // 8, 8, CONV_CH)
            acc = term if acc is None else acc + term
        y = _silu(_layer_norm(acc.reshape(CONV_ROWS, CONV_CH) + cb_ref[l:l + 1, :],
                              lng_ref[l:l + 1, :], lnb_ref[l:l + 1, :])).astype(BF16)
        yield CONV_COST - CONV_COST // 8
        out_ref[base:base + CONV_ROWS, :] = _dot(y, pw_b[...]).astype(BF16)
        yield CONV_COST // 8


RET_COST, ATT_PREP_COST, ATT_COST, CONV_PREP_COST, CONV_COST = 160, 250, 220, 200, 410
MIX_COST = ((RET_HEADS // 2) * (TOKEN_TILE // RET_CHUNK) * RET_COST + ATT_PREP_COST
            + 3 * (TOKEN_TILE // RET_CHUNK) * ATT_COST)
CONV_TILE_COST = CONV_PREP_COST + (TOKEN_TILE // CONV_ROWS) * CONV_COST


def _mix_out_ffn_kernel(l, n_tiles, tps, sinks_ref, z_ref, x1p_ref, mixc_ref, x1s_ref, mixs_ref, g_ref,
                        rng_ref, qg_ref, kg_ref, wout_hbm, wg_hbm, wu_hbm, wd_hbm,
                        x3p_ref, x3s_ref, ret_ref, kwin_ref, vwin_ref,
                        wout_b, wg_b, wu_b, wd_b, stage, sem, x2_scr, xn_scr, a_scr, mix_new, mix_old,
                        s_scr, kbuf, vbuf, qbuf, klast, dmask, qdec, kdec, abias):
    i = pl.program_id(0)
    c = lax.rem(i, tps)
    T = RET_CHUNK
    mix_args = (sinks_ref, rng_ref, qg_ref, kg_ref,
                s_scr, kbuf, vbuf, qbuf, klast, dmask, qdec, kdec, abias)
    dense_w = (g_ref, wout_b, wg_b, wu_b, wd_b)

    @pl.when(i == 0)
    def _():
        _stage_w_out(wout_hbm, l, wout_b, stage, sem)
        _stage_weight(wg_hbm, l, wg_b, stage, sem)
        _stage_weight(wu_hbm, l, wu_b, stage, sem)
        _stage_weight(wd_hbm, l, wd_b, stage, sem)
        _mix_constants(dmask, qdec, kdec, abias)

    @pl.when(c == 0)
    def _():
        s_scr[...] = jnp.zeros_like(s_scr)
        kbuf[0:T, :] = jnp.zeros((T, 128), BF16)
        vbuf[0:T, :] = jnp.zeros((T, 128), BF16)

    @pl.when(c > 0)
    def _():
        kbuf[0:T, :] = kbuf[TOKEN_TILE:TOKEN_TILE + T, :]
        vbuf[0:T, :] = vbuf[TOKEN_TILE:TOKEN_TILE + T, :]

    @pl.when(i > 0)
    def _():
        mix_old[:, 0:M_CONV] = mix_new[:, 0:M_CONV]
        mix_old[:, M_CONV:MIX_DIM] = mixc_ref[...]

    def mixers():
        return _mix_tile_steps(l, c == 0, z_ref, mix_new, *mix_args)

    def dense():
        return _out_ffn_steps(l, x1p_ref, mix_old, x3p_ref, *dense_w, x2_scr, xn_scr, a_scr)

    @pl.when(i == 0)
    def _():
        _run(mixers())

    @pl.when((i > 0) & (i < n_tiles))
    def _():
        _interleave(dense(), mixers(), _out_ffn_cost(TOKEN_TILE), MIX_COST)

    @pl.when(i == n_tiles)
    def _():
        _run(dense())
        ns = x1s_ref.shape[0]
        _run(_out_ffn_steps(l, x1s_ref, mixs_ref, x3s_ref, *dense_w,
                            x2_scr.at[0:ns], xn_scr.at[0:ns], a_scr.at[0:ns]))

    @pl.when((c == tps - 1) & (i < n_tiles))
    def _():
        ret_ref[0] = s_scr[...]
        kwin_ref[0] = klast[...].T
        vwin_ref[0] = z_ref[TOKEN_TILE - T:TOKEN_TILE, C_AV:C_AV + 128].T


def _mix_out_ffn(l, nb, z, x1p, mixc, x1s, mixs, g, sinks, rng, qg2, kg2, wout, wg, wu, wd):
    rows, ns = x1p.shape[0], x1s.shape[0]
    n_tiles = rows // TOKEN_TILE
    tps = n_tiles // nb
    T = RET_CHUNK
    cur = lambda w: pl.BlockSpec((TOKEN_TILE, w), lambda i: (jnp.minimum(i, n_tiles - 1), 0))
    prev = lambda w: pl.BlockSpec((TOKEN_TILE, w), lambda i: (jnp.maximum(i - 1, 0), 0))
    once = lambda w: pl.BlockSpec((ns, w), lambda i: (0, 0), pipeline_mode=pl.Buffered(1))
    whole = lambda a: pl.BlockSpec(a.shape, lambda i: (0,) * a.ndim)
    per_seq = lambda shape: pl.BlockSpec(
        (1,) + shape, lambda i: (jnp.minimum(i, n_tiles - 1) // tps,) + (0,) * len(shape))
    hbm = pl.BlockSpec(memory_space=pl.ANY)
    return pl.pallas_call(
        functools.partial(_mix_out_ffn_kernel, l, n_tiles, tps),
        grid=(n_tiles + 1,),
        in_specs=[pl.BlockSpec(memory_space=pltpu.SMEM), cur(IN_DIM), prev(D_MODEL), prev(CONV_CH),
                  once(D_MODEL), once(MIX_DIM), whole(g),
                  whole(rng), whole(qg2), whole(kg2), hbm, hbm, hbm, hbm],
        out_specs=[prev(D_MODEL), once(D_MODEL),
                   per_seq((2, 128, RET_DV)), per_seq((128, WINDOW)), per_seq((128, WINDOW))],
        out_shape=[jax.ShapeDtypeStruct((rows, D_MODEL), F32),
                   jax.ShapeDtypeStruct((ns, D_MODEL), F32),
                   jax.ShapeDtypeStruct((nb, 2, 128, RET_DV), F32),
                   jax.ShapeDtypeStruct((nb, 128, WINDOW), F32),
                   jax.ShapeDtypeStruct((nb, 128, WINDOW), F32)],
        scratch_shapes=[pltpu.VMEM((MIX_DIM, D_MODEL), BF16),
                        pltpu.VMEM((D_MODEL, FFN_DIM), BF16), pltpu.VMEM((D_MODEL, FFN_DIM), BF16),
                        pltpu.VMEM((FFN_DIM, D_MODEL), BF16),
                        *_stage_scratch(),
                        pltpu.VMEM((TOKEN_TILE, D_MODEL), F32),
                        pltpu.VMEM((TOKEN_TILE, D_MODEL), BF16),
                        pltpu.VMEM((TOKEN_TILE, FFN_DIM), BF16),
                        pltpu.VMEM((TOKEN_TILE, MIX_DIM), BF16),
                        pltpu.VMEM((TOKEN_TILE, MIX_DIM), BF16),
                        pltpu.VMEM((2, 128, RET_DV), F32),
                        pltpu.VMEM((T + TOKEN_TILE, 128), BF16),
                        pltpu.VMEM((T + TOKEN_TILE, 128), BF16),
                        pltpu.VMEM((TOKEN_TILE // T, 2 * KV_HEADS * T, 128), BF16),
                        pltpu.VMEM((T, 128), F32),
                        pltpu.VMEM((RET_HEADS, T, T), F32), pltpu.VMEM((RET_HEADS, T, T), F32),
                        pltpu.VMEM((2, T, T), F32), pltpu.VMEM((ATT_HEADS, T, 2 * T), F32)],
        compiler_params=pltpu.CompilerParams(dimension_semantics=("arbitrary",),
                                             vmem_limit_bytes=V7X_VMEM_LIMIT),
        name="mix_out_ffn",
    )(sinks, z, x1p, mixc, x1s, mixs, g, rng, qg2, kg2, wout, wg, wu, wd)


def _mix_sample_kernel(l, sinks_ref, z_ref, col_ref, s_ref, ck_ref, cv_ref, sc_ref,
                       rng_ref, qg_ref, kg_ref, kgc_ref, cw_ref, cb_ref, lng_ref, lnb_ref, pw_ref,
                       *rest):
    mix_ref, so_ref, cko_ref, cvo_ref, sco_ref, o_scr, oa_scr = rest[-7:]
    NB = SAMPLE_BLOCK
    P = WINDOW

    def put(ref, idx, val):
        for slab in range(ref.shape[0]):
            ref[(slab,) + idx] = val

    for bl in range(NB):
        for h in range(RET_HEADS):
            gamma = math.exp(LOG_DECAY[h])
            r0 = RET_DK * h
            S = s_ref[0, bl, h]
            qc = col_ref[0, r0:r0 + RET_DK, bl:bl + 1]
            kc = col_ref[0, 256 + r0:256 + r0 + RET_DK, bl:bl + 1] * (RET_DK ** -0.5)
            v = z_ref[bl:bl + 1, C_RV + 128 * h:C_RV + 128 * h + 128]
            qk = jnp.sum(qc * kc, axis=0, keepdims=True)
            o_scr[bl:bl + 1, 128 * h:128 * h + 128] = (
                gamma * jnp.sum(qc * S, axis=0, keepdims=True) + qk * v)
            put(so_ref, (bl, h), gamma * S + kc * v)
    for h in range(RET_HEADS):
        o = _rms(o_scr[:, 128 * h:128 * h + 128], rng_ref[l, h:h + 1, :])
        mix_ref[:, M_RET + 128 * h:M_RET + 128 * h + 128] = (
            o * _silu(z_ref[:, C_RG + 128 * h:C_RG + 128 * h + 128]))

    q_tiles = [_seg_rms(z_ref[:, C_AQ + 128 * t:C_AQ + 128 * t + 128], qg_ref[l:l + 1, :])
               * (HEAD_DIM ** -0.5) for t in range(2)]
    kn = _seg_rms(z_ref[:, C_AK:C_AK + 128], kg_ref[l:l + 1, :])
    vn = z_ref[:, C_AV:C_AV + 128]
    akc = col_ref[0, 512:640, :]
    vnc = col_ref[0, 640:768, :]
    ms = jnp.concatenate(
        [jnp.broadcast_to(jnp.mean(akc[64 * s:64 * s + 64, :] ** 2, axis=0, keepdims=True), (64, NB))
         for s in range(KV_HEADS)], axis=0)
    knc = akc * lax.rsqrt(ms + EPS) * kgc_ref[l]
    rid = lax.broadcasted_iota(jnp.int32, (8, 128), 0)
    lane = lax.broadcasted_iota(jnp.int32, (8, 128), 1)
    sel = ((rid % 2 == 0) == (lane < 64)) & (rid < ATT_HEADS)
    rcol = lax.broadcasted_iota(jnp.int32, (8, 1), 0)
    heads = [2 * (r % 2) + r // 2 for r in range(ATT_HEADS)]
    slope_col = jnp.zeros((8, 1), F32)
    sink_col = jnp.zeros((8, 1), F32)
    for r, hd in enumerate(heads):
        slope_col = jnp.where(rcol == r, SLOPES[hd], slope_col)
        sink_col = jnp.where(rcol == r, sinks_ref[l, hd], sink_col)
    dist = float(P) - lane.astype(F32)
    key_ok = (dist < float(WINDOW)) & (lane + (PAST_LEN - P) >= 0)
    newest = lax.broadcasted_iota(jnp.int32, (128, P), 1) == P - 1
    for bl in range(NB):
        KT = ck_ref[0, bl].reshape(128, P)
        VT = cv_ref[0, bl].reshape(128, P)
        qrows = jnp.where(rid < 2, q_tiles[0][bl:bl + 1, :], q_tiles[1][bl:bl + 1, :])
        q4 = jnp.where(sel, qrows, 0.0)
        s = _dot(q4.astype(BF16), KT.astype(BF16))
        s = jnp.where(key_ok, s - slope_col * dist, NEG_INF)
        s_new = jnp.sum(q4 * kn[bl:bl + 1, :], axis=-1, keepdims=True)
        m = jnp.maximum(jnp.maximum(jnp.max(s, axis=-1, keepdims=True), s_new), sink_col)
        e = jnp.exp(s - m)
        e_new = jnp.exp(s_new - m)
        inv = 1.0 / (jnp.sum(e, axis=-1, keepdims=True) + e_new + jnp.exp(sink_col - m))
        o = _dot_nt((e * inv).astype(BF16), VT.astype(BF16)) + (e_new * inv) * vn[bl:bl + 1, :]
        for t in range(2):
            oa_scr[bl:bl + 1, 128 * t:128 * t + 128] = jnp.where(
                lane[0:1, :] < 64, o[2 * t:2 * t + 1, :], o[2 * t + 1:2 * t + 2, :])
        put(cko_ref, (bl,), jnp.where(newest, knc[:, bl:bl + 1], pltpu.roll(KT, P - 1, 1)).reshape(
            KV_HEADS, HEAD_DIM, P))
        put(cvo_ref, (bl,), jnp.where(newest, vnc[:, bl:bl + 1], pltpu.roll(VT, P - 1, 1)).reshape(
            KV_HEADS, HEAD_DIM, P))
    mix_ref[:, M_ATT:M_ATT + 256] = oa_scr[...]

    u = z_ref[:, C_CA:C_CA + CONV_CH] * jax.nn.sigmoid(z_ref[:, C_CG:C_CG + CONV_CH])
    KT1 = CONV_K - 1
    y = cw_ref[l, KT1:KT1 + 1, :] * u
    for k in range(KT1):
        y = y + cw_ref[l, k:k + 1, :] * sc_ref[0, k]
        if k > 0:
            put(sco_ref, (k - 1,), sc_ref[0, k])
    put(sco_ref, (KT1 - 1,), u)
    y = _silu(_layer_norm(y + cb_ref[l:l + 1, :], lng_ref[l:l + 1, :], lnb_ref[l:l + 1, :]))
    mix_ref[:, M_CONV:M_CONV + CONV_CH] = _dot(y.astype(BF16), pw_ref[l].astype(BF16))


def _mix_sample(l, z, cols, s, ckt, cvt, sct, sinks, rng, qg2, kg2, kgc, cw, cb, lng, lnb, pw, prev):
    ns = z.shape[0]
    NB = SAMPLE_BLOCK
    whole = lambda a: pl.BlockSpec(a.shape, lambda i: (0,) * a.ndim)

    def specs(slabs, at):
        return [pl.BlockSpec((slabs, NB, RET_HEADS, RET_DK, RET_DV), lambda i: (at, i, 0, 0, 0)),
                pl.BlockSpec((slabs, NB, KV_HEADS, HEAD_DIM, WINDOW), lambda i: (at, i, 0, 0, 0)),
                pl.BlockSpec((slabs, NB, KV_HEADS, HEAD_DIM, WINDOW), lambda i: (at, i, 0, 0, 0)),
                pl.BlockSpec((slabs, CONV_K - 1, NB, CONV_CH), lambda i: (at, 0, i, 0))]

    state_specs = specs(1, l)
    out_state_specs = specs(s.shape[0], 0) if prev is None else state_specs
    in_specs = [pl.BlockSpec(memory_space=pltpu.SMEM),
                pl.BlockSpec((NB, IN_DIM), lambda i: (i, 0)),
                pl.BlockSpec((1, 768, NB), lambda i: (i, 0, 0)),
                *state_specs,
                whole(rng), whole(qg2), whole(kg2), whole(kgc), whole(cw), whole(cb), whole(lng),
                whole(lnb), whole(pw)]
    args = [sinks, z, cols, s, ckt, cvt, sct, rng, qg2, kg2, kgc, cw, cb, lng, lnb, pw]
    aliases = {}
    if prev is not None:
        for k, a in enumerate(prev):
            aliases[len(args)] = 1 + k
            in_specs.append(pl.BlockSpec(memory_space=pl.ANY))
            args.append(a)
    return pl.pallas_call(
        functools.partial(_mix_sample_kernel, l),
        grid=(ns // NB,),
        in_specs=in_specs,
        out_specs=[pl.BlockSpec((NB, MIX_DIM), lambda i: (i, 0)), *out_state_specs],
        out_shape=[jax.ShapeDtypeStruct((ns, MIX_DIM), F32),
                   jax.ShapeDtypeStruct(s.shape, F32), jax.ShapeDtypeStruct(ckt.shape, F32),
                   jax.ShapeDtypeStruct(cvt.shape, F32), jax.ShapeDtypeStruct(sct.shape, F32)],
        scratch_shapes=[pltpu.VMEM((NB, 512), F32), pltpu.VMEM((NB, 256), F32)],
        input_output_aliases=aliases,
        compiler_params=pltpu.CompilerParams(dimension_semantics=("arbitrary",)),
        name="mix_sample",
    )(*args)


def kernel(x_prompt, x_sample, state_ret, cache_k_win, cache_v_win, state_conv, ffn1_norm, ffn1_wg, ffn1_wu, ffn1_wd, mix_norm, w_in, ret_norm_g, q_norm_g, k_norm_g, sinks, conv_w, conv_b, conv_ln_g, conv_ln_b, conv_pw, w_out, ffn2_norm, ffn2_wg, ffn2_wu, ffn2_wd):
    nb, seq, _ = x_prompt.shape
    ns = x_sample.shape[0]
    assert x_sample.shape[1] == 1 and seq % TOKEN_TILE == 0
    assert ns % SAMPLE_BLOCK == 0 and ns <= TOKEN_TILE and cache_k_win.shape[2] == WINDOW
    hp = x_prompt.reshape(nb * seq, D_MODEL)
    hs = x_sample.reshape(ns, D_MODEL)
    ckt = cache_k_win.transpose(0, 1, 3, 4, 2)
    cvt = cache_v_win.transpose(0, 1, 3, 4, 2)
    sct = state_conv.transpose(0, 2, 1, 3)
    qg2 = jnp.tile(q_norm_g, (1, 2))
    kg2 = jnp.tile(k_norm_g, (1, 2))
    kgc = kg2.reshape(DEPTH, 128, 1)
    conv_params = (conv_w, conv_b, conv_ln_g, conv_ln_b, conv_pw)
    prompt_states = [[] for _ in range(4)]
    sample_states = None
    for l in range(DEPTH):
        hp1, zp, mixc, c1, hs1, zs = _ffn_in(l, nb, hp, hs, ffn1_norm, mix_norm, *conv_params,
                                             ffn1_wg, ffn1_wu, ffn1_wd, w_in)
        cols = jnp.concatenate([zs[:, C_RQ:C_RQ + 512], zs[:, C_AK:C_AK + 256]], axis=1)
        cols = cols.T.reshape(768, ns // SAMPLE_BLOCK, SAMPLE_BLOCK).transpose(1, 0, 2)
        mixs, *sample_states = _mix_sample(l, zs, cols, state_ret, ckt, cvt, sct, sinks,
                                           ret_norm_g, qg2, kg2, kgc, conv_w, conv_b, conv_ln_g,
                                           conv_ln_b, conv_pw, sample_states)
        hp, hs, r1, k1, v1 = _mix_out_ffn(l, nb, zp, hp1, mixc, hs1, mixs, ffn2_norm, sinks,
                                          ret_norm_g, qg2, kg2, w_out, ffn2_wg, ffn2_wu, ffn2_wd)
        for lst, val in zip(prompt_states, (r1, k1, v1, c1)):
            lst.append(val)
    ret_p, kwin_p, vwin_p, conv_p = (jnp.stack(s) for s in prompt_states)
    ret_s, kwin_s, vwin_s, conv_s = sample_states
    return (hp.reshape(nb, seq, D_MODEL), hs.reshape(ns, 1, D_MODEL),
            ret_p.reshape(DEPTH, nb, RET_HEADS, RET_DK, RET_DV), ret_s,
            kwin_p.reshape(DEPTH, nb, KV_HEADS, HEAD_DIM, WINDOW).transpose(0, 1, 4, 2, 3),
            kwin_s.transpose(0, 1, 4, 2, 3),
            vwin_p.reshape(DEPTH, nb, KV_HEADS, HEAD_DIM, WINDOW).transpose(0, 1, 4, 2, 3),
            vwin_s.transpose(0, 1, 4, 2, 3),
            conv_p, conv_s.transpose(0, 2, 1, 3))
```

```python
import functools
import math

import jax
import jax.numpy as jnp
from jax import lax
from jax.experimental import pallas as pl
from jax.experimental.pallas import tpu as pltpu

F32 = jnp.float32
BF16 = jnp.bfloat16

D_MODEL = 1024
DEPTH = 2
PAST_LEN = 8192
RET_HEADS = 4
RET_DK = 64
RET_DV = 128
RET_CHUNK = 128
ATT_HEADS = 4
KV_HEADS = 2
HEAD_DIM = 64
WINDOW = 128
CONV_CH = 256
CONV_K = 31
FFN_DIM = 2816
EPS = 1e-6
NEG_INF = -1e30

C_RQ, C_RK, C_RV, C_RG = 0, 256, 512, 1024
C_AQ, C_AK, C_AV, C_CA, C_CG = 1536, 1792, 1920, 2048, 2304
IN_DIM = 2560
MIX_DIM = 1024
M_RET, M_ATT, M_CONV = 0, 512, 768

LOG_DECAY = [math.log1p(-2.0 ** (-5 - h)) for h in range(RET_HEADS)]
SLOPES = [2.0 ** (-8.0 * (h + 1) / ATT_HEADS) for h in range(ATT_HEADS)]

V7X_VMEM_LIMIT = 60 * 1024 * 1024
TOKEN_TILE = 512
FFN_COLS = 256
OUT_COLS = 512
STAGE_ROWS = 128
SAMPLE_BLOCK = 16
CONV_ROWS = 32
CONV_PAD = 32
CONV_SPAN = CONV_ROWS + CONV_PAD - 8


def _dot(a, b):
    return jnp.dot(a, b, preferred_element_type=F32)


def _dot_nt(a, b):
    return lax.dot_general(a, b, (((1,), (1,)), ((), ())), preferred_element_type=F32)


def _dot_tn(a, b):
    return lax.dot_general(a, b, (((0,), (0,)), ((), ())), preferred_element_type=F32)


def _silu(x):
    return x * jax.nn.sigmoid(x)


def _rms(x, g):
    return x * lax.rsqrt(jnp.mean(x * x, axis=-1, keepdims=True) + EPS) * g


def _seg_rms(x, g):
    r = lax.broadcasted_iota(jnp.int32, (128, 128), 0)
    c = lax.broadcasted_iota(jnp.int32, (128, 128), 1)
    seg = jnp.where((r < 64) == (c < 64), 1.0 / HEAD_DIM, 0.0).astype(BF16)
    xx = x * x
    hi = xx.astype(BF16)
    lo = (xx - hi.astype(F32)).astype(BF16)
    ms = _dot(hi, seg) + _dot(lo, seg)
    return x * lax.rsqrt(ms + EPS) * g


def _layer_norm(y, g, b):
    mu = jnp.mean(y, axis=-1, keepdims=True)
    d = y - mu
    var = jnp.mean(d * d, axis=-1, keepdims=True)
    return d * lax.rsqrt(var + EPS) * g + b


def _stage_copy(src, l, j, stage, sem, slot, width):
    return pltpu.make_async_copy(src.at[l, pl.ds(j * STAGE_ROWS, STAGE_ROWS), :],
                                 stage.at[slot, :, pl.ds(0, width)], sem.at[slot])


def _swap_inner_heads(t0, t1):
    low = lax.broadcasted_iota(jnp.int32, t0.shape, 1) < 64
    return (jnp.where(low, t0, pltpu.roll(t1, 64, 1)), jnp.where(low, pltpu.roll(t0, 64, 1), t1))


def _stage_weight(src, l, dst, stage, sem, permute_q_cols=False):
    rows, width = dst.shape
    n = rows // STAGE_ROWS
    _stage_copy(src, l, 0, stage, sem, 0, width).start()

    def body(j, carry):
        slot = lax.rem(j, 2)

        @pl.when(j + 1 < n)
        def _():
            _stage_copy(src, l, j + 1, stage, sem, 1 - slot, width).start()

        _stage_copy(src, l, j, stage, sem, slot, width).wait()
        r0 = pl.multiple_of(j * STAGE_ROWS, STAGE_ROWS)
        dst[pl.ds(r0, STAGE_ROWS), :] = stage[slot, :, 0:width].astype(BF16)
        if permute_q_cols:
            t0, t1 = _swap_inner_heads(stage[slot, :, C_AQ:C_AQ + 128],
                                       stage[slot, :, C_AQ + 128:C_AQ + 256])
            dst[pl.ds(r0, STAGE_ROWS), C_AQ:C_AQ + 128] = t0.astype(BF16)
            dst[pl.ds(r0, STAGE_ROWS), C_AQ + 128:C_AQ + 256] = t1.astype(BF16)
        return carry

    lax.fori_loop(0, n, body, 0)


def _stage_w_out(src, l, dst, stage, sem):
    rows, width = dst.shape
    n = rows // STAGE_ROWS
    _stage_copy(src, l, 0, stage, sem, 0, width).start()
    for j in range(n):
        slot = j % 2
        if j + 1 < n:
            _stage_copy(src, l, j + 1, stage, sem, 1 - slot, width).start()
        _stage_copy(src, l, j, stage, sem, slot, width).wait()
        r0 = j * STAGE_ROWS
        if r0 == M_ATT:
            dst[M_ATT:M_ATT + 64, :] = stage[slot, 0:64, 0:width].astype(BF16)
            dst[M_ATT + 128:M_ATT + 192, :] = stage[slot, 64:128, 0:width].astype(BF16)
        elif r0 == M_ATT + 128:
            dst[M_ATT + 64:M_ATT + 128, :] = stage[slot, 0:64, 0:width].astype(BF16)
            dst[M_ATT + 192:M_ATT + 256, :] = stage[slot, 64:128, 0:width].astype(BF16)
        else:
            dst[r0:r0 + STAGE_ROWS, :] = stage[slot, :, 0:width].astype(BF16)


def _run(steps):
    for _ in steps:
        pass


def _interleave(major, minor, major_total, minor_total):
    major_done = minor_done = 0
    for cost in major:
        major_done += cost
        while minor_done * major_total < major_done * minor_total:
            step = next(minor, None)
            if step is None:
                break
            minor_done += step
    _run(minor)


def _matmul_cost(rows, k, n):
    return rows * k * n // (256 * 1024)


def _ffn_steps(x_ref, g, wg_ref, wu_ref, wd_ref, out_ref, xn_scr, a_scr):
    rows = x_ref.shape[0]
    xn_scr[...] = _rms(x_ref[...], g).astype(BF16)
    for c in range(FFN_DIM // FFN_COLS):
        sl = slice(c * FFN_COLS, (c + 1) * FFN_COLS)
        gate = _dot(xn_scr[...], wg_ref[:, sl])
        up = _dot(xn_scr[...], wu_ref[:, sl])
        a_scr[:, sl] = (_silu(gate) * up).astype(BF16)
        yield 2 * _matmul_cost(rows, D_MODEL, FFN_COLS)
    for c in range(D_MODEL // OUT_COLS):
        sl = slice(c * OUT_COLS, (c + 1) * OUT_COLS)
        out_ref[:, sl] = x_ref[:, sl] + 0.5 * _dot(a_scr[...], wd_ref[:, sl])
        yield _matmul_cost(rows, FFN_DIM, OUT_COLS)


def _out_ffn_steps(l, x1_ref, mix_ref, x3_ref, g_ref, wout_b, wg_b, wu_b, wd_b, x2, xn, a):
    rows = x1_ref.shape[0]
    mix = mix_ref[...].astype(BF16)
    for c in range(D_MODEL // OUT_COLS):
        sl = slice(c * OUT_COLS, (c + 1) * OUT_COLS)
        x2[:, sl] = x1_ref[:, sl] + _dot(mix, wout_b[:, sl])
        yield _matmul_cost(rows, MIX_DIM, OUT_COLS)
    yield from _ffn_steps(x2, g_ref[l:l + 1, :], wg_b, wu_b, wd_b, x3_ref, xn, a)


def _out_ffn_cost(rows):
    return _matmul_cost(rows, MIX_DIM * D_MODEL + 3 * D_MODEL * FFN_DIM, 1)


def _ffn_in_steps(l, x_ref, x1_ref, z_ref, g1_ref, g2_ref, wg_b, wu_b, wd_b, win_b, xn, a):
    rows = x_ref.shape[0]
    yield from _ffn_steps(x_ref, g1_ref[l:l + 1, :], wg_b, wu_b, wd_b, x1_ref, xn, a)
    xn[...] = _rms(x1_ref[...], g2_ref[l:l + 1, :]).astype(BF16)
    for c in range(IN_DIM // OUT_COLS):
        sl = slice(c * OUT_COLS, (c + 1) * OUT_COLS)
        z_ref[:, sl] = _dot(xn[...], win_b[:, sl])
        yield _matmul_cost(rows, D_MODEL, OUT_COLS)


def _ffn_in_cost(rows):
    return _matmul_cost(rows, 3 * D_MODEL * FFN_DIM + D_MODEL * IN_DIM, 1)


def _ffn_in_kernel(l, n_tiles, tps, xp_ref, xs_ref, g1_ref, g2_ref, cw_ref, cb_ref, lng_ref, lnb_ref,
                   pw_ref, wg_hbm, wu_hbm, wd_hbm, win_hbm,
                   x1p_ref, zp_ref, mixc_ref, conv_ref, x1s_ref, zs_ref,
                   wg_b, wu_b, wd_b, win_b, stage, sem, xn_scr, a_scr, zcg, ubuf, ush, pw_b, cwb):
    i = pl.program_id(0)
    cprev = lax.rem(i + tps - 1, tps)
    dense_w = (g1_ref, g2_ref, wg_b, wu_b, wd_b, win_b)

    @pl.when(i == 0)
    def _():
        _stage_weight(wg_hbm, l, wg_b, stage, sem)
        _stage_weight(wu_hbm, l, wu_b, stage, sem)
        _stage_weight(wd_hbm, l, wd_b, stage, sem)
        _stage_weight(win_hbm, l, win_b, stage, sem, permute_q_cols=True)
        pw_b[...] = pw_ref[l].astype(BF16)
        for k in range(CONV_K):
            cwb[k] = jnp.broadcast_to(cw_ref[l, k:k + 1, :], (8, CONV_CH))

    @pl.when((i > 0) & (cprev == 0))
    def _():
        ubuf[0:CONV_PAD, :] = jnp.zeros((CONV_PAD, CONV_CH), F32)

    @pl.when((i > 0) & (cprev > 0))
    def _():
        ubuf[0:CONV_PAD, :] = ubuf[TOKEN_TILE:TOKEN_TILE + CONV_PAD, :]

    def dense():
        return _ffn_in_steps(l, xp_ref, x1p_ref, zp_ref, *dense_w, xn_scr, a_scr)

    def conv():
        return _conv_tile_steps(l, zcg, mixc_ref, cwb, cb_ref, lng_ref, lnb_ref, ubuf, ush, pw_b)

    @pl.when(i == 0)
    def _():
        _run(dense())

    @pl.when((i > 0) & (i < n_tiles))
    def _():
        _interleave(dense(), conv(), _ffn_in_cost(TOKEN_TILE), CONV_TILE_COST)

    @pl.when(i == n_tiles)
    def _():
        _run(conv())
        ns = xs_ref.shape[0]
        _run(_ffn_in_steps(l, xs_ref, x1s_ref, zs_ref, *dense_w, xn_scr.at[0:ns], a_scr.at[0:ns]))

    @pl.when(i < n_tiles)
    def _():
        zcg[...] = zp_ref[:, C_CA:C_CA + 2 * CONV_CH]

    @pl.when((i > 0) & (cprev == tps - 1))
    def _():
        conv_ref[0] = ubuf[CONV_PAD + TOKEN_TILE - (CONV_K - 1):CONV_PAD + TOKEN_TILE, :]


def _stage_scratch():
    return [pltpu.VMEM((2, STAGE_ROWS, FFN_DIM), F32), pltpu.SemaphoreType.DMA((2,))]


def _ffn_in(l, nb, xp, xs, g1, g2, cw, cb, lng, lnb, pw, wg, wu, wd, win):
    rows, ns = xp.shape[0], xs.shape[0]
    n_tiles = rows // TOKEN_TILE
    tps = n_tiles // nb
    cur = lambda w: pl.BlockSpec((TOKEN_TILE, w), lambda i: (jnp.minimum(i, n_tiles - 1), 0))
    prev = lambda w: pl.BlockSpec((TOKEN_TILE, w), lambda i: (jnp.maximum(i - 1, 0), 0))
    once = lambda w: pl.BlockSpec((ns, w), lambda i: (0, 0), pipeline_mode=pl.Buffered(1))
    whole = lambda a: pl.BlockSpec(a.shape, lambda i: (0,) * a.ndim)
    hbm = pl.BlockSpec(memory_space=pl.ANY)
    return pl.pallas_call(
        functools.partial(_ffn_in_kernel, l, n_tiles, tps),
        grid=(n_tiles + 1,),
        in_specs=[cur(D_MODEL), once(D_MODEL), whole(g1), whole(g2), whole(cw), whole(cb),
                  whole(lng), whole(lnb), whole(pw), hbm, hbm, hbm, hbm],
        out_specs=[cur(D_MODEL), cur(IN_DIM), prev(CONV_CH),
                   pl.BlockSpec((1, CONV_K - 1, CONV_CH),
                                lambda i: (jnp.maximum(i - 1, 0) // tps, 0, 0)),
                   once(D_MODEL), once(IN_DIM)],
        out_shape=[jax.ShapeDtypeStruct((rows, D_MODEL), F32),
                   jax.ShapeDtypeStruct((rows, IN_DIM), F32),
                   jax.ShapeDtypeStruct((rows, CONV_CH), BF16),
                   jax.ShapeDtypeStruct((nb, CONV_K - 1, CONV_CH), F32),
                   jax.ShapeDtypeStruct((ns, D_MODEL), F32),
                   jax.ShapeDtypeStruct((ns, IN_DIM), F32)],
        scratch_shapes=[pltpu.VMEM((D_MODEL, FFN_DIM), BF16), pltpu.VMEM((D_MODEL, FFN_DIM), BF16),
                        pltpu.VMEM((FFN_DIM, D_MODEL), BF16), pltpu.VMEM((D_MODEL, IN_DIM), BF16),
                        *_stage_scratch(),
                        pltpu.VMEM((TOKEN_TILE, D_MODEL), BF16),
                        pltpu.VMEM((TOKEN_TILE, FFN_DIM), BF16),
                        pltpu.VMEM((TOKEN_TILE, 2 * CONV_CH), F32),
                        pltpu.VMEM((CONV_PAD + TOKEN_TILE, CONV_CH), F32),
                        pltpu.VMEM((2, 7, CONV_SPAN, CONV_CH), F32),
                        pltpu.VMEM((CONV_CH, CONV_CH), BF16),
                        pltpu.VMEM((CONV_K, 8, CONV_CH), F32)],
        compiler_params=pltpu.CompilerParams(dimension_semantics=("arbitrary",),
                                             vmem_limit_bytes=V7X_VMEM_LIMIT),
        name="ffn_in",
    )(xp, xs, g1, g2, cw, cb, lng, lnb, pw, wg, wu, wd, win)


def _mix_constants(dmask, qdec, kdec, abias):
    T = RET_CHUNK
    row_f = lax.broadcasted_iota(jnp.int32, (T, T), 0).astype(F32)
    col_f = lax.broadcasted_iota(jnp.int32, (T, T), 1).astype(F32)
    low = lax.broadcasted_iota(jnp.int32, (T, T), 1) < 64
    for h in range(RET_HEADS):
        lg = LOG_DECAY[h]
        diff = row_f - col_f
        dmask[h] = jnp.where(diff >= 0, jnp.exp(lg * jnp.maximum(diff, 0.0)), 0.0)
        qdec[h] = jnp.exp(lg * (row_f + 1.0))
    for p in range(RET_HEADS // 2):
        lgp = jnp.where(low, LOG_DECAY[2 * p], LOG_DECAY[2 * p + 1])
        kdec[p] = jnp.exp(lgp * (T - 1.0 - row_f))
    qpos = lax.broadcasted_iota(jnp.int32, (T, 2 * T), 0) + T
    kpos = lax.broadcasted_iota(jnp.int32, (T, 2 * T), 1)
    dist = (qpos - kpos).astype(F32)
    for h in range(ATT_HEADS):
        abias[h] = jnp.where((dist >= 0) & (dist < WINDOW), -SLOPES[h] * dist, NEG_INF)


def _mix_tile_steps(l, first, z_ref, mix_ref, sinks_ref, rng_ref, qg_ref, kg_ref,
                    s_scr, kbuf, vbuf, qbuf, klast, dmask, qdec, kdec, abias):
    T = RET_CHUNK
    NCH = TOKEN_TILE // T
    row_i = lax.broadcasted_iota(jnp.int32, (T, T), 0)
    low = lax.broadcasted_iota(jnp.int32, (T, T), 1) < 64

    def zc(r0, nrows, a, w):
        return z_ref[r0:r0 + nrows, a:a + w]

    def seg_mean_sq(x):
        low_x = lax.broadcasted_iota(jnp.int32, x.shape, 1) < 64
        xx = x * x
        s_lo = jnp.sum(jnp.where(low_x, xx, 0.0), axis=-1, keepdims=True)
        s_hi = jnp.sum(jnp.where(low_x, 0.0, xx), axis=-1, keepdims=True)
        return jnp.where(low_x, s_lo, s_hi) * (1.0 / HEAD_DIM)

    def retention_unit(p, j):
        s_cur = s_scr[p]
        r0 = T * j
        qp = zc(r0, T, C_RQ + 128 * p, 128)
        kp = zc(r0, T, C_RK + 128 * p, 128) * (RET_DK ** -0.5)
        vb = zc(r0, T, C_RV + 256 * p, 256).astype(BF16)
        kb = kp.astype(BF16)
        sb = s_cur.astype(BF16)
        qm = jnp.concatenate([jnp.where(low, qp, 0.0), jnp.where(low, 0.0, qp)], axis=0).astype(BF16)
        a = _dot_nt(qm, kb)
        qs = _dot(qm, sb)
        upd = _dot_tn((kp * kdec[p]).astype(BF16), vb)
        yield RET_COST // 4
        ab = (a * dmask[2 * p:2 * p + 2].reshape(2 * T, T)).astype(BF16)
        yield RET_COST // 4
        o = _dot(ab, vb)
        yield RET_COST // 4
        for e in range(2):
            h = 2 * p + e
            oh = _rms(o[T * e:T * e + T, 128 * e:128 * e + 128] + qs[T * e:T * e + T, :] * qdec[h],
                      rng_ref[l, h:h + 1, :])
            mix_ref[r0:r0 + T, M_RET + 128 * h:M_RET + 128 * h + 128] = (
                oh * _silu(zc(r0, T, C_RG + 128 * h, 128))).astype(BF16)
        top = row_i < 64
        new = jnp.where(top, upd[:, 0:128], upd[:, 128:256])
        cd = jnp.where(top, math.exp(LOG_DECAY[2 * p] * T), math.exp(LOG_DECAY[2 * p + 1] * T))
        s_scr[p] = s_cur * cd + new
        yield RET_COST // 4

    def attention_prep():
        ak = zc(0, TOKEN_TILE, C_AK, 128)
        aq = [zc(0, TOKEN_TILE, C_AQ + 128 * t, 128) for t in range(2)]
        ms_k = seg_mean_sq(ak)
        ms_q = [seg_mean_sq(aq[t]) for t in range(2)]
        yield ATT_PREP_COST // 2
        kn = ak * lax.rsqrt(ms_k + EPS) * kg_ref[l:l + 1, :]
        klast[...] = kn[TOKEN_TILE - T:TOKEN_TILE, :]
        kbuf[T:T + TOKEN_TILE, :] = kn.astype(BF16)
        vbuf[T:T + TOKEN_TILE, :] = zc(0, TOKEN_TILE, C_AV, 128).astype(BF16)
        for t in range(2):
            qt = aq[t] * lax.rsqrt(ms_q[t] + EPS) * qg_ref[l:l + 1, :] * (HEAD_DIM ** -0.5)
            for j in range(NCH):
                qj = qt[T * j:T * j + T, :]
                qbuf[j, (2 * t) * T:(2 * t + 1) * T, :] = jnp.where(low, qj, 0.0).astype(BF16)
                qbuf[j, (2 * t + 1) * T:(2 * t + 2) * T, :] = jnp.where(low, 0.0, qj).astype(BF16)
        yield ATT_PREP_COST // 2

    def attention_unit(j):
        r0 = T * j
        kcat = kbuf[r0:r0 + 2 * T, :]
        vcat = vbuf[r0:r0 + 2 * T, :]
        s = _dot_nt(qbuf[j], kcat)
        yield ATT_COST // 2
        probs = []
        for t in range(2):
            for kvh in range(KV_HEADS):
                head = 2 * kvh + t
                sink = sinks_ref[l, head]
                bias = abias[head]
                if j == 0:
                    no_prev = lax.broadcasted_iota(jnp.int32, (T, 2 * T), 1) < jnp.where(first, T, 0)
                    bias = jnp.where(no_prev, NEG_INF, bias)
                blk = 2 * t + kvh
                sb = s[T * blk:T * blk + T, :] + bias
                m = jnp.maximum(jnp.max(sb, axis=-1, keepdims=True), sink)
                e = jnp.exp(sb - m)
                inv = 1.0 / (jnp.sum(e, axis=-1, keepdims=True) + jnp.exp(sink - m))
                probs.append((e * inv).astype(BF16))
            yield ATT_COST
        o = _dot(jnp.concatenate(probs, axis=0), vcat)
        for t in range(2):
            mix_ref[r0:r0 + T, M_ATT + 128 * t:M_ATT + 128 * t + 128] = jnp.where(
                low, o[T * 2 * t:T * (2 * t + 1), :], o[T * (2 * t + 1):T * (2 * t + 2), :]
            ).astype(BF16)
        yield ATT_COST // 2

    def ret_lane():
        for j in range(NCH):
            for p in range(RET_HEADS // 2):
                yield from retention_unit(p, j)

    def att_lane():
        yield from attention_prep()
        for j in range(NCH):
            yield from attention_unit(j)

    live = [att_lane(), ret_lane()]
    while live:
        for lane in list(live):
            cost = next(lane, None)
            if cost is None:
                live.remove(lane)
            else:
                yield cost


def _conv_tile_steps(l, zcg_ref, out_ref, cwb, cb_ref, lng_ref, lnb_ref, ubuf, ush, pw_b):
    ubuf[CONV_PAD:CONV_PAD + TOKEN_TILE, :] = (
        zcg_ref[:, 0:CONV_CH] * jax.nn.sigmoid(zcg_ref[:, CONV_CH:2 * CONV_CH]))
    yield CONV_PREP_COST
    for rb in range(TOKEN_TILE // CONV_ROWS):
        base = rb * CONV_ROWS
        sh = ush.at[rb % 2]
        for r in range(1, 8):
            sh[r - 1] = ubuf[base + r:base + r + CONV_SPAN, :]
        acc = None
        for k in range(CONV_K):
            off = CONV_PAD - (CONV_K - 1) + k
            a8, r = off // 8 * 8, off % 8
            src = (ubuf[base + a8:base + a8 + CONV_ROWS, :] if r == 0
                   else sh[r - 1, a8:a8 + CONV_ROWS, :])
            term = cwb[k][None] * src.reshape(CONV_ROWS // 8, 8, CONV_CH)
            acc = term if acc is None else acc + term
        y = _silu(_layer_norm(acc.reshape(CONV_ROWS, CONV_CH) + cb_ref[l:l + 1, :],
                              lng_ref[l:l + 1, :], lnb_ref[l:l + 1, :])).astype(BF16)
        yield CONV_COST - CONV_COST // 8
        out_ref[base:base + CONV_ROWS, :] = _dot(y, pw_b[...]).astype(BF16)
        yield CONV_COST // 8


RET_COST, ATT_PREP_COST, ATT_COST, CONV_PREP_COST, CONV_COST = 160, 250, 220, 200, 410
MIX_COST = ((RET_HEADS // 2) * (TOKEN_TILE // RET_CHUNK) * RET_COST + ATT_PREP_COST
            + 3 * (TOKEN_TILE // RET_CHUNK) * ATT_COST)
CONV_TILE_COST = CONV_PREP_COST + (TOKEN_TILE // CONV_ROWS) * CONV_COST


def _mix_out_ffn_kernel(l, n_tiles, tps, sinks_ref, z_ref, x1p_ref, mixc_ref, x1s_ref, mixs_ref, g_ref,
                        rng_ref, qg_ref, kg_ref, wout_hbm, wg_hbm, wu_hbm, wd_hbm,
                        x3p_ref, x3s_ref, ret_ref, kwin_ref, vwin_ref,
                        wout_b, wg_b, wu_b, wd_b, stage, sem, x2_scr, xn_scr, a_scr, mix_new, mix_old,
                        s_scr, kbuf, vbuf, qbuf, klast, dmask, qdec, kdec, abias):
    i = pl.program_id(0)
    c = lax.rem(i, tps)
    T = RET_CHUNK
    mix_args = (sinks_ref, rng_ref, qg_ref, kg_ref,
                s_scr, kbuf, vbuf, qbuf, klast, dmask, qdec, kdec, abias)
    dense_w = (g_ref, wout_b, wg_b, wu_b, wd_b)

    @pl.when(i == 0)
    def _():
        _stage_w_out(wout_hbm, l, wout_b, stage, sem)
        _stage_weight(wg_hbm, l, wg_b, stage, sem)
        _stage_weight(wu_hbm, l, wu_b, stage, sem)
        _stage_weight(wd_hbm, l, wd_b, stage, sem)
        _mix_constants(dmask, qdec, kdec, abias)

    @pl.when(c == 0)
    def _():
        s_scr[...] = jnp.zeros_like(s_scr)
        kbuf[0:T, :] = jnp.zeros((T, 128), BF16)
        vbuf[0:T, :] = jnp.zeros((T, 128), BF16)

    @pl.when(c > 0)
    def _():
        kbuf[0:T, :] = kbuf[TOKEN_TILE:TOKEN_TILE + T, :]
        vbuf[0:T, :] = vbuf[TOKEN_TILE:TOKEN_TILE + T, :]

    @pl.when(i > 0)
    def _():
        mix_old[:, 0:M_CONV] = mix_new[:, 0:M_CONV]
        mix_old[:, M_CONV:MIX_DIM] = mixc_ref[...]

    def mixers():
        return _mix_tile_steps(l, c == 0, z_ref, mix_new, *mix_args)

    def dense():
        return _out_ffn_steps(l, x1p_ref, mix_old, x3p_ref, *dense_w, x2_scr, xn_scr, a_scr)

    @pl.when(i == 0)
    def _():
        _run(mixers())

    @pl.when((i > 0) & (i < n_tiles))
    def _():
        _interleave(dense(), mixers(), _out_ffn_cost(TOKEN_TILE), MIX_COST)

    @pl.when(i == n_tiles)
    def _():
        _run(dense())
        ns = x1s_ref.shape[0]
        _run(_out_ffn_steps(l, x1s_ref, mixs_ref, x3s_ref, *dense_w,
                            x2_scr.at[0:ns], xn_scr.at[0:ns], a_scr.at[0:ns]))

    @pl.when((c == tps - 1) & (i < n_tiles))
    def _():
        ret_ref[0] = s_scr[...]
        kwin_ref[0] = klast[...].T
        vwin_ref[0] = z_ref[TOKEN_TILE - T:TOKEN_TILE, C_AV:C_AV + 128].T


def _mix_out_ffn(l, nb, z, x1p, mixc, x1s, mixs, g, sinks, rng, qg2, kg2, wout, wg, wu, wd):
    rows, ns = x1p.shape[0], x1s.shape[0]
    n_tiles = rows // TOKEN_TILE
    tps = n_tiles // nb
    T = RET_CHUNK
    cur = lambda w: pl.BlockSpec((TOKEN_TILE, w), lambda i: (jnp.minimum(i, n_tiles - 1), 0))
    prev = lambda w: pl.BlockSpec((TOKEN_TILE, w), lambda i: (jnp.maximum(i - 1, 0), 0))
    once = lambda w: pl.BlockSpec((ns, w), lambda i: (0, 0), pipeline_mode=pl.Buffered(1))
    whole = lambda a: pl.BlockSpec(a.shape, lambda i: (0,) * a.ndim)
    per_seq = lambda shape: pl.BlockSpec(
        (1,) + shape, lambda i: (jnp.minimum(i, n_tiles - 1) // tps,) + (0,) * len(shape))
    hbm = pl.BlockSpec(memory_space=pl.ANY)
    return pl.pallas_call(
        functools.partial(_mix_out_ffn_kernel, l, n_tiles, tps),
        grid=(n_tiles + 1,),
        in_specs=[pl.BlockSpec(memory_space=pltpu.SMEM), cur(IN_DIM), prev(D_MODEL), prev(CONV_CH),
                  once(D_MODEL), once(MIX_DIM), whole(g),
                  whole(rng), whole(qg2), whole(kg2), hbm, hbm, hbm, hbm],
        out_specs=[prev(D_MODEL), once(D_MODEL),
                   per_seq((2, 128, RET_DV)), per_seq((128, WINDOW)), per_seq((128, WINDOW))],
        out_shape=[jax.ShapeDtypeStruct((rows, D_MODEL), F32),
                   jax.ShapeDtypeStruct((ns, D_MODEL), F32),
                   jax.ShapeDtypeStruct((nb, 2, 128, RET_DV), F32),
                   jax.ShapeDtypeStruct((nb, 128, WINDOW), F32),
                   jax.ShapeDtypeStruct((nb, 128, WINDOW), F32)],
        scratch_shapes=[pltpu.VMEM((MIX_DIM, D_MODEL), BF16),
                        pltpu.VMEM((D_MODEL, FFN_DIM), BF16), pltpu.VMEM((D_MODEL, FFN_DIM), BF16),
                        pltpu.VMEM((FFN_DIM, D_MODEL), BF16),
                        *_stage_scratch(),
                        pltpu.VMEM((TOKEN_TILE, D_MODEL), F32),
                        pltpu.VMEM((TOKEN_TILE, D_MODEL), BF16),
                        pltpu.VMEM((TOKEN_TILE, FFN_DIM), BF16),
                        pltpu.VMEM((TOKEN_TILE, MIX_DIM), BF16),
                        pltpu.VMEM((TOKEN_TILE, MIX_DIM), BF16),
                        pltpu.VMEM((2, 128, RET_DV), F32),
                        pltpu.VMEM((T + TOKEN_TILE, 128), BF16),
                        pltpu.VMEM((T + TOKEN_TILE, 128), BF16),
                        pltpu.VMEM((TOKEN_TILE // T, 2 * KV_HEADS * T, 128), BF16),
                        pltpu.VMEM((T, 128), F32),
                        pltpu.VMEM((RET_HEADS, T, T), F32), pltpu.VMEM((RET_HEADS, T, T), F32),
                        pltpu.VMEM((2, T, T), F32), pltpu.VMEM((ATT_HEADS, T, 2 * T), F32)],
        compiler_params=pltpu.CompilerParams(dimension_semantics=("arbitrary",),
                                             vmem_limit_bytes=V7X_VMEM_LIMIT),
        name="mix_out_ffn",
    )(sinks, z, x1p, mixc, x1s, mixs, g, rng, qg2, kg2, wout, wg, wu, wd)


def _mix_sample_kernel(l, sinks_ref, z_ref, col_ref, s_ref, ck_ref, cv_ref, sc_ref,
                       rng_ref, qg_ref, kg_ref, kgc_ref, cw_ref, cb_ref, lng_ref, lnb_ref, pw_ref,
                       *rest):
    mix_ref, so_ref, cko_ref, cvo_ref, sco_ref, o_scr, oa_scr = rest[-7:]
    NB = SAMPLE_BLOCK
    P = WINDOW

    def put(ref, idx, val):
        for slab in range(ref.shape[0]):
            ref[(slab,) + idx] = val

    for bl in range(NB):
        for h in range(RET_HEADS):
            gamma = math.exp(LOG_DECAY[h])
            r0 = RET_DK * h
            S = s_ref[0, bl, h]
            qc = col_ref[0, r0:r0 + RET_DK, bl:bl + 1]
            kc = col_ref[0, 256 + r0:256 + r0 + RET_DK, bl:bl + 1] * (RET_DK ** -0.5)
            v = z_ref[bl:bl + 1, C_RV + 128 * h:C_RV + 128 * h + 128]
            qk = jnp.sum(qc * kc, axis=0, keepdims=True)
            o_scr[bl:bl + 1, 128 * h:128 * h + 128] = (
                gamma * jnp.sum(qc * S, axis=0, keepdims=True) + qk * v)
            put(so_ref, (bl, h), gamma * S + kc * v)
    for h in range(RET_HEADS):
        o = _rms(o_scr[:, 128 * h:128 * h + 128], rng_ref[l, h:h + 1, :])
        mix_ref[:, M_RET + 128 * h:M_RET + 128 * h + 128] = (
            o * _silu(z_ref[:, C_RG + 128 * h:C_RG + 128 * h + 128]))

    q_tiles = [_seg_rms(z_ref[:, C_AQ + 128 * t:C_AQ + 128 * t + 128], qg_ref[l:l + 1, :])
               * (HEAD_DIM ** -0.5) for t in range(2)]
    kn = _seg_rms(z_ref[:, C_AK:C_AK + 128], kg_ref[l:l + 1, :])
    vn = z_ref[:, C_AV:C_AV + 128]
    akc = col_ref[0, 512:640, :]
    vnc = col_ref[0, 640:768, :]
    ms = jnp.concatenate(
        [jnp.broadcast_to(jnp.mean(akc[64 * s:64 * s + 64, :] ** 2, axis=0, keepdims=True), (64, NB))
         for s in range(KV_HEADS)], axis=0)
    knc = akc * lax.rsqrt(ms + EPS) * kgc_ref[l]
    rid = lax.broadcasted_iota(jnp.int32, (8, 128), 0)
    lane = lax.broadcasted_iota(jnp.int32, (8, 128), 1)
    sel = ((rid % 2 == 0) == (lane < 64)) & (rid < ATT_HEADS)
    rcol = lax.broadcasted_iota(jnp.int32, (8, 1), 0)
    heads = [2 * (r % 2) + r // 2 for r in range(ATT_HEADS)]
    slope_col = jnp.zeros((8, 1), F32)
    sink_col = jnp.zeros((8, 1), F32)
    for r, hd in enumerate(heads):
        slope_col = jnp.where(rcol == r, SLOPES[hd], slope_col)
        sink_col = jnp.where(rcol == r, sinks_ref[l, hd], sink_col)
    dist = float(P) - lane.astype(F32)
    key_ok = (dist < float(WINDOW)) & (lane + (PAST_LEN - P) >= 0)
    newest = lax.broadcasted_iota(jnp.int32, (128, P), 1) == P - 1
    for bl in range(NB):
        KT = ck_ref[0, bl].reshape(128, P)
        VT = cv_ref[0, bl].reshape(128, P)
        qrows = jnp.where(rid < 2, q_tiles[0][bl:bl + 1, :], q_tiles[1][bl:bl + 1, :])
        q4 = jnp.where(sel, qrows, 0.0)
        s = _dot(q4.astype(BF16), KT.astype(BF16))
        s = jnp.where(key_ok, s - slope_col * dist, NEG_INF)
        s_new = jnp.sum(q4 * kn[bl:bl + 1, :], axis=-1, keepdims=True)
        m = jnp.maximum(jnp.maximum(jnp.max(s, axis=-1, keepdims=True), s_new), sink_col)
        e = jnp.exp(s - m)
        e_new = jnp.exp(s_new - m)
        inv = 1.0 / (jnp.sum(e, axis=-1, keepdims=True) + e_new + jnp.exp(sink_col - m))
        o = _dot_nt((e * inv).astype(BF16), VT.astype(BF16)) + (e_new * inv) * vn[bl:bl + 1, :]
        for t in range(2):
            oa_scr[bl:bl + 1, 128 * t:128 * t + 128] = jnp.where(
                lane[0:1, :] < 64, o[2 * t:2 * t + 1, :], o[2 * t + 1:2 * t + 2, :])
        put(cko_ref, (bl,), jnp.where(newest, knc[:, bl:bl + 1], pltpu.roll(KT, P - 1, 1)).reshape(
            KV_HEADS, HEAD_DIM, P))
        put(cvo_ref, (bl,), jnp.where(newest, vnc[:, bl:bl + 1], pltpu.roll(VT, P - 1, 1)).reshape(
            KV_HEADS, HEAD_DIM, P))
    mix_ref[:, M_ATT:M_ATT + 256] = oa_scr[...]

    u = z_ref[:, C_CA:C_CA + CONV_CH] * jax.nn.sigmoid(z_ref[:, C_CG:C_CG + CONV_CH])
    KT1 = CONV_K - 1
    y = cw_ref[l, KT1:KT1 + 1, :] * u
    for k in range(KT1):
        y = y + cw_ref[l, k:k + 1, :] * sc_ref[0, k]
        if k > 0:
            put(sco_ref, (k - 1,), sc_ref[0, k])
    put(sco_ref, (KT1 - 1,), u)
    y = _silu(_layer_norm(y + cb_ref[l:l + 1, :], lng_ref[l:l + 1, :], lnb_ref[l:l + 1, :]))
    mix_ref[:, M_CONV:M_CONV + CONV_CH] = _dot(y.astype(BF16), pw_ref[l].astype(BF16))


def _mix_sample(l, z, cols, s, ckt, cvt, sct, sinks, rng, qg2, kg2, kgc, cw, cb, lng, lnb, pw, prev):
    ns = z.shape[0]
    NB = SAMPLE_BLOCK
    whole = lambda a: pl.BlockSpec(a.shape, lambda i: (0,) * a.ndim)

    def specs(slabs, at):
        return [pl.BlockSpec((slabs, NB, RET_HEADS, RET_DK, RET_DV), lambda i: (at, i, 0, 0, 0)),
                pl.BlockSpec((slabs, NB, KV_HEADS, HEAD_DIM, WINDOW), lambda i: (at, i, 0, 0, 0)),
                pl.BlockSpec((slabs, NB, KV_HEADS, HEAD_DIM, WINDOW), lambda i: (at, i, 0, 0, 0)),
                pl.BlockSpec((slabs, CONV_K - 1, NB, CONV_CH), lambda i: (at, 0, i, 0))]

    state_specs = specs(1, l)
    out_state_specs = specs(s.shape[0], 0) if prev is None else state_specs
    in_specs = [pl.BlockSpec(memory_space=pltpu.SMEM),
                pl.BlockSpec((NB, IN_DIM), lambda i: (i, 0)),
                pl.BlockSpec((1, 768, NB), lambda i: (i, 0, 0)),
                *state_specs,
                whole(rng), whole(qg2), whole(kg2), whole(kgc), whole(cw), whole(cb), whole(lng),
                whole(lnb), whole(pw)]
    args = [sinks, z, cols, s, ckt, cvt, sct, rng, qg2, kg2, kgc, cw, cb, lng, lnb, pw]
    aliases = {}
    if prev is not None:
        for k, a in enumerate(prev):
            aliases[len(args)] = 1 + k
            in_specs.append(pl.BlockSpec(memory_space=pl.ANY))
            args.append(a)
    return pl.pallas_call(
        functools.partial(_mix_sample_kernel, l),
        grid=(ns // NB,),
        in_specs=in_specs,
        out_specs=[pl.BlockSpec((NB, MIX_DIM), lambda i: (i, 0)), *out_state_specs],
        out_shape=[jax.ShapeDtypeStruct((ns, MIX_DIM), F32),
                   jax.ShapeDtypeStruct(s.shape, F32), jax.ShapeDtypeStruct(ckt.shape, F32),
                   jax.ShapeDtypeStruct(cvt.shape, F32), jax.ShapeDtypeStruct(sct.shape, F32)],
        scratch_shapes=[pltpu.VMEM((NB, 512), F32), pltpu.VMEM((NB, 256), F32)],
        input_output_aliases=aliases,
        compiler_params=pltpu.CompilerParams(dimension_semantics=("arbitrary",)),
        name="mix_sample",
    )(*args)


def kernel(x_prompt, x_sample, state_ret, cache_k_win, cache_v_win, state_conv, ffn1_norm, ffn1_wg, ffn1_wu, ffn1_wd, mix_norm, w_in, ret_norm_g, q_norm_g, k_norm_g, sinks, conv_w, conv_b, conv_ln_g, conv_ln_b, conv_pw, w_out, ffn2_norm, ffn2_wg, ffn2_wu, ffn2_wd):
    nb, seq, _ = x_prompt.shape
    ns = x_sample.shape[0]
    assert x_sample.shape[1] == 1 and seq % TOKEN_TILE == 0
    assert ns % SAMPLE_BLOCK == 0 and ns <= TOKEN_TILE and cache_k_win.shape[2] == WINDOW
    hp = x_prompt.reshape(nb * seq, D_MODEL)
    hs = x_sample.reshape(ns, D_MODEL)
    ckt = cache_k_win.transpose(0, 1, 3, 4, 2)
    cvt = cache_v_win.transpose(0, 1, 3, 4, 2)
    sct = state_conv.transpose(0, 2, 1, 3)
    qg2 = jnp.tile(q_norm_g, (1, 2))
    kg2 = jnp.tile(k_norm_g, (1, 2))
    kgc = kg2.reshape(DEPTH, 128, 1)
    conv_params = (conv_w, conv_b, conv_ln_g, conv_ln_b, conv_pw)
    prompt_states = [[] for _ in range(4)]
    sample_states = None
    for l in range(DEPTH):
        hp1, zp, mixc, c1, hs1, zs = _ffn_in(l, nb, hp, hs, ffn1_norm, mix_norm, *conv_params,
                                             ffn1_wg, ffn1_wu, ffn1_wd, w_in)
        cols = jnp.concatenate([zs[:, C_RQ:C_RQ + 512], zs[:, C_AK:C_AK + 256]], axis=1)
        cols = cols.T.reshape(768, ns // SAMPLE_BLOCK, SAMPLE_BLOCK).transpose(1, 0, 2)
        mixs, *sample_states = _mix_sample(l, zs, cols, state_ret, ckt, cvt, sct, sinks,
                                           ret_norm_g, qg2, kg2, kgc, conv_w, conv_b, conv_ln_g,
                                           conv_ln_b, conv_pw, sample_states)
        hp, hs, r1, k1, v1 = _mix_out_ffn(l, nb, zp, hp1, mixc, hs1, mixs, ffn2_norm, sinks,
                                          ret_norm_g, qg2, kg2, w_out, ffn2_wg, ffn2_wu, ffn2_wd)
        for lst, val in zip(prompt_states, (r1, k1, v1, c1)):
            lst.append(val)
    ret_p, kwin_p, vwin_p, conv_p = (jnp.stack(s) for s in prompt_states)
    ret_s, kwin_s, vwin_s, conv_s = sample_states
    return (hp.reshape(nb, seq, D_MODEL), hs.reshape(ns, 1, D_MODEL),
            ret_p.reshape(DEPTH, nb, RET_HEADS, RET_DK, RET_DV), ret_s,
            kwin_p.reshape(DEPTH, nb, KV_HEADS, HEAD_DIM, WINDOW).transpose(0, 1, 4, 2, 3),
            kwin_s.transpose(0, 1, 4, 2, 3),
            vwin_p.reshape(DEPTH, nb, KV_HEADS, HEAD_DIM, WINDOW).transpose(0, 1, 4, 2, 3),
            vwin_s.transpose(0, 1, 4, 2, 3),
            conv_p, conv_s.transpose(0, 2, 1, 3))
```

```python
import functools
import math

import jax
import jax.numpy as jnp
from jax import lax
from jax.experimental import pallas as pl
from jax.experimental.pallas import tpu as pltpu

F32 = jnp.float32
BF16 = jnp.bfloat16

D_MODEL = 1024
DEPTH = 2
PAST_LEN = 8192
RET_HEADS = 4
RET_DK = 64
RET_DV = 128
RET_CHUNK = 128
ATT_HEADS = 4
KV_HEADS = 2
HEAD_DIM = 64
WINDOW = 128
CONV_CH = 256
CONV_K = 31
FFN_DIM = 2816
EPS = 1e-6
NEG_INF = -1e30

C_RQ, C_RK, C_RV, C_RG = 0, 256, 512, 1024
C_AQ, C_AK, C_AV, C_CA, C_CG = 1536, 1792, 1920, 2048, 2304
IN_DIM = 2560
MIX_DIM = 1024
M_RET, M_ATT, M_CONV = 0, 512, 768

LOG_DECAY = [math.log1p(-2.0 ** (-5 - h)) for h in range(RET_HEADS)]
SLOPES = [2.0 ** (-8.0 * (h + 1) / ATT_HEADS) for h in range(ATT_HEADS)]

V7X_VMEM_LIMIT = 60 * 1024 * 1024
TOKEN_TILE = 512
FFN_COLS = 256
OUT_COLS = 512
STAGE_ROWS = 128
SAMPLE_BLOCK = 16
CONV_ROWS = 32
CONV_PAD = 32
CONV_SPAN = CONV_ROWS + CONV_PAD - 8


def _dot(a, b):
    return jnp.dot(a, b, preferred_element_type=F32)


def _dot_nt(a, b):
    return lax.dot_general(a, b, (((1,), (1,)), ((), ())), preferred_element_type=F32)


def _dot_tn(a, b):
    return lax.dot_general(a, b, (((0,), (0,)), ((), ())), preferred_element_type=F32)


def _silu(x):
    return x * jax.nn.sigmoid(x)


def _rms(x, g):
    return x * lax.rsqrt(jnp.mean(x * x, axis=-1, keepdims=True) + EPS) * g


def _seg_rms(x, g):
    r = lax.broadcasted_iota(jnp.int32, (128, 128), 0)
    c = lax.broadcasted_iota(jnp.int32, (128, 128), 1)
    seg = jnp.where((r < 64) == (c < 64), 1.0 / HEAD_DIM, 0.0).astype(BF16)
    xx = x * x
    hi = xx.astype(BF16)
    lo = (xx - hi.astype(F32)).astype(BF16)
    ms = _dot(hi, seg) + _dot(lo, seg)
    return x * lax.rsqrt(ms + EPS) * g


def _layer_norm(y, g, b):
    mu = jnp.mean(y, axis=-1, keepdims=True)
    d = y - mu
    var = jnp.mean(d * d, axis=-1, keepdims=True)
    return d * lax.rsqrt(var + EPS) * g + b


def _stage_copy(src, l, j, stage, sem, slot, width):
    return pltpu.make_async_copy(src.at[l, pl.ds(j * STAGE_ROWS, STAGE_ROWS), :],
                                 stage.at[slot, :, pl.ds(0, width)], sem.at[slot])


def _swap_inner_heads(t0, t1):
    low = lax.broadcasted_iota(jnp.int32, t0.shape, 1) < 64
    return (jnp.where(low, t0, pltpu.roll(t1, 64, 1)), jnp.where(low, pltpu.roll(t0, 64, 1), t1))


def _stage_weight(src, l, dst, stage, sem, permute_q_cols=False, gate_up_half=None):
    rows = dst.shape[0]
    width = src.shape[2]
    n = rows // STAGE_ROWS
    _stage_copy(src, l, 0, stage, sem, 0, width).start()

    def body(j, carry):
        slot = lax.rem(j, 2)

        @pl.when(j + 1 < n)
        def _():
            _stage_copy(src, l, j + 1, stage, sem, 1 - slot, width).start()

        _stage_copy(src, l, j, stage, sem, slot, width).wait()
        r0 = pl.multiple_of(j * STAGE_ROWS, STAGE_ROWS)
        if gate_up_half is None:
            dst[pl.ds(r0, STAGE_ROWS), :] = stage[slot, :, 0:width].astype(BF16)
        else:
            for c in range(width // FFN_COLS):
                d0 = (2 * c + gate_up_half) * FFN_COLS
                dst[pl.ds(r0, STAGE_ROWS), d0:d0 + FFN_COLS] = (
                    stage[slot, :, c * FFN_COLS:(c + 1) * FFN_COLS].astype(BF16))
        if permute_q_cols:
            t0, t1 = _swap_inner_heads(stage[slot, :, C_AQ:C_AQ + 128],
                                       stage[slot, :, C_AQ + 128:C_AQ + 256])
            dst[pl.ds(r0, STAGE_ROWS), C_AQ:C_AQ + 128] = t0.astype(BF16)
            dst[pl.ds(r0, STAGE_ROWS), C_AQ + 128:C_AQ + 256] = t1.astype(BF16)
        return carry

    lax.fori_loop(0, n, body, 0)


def _stage_w_out(src, l, dst, stage, sem):
    rows, width = dst.shape
    n = rows // STAGE_ROWS
    _stage_copy(src, l, 0, stage, sem, 0, width).start()
    for j in range(n):
        slot = j % 2
        if j + 1 < n:
            _stage_copy(src, l, j + 1, stage, sem, 1 - slot, width).start()
        _stage_copy(src, l, j, stage, sem, slot, width).wait()
        r0 = j * STAGE_ROWS
        if r0 == M_ATT:
            dst[M_ATT:M_ATT + 64, :] = stage[slot, 0:64, 0:width].astype(BF16)
            dst[M_ATT + 128:M_ATT + 192, :] = stage[slot, 64:128, 0:width].astype(BF16)
        elif r0 == M_ATT + 128:
            dst[M_ATT + 64:M_ATT + 128, :] = stage[slot, 0:64, 0:width].astype(BF16)
            dst[M_ATT + 192:M_ATT + 256, :] = stage[slot, 64:128, 0:width].astype(BF16)
        else:
            dst[r0:r0 + STAGE_ROWS, :] = stage[slot, :, 0:width].astype(BF16)


def _run(steps):
    for _ in steps:
        pass


def _interleave(major, minor, major_total, minor_total):
    major_done = minor_done = 0
    for cost in major:
        major_done += cost
        while minor_done * major_total < major_done * minor_total:
            step = next(minor, None)
            if step is None:
                break
            minor_done += step
    _run(minor)


def _matmul_cost(rows, k, n):
    return rows * k * n // (256 * 1024)


def _ffn_steps(x_ref, g, wgu_ref, wd_ref, out_ref, xn_scr, a_scr):
    rows = x_ref.shape[0]
    xn_scr[...] = _rms(x_ref[...], g).astype(BF16)
    for c in range(FFN_DIM // FFN_COLS):
        sl = slice(c * FFN_COLS, (c + 1) * FFN_COLS)
        gu = _dot(xn_scr[...], wgu_ref[:, 2 * c * FFN_COLS:2 * (c + 1) * FFN_COLS])
        gate, up = gu[:, 0:FFN_COLS], gu[:, FFN_COLS:2 * FFN_COLS]
        a_scr[:, sl] = (_silu(gate) * up).astype(BF16)
        yield 2 * _matmul_cost(rows, D_MODEL, FFN_COLS)
    for c in range(D_MODEL // OUT_COLS):
        sl = slice(c * OUT_COLS, (c + 1) * OUT_COLS)
        out_ref[:, sl] = x_ref[:, sl] + 0.5 * _dot(a_scr[...], wd_ref[:, sl])
        yield _matmul_cost(rows, FFN_DIM, OUT_COLS)


def _out_ffn_steps(l, x1_ref, mix_ref, x3_ref, g_ref, wout_b, wgu_b, wd_b, x2, xn, a):
    rows = x1_ref.shape[0]
    mix = mix_ref[...].astype(BF16)
    for c in range(D_MODEL // OUT_COLS):
        sl = slice(c * OUT_COLS, (c + 1) * OUT_COLS)
        x2[:, sl] = x1_ref[:, sl] + _dot(mix, wout_b[:, sl])
        yield _matmul_cost(rows, MIX_DIM, OUT_COLS)
    yield from _ffn_steps(x2, g_ref[l:l + 1, :], wgu_b, wd_b, x3_ref, xn, a)


def _out_ffn_cost(rows):
    return _matmul_cost(rows, MIX_DIM * D_MODEL + 3 * D_MODEL * FFN_DIM, 1)


def _ffn_in_steps(l, x_ref, x1_ref, z_ref, g1_ref, g2_ref, wgu_b, wd_b, win_b, xn, a):
    rows = x_ref.shape[0]
    yield from _ffn_steps(x_ref, g1_ref[l:l + 1, :], wgu_b, wd_b, x1_ref, xn, a)
    xn[...] = _rms(x1_ref[...], g2_ref[l:l + 1, :]).astype(BF16)
    for c in range(IN_DIM // OUT_COLS):
        sl = slice(c * OUT_COLS, (c + 1) * OUT_COLS)
        z_ref[:, sl] = _dot(xn[...], win_b[:, sl])
        yield _matmul_cost(rows, D_MODEL, OUT_COLS)


def _ffn_in_cost(rows):
    return _matmul_cost(rows, 3 * D_MODEL * FFN_DIM + D_MODEL * IN_DIM, 1)


def _ffn_in_kernel(l, n_tiles, tps, xp_ref, xs_ref, g1_ref, g2_ref, cw_ref, cb_ref, lng_ref, lnb_ref,
                   pw_ref, wg_hbm, wu_hbm, wd_hbm, win_hbm,
                   x1p_ref, zp_ref, mixc_ref, conv_ref, x1s_ref, zs_ref,
                   wgu_b, wd_b, win_b, stage, sem, xn_scr, a_scr, zcg, ubuf, ush, pw_b, cwb):
    i = pl.program_id(0)
    cprev = lax.rem(i + tps - 1, tps)
    dense_w = (g1_ref, g2_ref, wgu_b, wd_b, win_b)

    @pl.when(i == 0)
    def _():
        _stage_weight(wg_hbm, l, wgu_b, stage, sem, gate_up_half=0)
        _stage_weight(wu_hbm, l, wgu_b, stage, sem, gate_up_half=1)
        _stage_weight(wd_hbm, l, wd_b, stage, sem)
        _stage_weight(win_hbm, l, win_b, stage, sem, permute_q_cols=True)
        pw_b[...] = pw_ref[l].astype(BF16)
        for k in range(CONV_K):
            cwb[k] = jnp.broadcast_to(cw_ref[l, k:k + 1, :], (8, CONV_CH))

    @pl.when((i > 0) & (cprev == 0))
    def _():
        ubuf[0:CONV_PAD, :] = jnp.zeros((CONV_PAD, CONV_CH), F32)

    @pl.when((i > 0) & (cprev > 0))
    def _():
        ubuf[0:CONV_PAD, :] = ubuf[TOKEN_TILE:TOKEN_TILE + CONV_PAD, :]

    def dense():
        return _ffn_in_steps(l, xp_ref, x1p_ref, zp_ref, *dense_w, xn_scr, a_scr)

    def conv():
        return _conv_tile_steps(l, zcg, mixc_ref, cwb, cb_ref, lng_ref, lnb_ref, ubuf, ush, pw_b)

    @pl.when(i == 0)
    def _():
        _run(dense())

    @pl.when((i > 0) & (i < n_tiles))
    def _():
        _interleave(dense(), conv(), _ffn_in_cost(TOKEN_TILE), CONV_TILE_COST)

    @pl.when(i == n_tiles)
    def _():
        _run(conv())
        ns = xs_ref.shape[0]
        _run(_ffn_in_steps(l, xs_ref, x1s_ref, zs_ref, *dense_w, xn_scr.at[0:ns], a_scr.at[0:ns]))

    @pl.when(i < n_tiles)
    def _():
        zcg[...] = zp_ref[:, C_CA:C_CA + 2 * CONV_CH]

    @pl.when((i > 0) & (cprev == tps - 1))
    def _():
        conv_ref[0] = ubuf[CONV_PAD + TOKEN_TILE - (CONV_K - 1):CONV_PAD + TOKEN_TILE, :]


def _stage_scratch():
    return [pltpu.VMEM((2, STAGE_ROWS, FFN_DIM), F32), pltpu.SemaphoreType.DMA((2,))]


def _ffn_in(l, nb, xp, xs, g1, g2, cw, cb, lng, lnb, pw, wg, wu, wd, win):
    rows, ns = xp.shape[0], xs.shape[0]
    n_tiles = rows // TOKEN_TILE
    tps = n_tiles // nb
    cur = lambda w: pl.BlockSpec((TOKEN_TILE, w), lambda i: (jnp.minimum(i, n_tiles - 1), 0))
    prev = lambda w: pl.BlockSpec((TOKEN_TILE, w), lambda i: (jnp.maximum(i - 1, 0), 0))
    once = lambda w: pl.BlockSpec((ns, w), lambda i: (0, 0), pipeline_mode=pl.Buffered(1))
    whole = lambda a: pl.BlockSpec(a.shape, lambda i: (0,) * a.ndim)
    hbm = pl.BlockSpec(memory_space=pl.ANY)
    return pl.pallas_call(
        functools.partial(_ffn_in_kernel, l, n_tiles, tps),
        grid=(n_tiles + 1,),
        in_specs=[cur(D_MODEL), once(D_MODEL), whole(g1), whole(g2), whole(cw), whole(cb),
                  whole(lng), whole(lnb), whole(pw), hbm, hbm, hbm, hbm],
        out_specs=[cur(D_MODEL), cur(IN_DIM), prev(CONV_CH),
                   pl.BlockSpec((1, CONV_K - 1, CONV_CH),
                                lambda i: (jnp.maximum(i - 1, 0) // tps, 0, 0)),
                   once(D_MODEL), once(IN_DIM)],
        out_shape=[jax.ShapeDtypeStruct((rows, D_MODEL), F32),
                   jax.ShapeDtypeStruct((rows, IN_DIM), F32),
                   jax.ShapeDtypeStruct((rows, CONV_CH), BF16),
                   jax.ShapeDtypeStruct((nb, CONV_K - 1, CONV_CH), F32),
                   jax.ShapeDtypeStruct((ns, D_MODEL), F32),
                   jax.ShapeDtypeStruct((ns, IN_DIM), F32)],
        scratch_shapes=[pltpu.VMEM((D_MODEL, 2 * FFN_DIM), BF16),
                        pltpu.VMEM((FFN_DIM, D_MODEL), BF16), pltpu.VMEM((D_MODEL, IN_DIM), BF16),
                        *_stage_scratch(),
                        pltpu.VMEM((TOKEN_TILE, D_MODEL), BF16),
                        pltpu.VMEM((TOKEN_TILE, FFN_DIM), BF16),
                        pltpu.VMEM((TOKEN_TILE, 2 * CONV_CH), F32),
                        pltpu.VMEM((CONV_PAD + TOKEN_TILE, CONV_CH), F32),
                        pltpu.VMEM((2, 7, CONV_SPAN, CONV_CH), F32),
                        pltpu.VMEM((CONV_CH, CONV_CH), BF16),
                        pltpu.VMEM((CONV_K, 8, CONV_CH), F32)],
        compiler_params=pltpu.CompilerParams(dimension_semantics=("arbitrary",),
                                             vmem_limit_bytes=V7X_VMEM_LIMIT),
        name="ffn_in",
    )(xp, xs, g1, g2, cw, cb, lng, lnb, pw, wg, wu, wd, win)


def _mix_constants(dmask, qdec, kdec, abias):
    T = RET_CHUNK
    row_f = lax.broadcasted_iota(jnp.int32, (T, T), 0).astype(F32)
    col_f = lax.broadcasted_iota(jnp.int32, (T, T), 1).astype(F32)
    low = lax.broadcasted_iota(jnp.int32, (T, T), 1) < 64
    for h in range(RET_HEADS):
        lg = LOG_DECAY[h]
        diff = row_f - col_f
        dmask[h] = jnp.where(diff >= 0, jnp.exp(lg * jnp.maximum(diff, 0.0)), 0.0)
        qdec[h] = jnp.exp(lg * (row_f + 1.0))
    for p in range(RET_HEADS // 2):
        lgp = jnp.where(low, LOG_DECAY[2 * p], LOG_DECAY[2 * p + 1])
        kdec[p] = jnp.exp(lgp * (T - 1.0 - row_f))
    qpos = lax.broadcasted_iota(jnp.int32, (T, 2 * T), 0) + T
    kpos = lax.broadcasted_iota(jnp.int32, (T, 2 * T), 1)
    dist = (qpos - kpos).astype(F32)
    for h in range(ATT_HEADS):
        abias[h] = jnp.where((dist >= 0) & (dist < WINDOW), -SLOPES[h] * dist, NEG_INF)


def _mix_tile_steps(l, first, z_ref, mix_ref, sinks_ref, rng_ref, qg_ref, kg_ref,
                    s_scr, kbuf, vbuf, qbuf, klast, dmask, qdec, kdec, abias):
    T = RET_CHUNK
    NCH = TOKEN_TILE // T
    row_i = lax.broadcasted_iota(jnp.int32, (T, T), 0)
    low = lax.broadcasted_iota(jnp.int32, (T, T), 1) < 64

    def zc(r0, nrows, a, w):
        return z_ref[r0:r0 + nrows, a:a + w]

    def seg_mean_sq(x):
        low_x = lax.broadcasted_iota(jnp.int32, x.shape, 1) < 64
        xx = x * x
        s_lo = jnp.sum(jnp.where(low_x, xx, 0.0), axis=-1, keepdims=True)
        s_hi = jnp.sum(jnp.where(low_x, 0.0, xx), axis=-1, keepdims=True)
        return jnp.where(low_x, s_lo, s_hi) * (1.0 / HEAD_DIM)

    def retention_unit(p, j):
        s_cur = s_scr[p]
        r0 = T * j
        qp = zc(r0, T, C_RQ + 128 * p, 128)
        kp = zc(r0, T, C_RK + 128 * p, 128) * (RET_DK ** -0.5)
        vb = zc(r0, T, C_RV + 256 * p, 256).astype(BF16)
        kb = kp.astype(BF16)
        sb = s_cur.astype(BF16)
        qm = jnp.concatenate([jnp.where(low, qp, 0.0), jnp.where(low, 0.0, qp)], axis=0).astype(BF16)
        a = _dot_nt(qm, kb)
        qs = _dot(qm, sb)
        upd = _dot_tn((kp * kdec[p]).astype(BF16), vb)
        yield RET_COST // 4
        ab = (a * dmask[2 * p:2 * p + 2].reshape(2 * T, T)).astype(BF16)
        yield RET_COST // 4
        o = _dot(ab, vb)
        yield RET_COST // 4
        for e in range(2):
            h = 2 * p + e
            oh = _rms(o[T * e:T * e + T, 128 * e:128 * e + 128] + qs[T * e:T * e + T, :] * qdec[h],
                      rng_ref[l, h:h + 1, :])
            mix_ref[r0:r0 + T, M_RET + 128 * h:M_RET + 128 * h + 128] = (
                oh * _silu(zc(r0, T, C_RG + 128 * h, 128))).astype(BF16)
        top = row_i < 64
        new = jnp.where(top, upd[:, 0:128], upd[:, 128:256])
        cd = jnp.where(top, math.exp(LOG_DECAY[2 * p] * T), math.exp(LOG_DECAY[2 * p + 1] * T))
        s_scr[p] = s_cur * cd + new
        yield RET_COST // 4

    def attention_prep():
        ak = zc(0, TOKEN_TILE, C_AK, 128)
        aq = [zc(0, TOKEN_TILE, C_AQ + 128 * t, 128) for t in range(2)]
        ms_k = seg_mean_sq(ak)
        ms_q = [seg_mean_sq(aq[t]) for t in range(2)]
        yield ATT_PREP_COST // 2
        kn = ak * lax.rsqrt(ms_k + EPS) * kg_ref[l:l + 1, :]
        klast[...] = kn[TOKEN_TILE - T:TOKEN_TILE, :]
        kbuf[T:T + TOKEN_TILE, :] = kn.astype(BF16)
        vbuf[T:T + TOKEN_TILE, :] = zc(0, TOKEN_TILE, C_AV, 128).astype(BF16)
        for t in range(2):
            qt = aq[t] * lax.rsqrt(ms_q[t] + EPS) * qg_ref[l:l + 1, :] * (HEAD_DIM ** -0.5)
            for j in range(NCH):
                qj = qt[T * j:T * j + T, :]
                qbuf[j, (2 * t) * T:(2 * t + 1) * T, :] = jnp.where(low, qj, 0.0).astype(BF16)
                qbuf[j, (2 * t + 1) * T:(2 * t + 2) * T, :] = jnp.where(low, 0.0, qj).astype(BF16)
        yield ATT_PREP_COST // 2

    def attention_unit(j):
        r0 = T * j
        kcat = kbuf[r0:r0 + 2 * T, :]
        vcat = vbuf[r0:r0 + 2 * T, :]
        s = _dot_nt(qbuf[j], kcat)
        yield ATT_COST // 2
        probs = []
        for t in range(2):
            for kvh in range(KV_HEADS):
                head = 2 * kvh + t
                sink = sinks_ref[l, head]
                bias = abias[head]
                if j == 0:
                    no_prev = lax.broadcasted_iota(jnp.int32, (T, 2 * T), 1) < jnp.where(first, T, 0)
                    bias = jnp.where(no_prev, NEG_INF, bias)
                blk = 2 * t + kvh
                sb = s[T * blk:T * blk + T, :] + bias
                m = jnp.maximum(jnp.max(sb, axis=-1, keepdims=True), sink)
                e = jnp.exp(sb - m)
                inv = 1.0 / (jnp.sum(e, axis=-1, keepdims=True) + jnp.exp(sink - m))
                probs.append((e * inv).astype(BF16))
            yield ATT_COST
        o = _dot(jnp.concatenate(probs, axis=0), vcat)
        for t in range(2):
            mix_ref[r0:r0 + T, M_ATT + 128 * t:M_ATT + 128 * t + 128] = jnp.where(
                low, o[T * 2 * t:T * (2 * t + 1), :], o[T * (2 * t + 1):T * (2 * t + 2), :]
            ).astype(BF16)
        yield ATT_COST // 2

    def ret_lane():
        for j in range(NCH):
            for p in range(RET_HEADS // 2):
                yield from retention_unit(p, j)

    def att_lane():
        yield from attention_prep()
        for j in range(NCH):
            yield from attention_unit(j)

    live = [att_lane(), ret_lane()]
    while live:
        for lane in list(live):
            cost = next(lane, None)
            if cost is None:
                live.remove(lane)
            else:
                yield cost


def _conv_tile_steps(l, zcg_ref, out_ref, cwb, cb_ref, lng_ref, lnb_ref, ubuf, ush, pw_b):
    ubuf[CONV_PAD:CONV_PAD + TOKEN_TILE, :] = (
        zcg_ref[:, 0:CONV_CH] * jax.nn.sigmoid(zcg_ref[:, CONV_CH:2 * CONV_CH]))
    yield CONV_PREP_COST
    for rb in range(TOKEN_TILE // CONV_ROWS):
        base = rb * CONV_ROWS
        sh = ush.at[rb % 2]
        for r in range(1, 8):
            sh[r - 1] = ubuf[base + r:base + r + CONV_SPAN, :]
        acc = None
        for k in range(CONV_K):
            off = CONV_PAD - (CONV_K - 1) + k
            a8, r = off // 8 * 8, off % 8
            src = (ubuf[base + a8:base + a8 + CONV_ROWS, :] if r == 0
                   else sh[r - 1, a8:a8 + CONV_ROWS, :])
            term = cwb[k][None] * src.reshape(CONV_ROWS // 8, 8, CONV_CH)
            acc = term if acc is None else acc + term
        y = _silu(_layer_norm(acc.reshape(CONV_ROWS, CONV_CH) + cb_ref[l:l + 1, :],
                              lng_ref[l:l + 1, :], lnb_ref[l:l + 1, :])).astype(BF16)
        yield CONV_COST - CONV_COST // 8
        out_ref[base:base + CONV_ROWS, :] = _dot(y, pw_b[...]).astype(BF16)
        yield CONV_COST // 8


RET_COST, ATT_PREP_COST, ATT_COST, CONV_PREP_COST, CONV_COST = 160, 250, 220, 200, 410
MIX_COST = ((RET_HEADS // 2) * (TOKEN_TILE // RET_CHUNK) * RET_COST + ATT_PREP_COST
            + 3 * (TOKEN_TILE // RET_CHUNK) * ATT_COST)
CONV_TILE_COST = CONV_PREP_COST + (TOKEN_TILE // CONV_ROWS) * CONV_COST


def _mix_out_ffn_kernel(l, n_tiles, tps, sinks_ref, z_ref, x1p_ref, mixc_ref, x1s_ref, mixs_ref, g_ref,
                        rng_ref, qg_ref, kg_ref, wout_hbm, wg_hbm, wu_hbm, wd_hbm,
                        x3p_ref, x3s_ref, ret_ref, kwin_ref, vwin_ref,
                        wout_b, wgu_b, wd_b, stage, sem, x2_scr, xn_scr, a_scr, mix_new, mix_old,
                        s_scr, kbuf, vbuf, qbuf, klast, dmask, qdec, kdec, abias):
    i = pl.program_id(0)
    c = lax.rem(i, tps)
    T = RET_CHUNK
    mix_args = (sinks_ref, rng_ref, qg_ref, kg_ref,
                s_scr, kbuf, vbuf, qbuf, klast, dmask, qdec, kdec, abias)
    dense_w = (g_ref, wout_b, wgu_b, wd_b)

    @pl.when(i == 0)
    def _():
        _stage_w_out(wout_hbm, l, wout_b, stage, sem)
        _stage_weight(wg_hbm, l, wgu_b, stage, sem, gate_up_half=0)
        _stage_weight(wu_hbm, l, wgu_b, stage, sem, gate_up_half=1)
        _stage_weight(wd_hbm, l, wd_b, stage, sem)
        _mix_constants(dmask, qdec, kdec, abias)

    @pl.when(c == 0)
    def _():
        s_scr[...] = jnp.zeros_like(s_scr)
        kbuf[0:T, :] = jnp.zeros((T, 128), BF16)
        vbuf[0:T, :] = jnp.zeros((T, 128), BF16)

    @pl.when(c > 0)
    def _():
        kbuf[0:T, :] = kbuf[TOKEN_TILE:TOKEN_TILE + T, :]
        vbuf[0:T, :] = vbuf[TOKEN_TILE:TOKEN_TILE + T, :]

    @pl.when(i > 0)
    def _():
        mix_old[:, 0:M_CONV] = mix_new[:, 0:M_CONV]
        mix_old[:, M_CONV:MIX_DIM] = mixc_ref[...]

    def mixers():
        return _mix_tile_steps(l, c == 0, z_ref, mix_new, *mix_args)

    def dense():
        return _out_ffn_steps(l, x1p_ref, mix_old, x3p_ref, *dense_w, x2_scr, xn_scr, a_scr)

    @pl.when(i == 0)
    def _():
        _run(mixers())

    @pl.when((i > 0) & (i < n_tiles))
    def _():
        _interleave(dense(), mixers(), _out_ffn_cost(TOKEN_TILE), MIX_COST)

    @pl.when(i == n_tiles)
    def _():
        _run(dense())
        ns = x1s_ref.shape[0]
        _run(_out_ffn_steps(l, x1s_ref, mixs_ref, x3s_ref, *dense_w,
                            x2_scr.at[0:ns], xn_scr.at[0:ns], a_scr.at[0:ns]))

    @pl.when((c == tps - 1) & (i < n_tiles))
    def _():
        ret_ref[0] = s_scr[...]
        kwin_ref[0] = klast[...].T
        vwin_ref[0] = z_ref[TOKEN_TILE - T:TOKEN_TILE, C_AV:C_AV + 128].T


def _mix_out_ffn(l, nb, z, x1p, mixc, x1s, mixs, g, sinks, rng, qg2, kg2, wout, wg, wu, wd):
    rows, ns = x1p.shape[0], x1s.shape[0]
    n_tiles = rows // TOKEN_TILE
    tps = n_tiles // nb
    T = RET_CHUNK
    cur = lambda w: pl.BlockSpec((TOKEN_TILE, w), lambda i: (jnp.minimum(i, n_tiles - 1), 0))
    prev = lambda w: pl.BlockSpec((TOKEN_TILE, w), lambda i: (jnp.maximum(i - 1, 0), 0))
    once = lambda w: pl.BlockSpec((ns, w), lambda i: (0, 0), pipeline_mode=pl.Buffered(1))
    whole = lambda a: pl.BlockSpec(a.shape, lambda i: (0,) * a.ndim)
    per_seq = lambda shape: pl.BlockSpec(
        (1,) + shape, lambda i: (jnp.minimum(i, n_tiles - 1) // tps,) + (0,) * len(shape))
    hbm = pl.BlockSpec(memory_space=pl.ANY)
    return pl.pallas_call(
        functools.partial(_mix_out_ffn_kernel, l, n_tiles, tps),
        grid=(n_tiles + 1,),
        in_specs=[pl.BlockSpec(memory_space=pltpu.SMEM), cur(IN_DIM), prev(D_MODEL), prev(CONV_CH),
                  once(D_MODEL), once(MIX_DIM), whole(g),
                  whole(rng), whole(qg2), whole(kg2), hbm, hbm, hbm, hbm],
        out_specs=[prev(D_MODEL), once(D_MODEL),
                   per_seq((2, 128, RET_DV)), per_seq((128, WINDOW)), per_seq((128, WINDOW))],
        out_shape=[jax.ShapeDtypeStruct((rows, D_MODEL), F32),
                   jax.ShapeDtypeStruct((ns, D_MODEL), F32),
                   jax.ShapeDtypeStruct((nb, 2, 128, RET_DV), F32),
                   jax.ShapeDtypeStruct((nb, 128, WINDOW), F32),
                   jax.ShapeDtypeStruct((nb, 128, WINDOW), F32)],
        scratch_shapes=[pltpu.VMEM((MIX_DIM, D_MODEL), BF16),
                        pltpu.VMEM((D_MODEL, 2 * FFN_DIM), BF16),
                        pltpu.VMEM((FFN_DIM, D_MODEL), BF16),
                        *_stage_scratch(),
                        pltpu.VMEM((TOKEN_TILE, D_MODEL), F32),
                        pltpu.VMEM((TOKEN_TILE, D_MODEL), BF16),
                        pltpu.VMEM((TOKEN_TILE, FFN_DIM), BF16),
                        pltpu.VMEM((TOKEN_TILE, MIX_DIM), BF16),
                        pltpu.VMEM((TOKEN_TILE, MIX_DIM), BF16),
                        pltpu.VMEM((2, 128, RET_DV), F32),
                        pltpu.VMEM((T + TOKEN_TILE, 128), BF16),
                        pltpu.VMEM((T + TOKEN_TILE, 128), BF16),
                        pltpu.VMEM((TOKEN_TILE // T, 2 * KV_HEADS * T, 128), BF16),
                        pltpu.VMEM((T, 128), F32),
                        pltpu.VMEM((RET_HEADS, T, T), F32), pltpu.VMEM((RET_HEADS, T, T), F32),
                        pltpu.VMEM((2, T, T), F32), pltpu.VMEM((ATT_HEADS, T, 2 * T), F32)],
        compiler_params=pltpu.CompilerParams(dimension_semantics=("arbitrary",),
                                             vmem_limit_bytes=V7X_VMEM_LIMIT),
        name="mix_out_ffn",
    )(sinks, z, x1p, mixc, x1s, mixs, g, rng, qg2, kg2, wout, wg, wu, wd)


def _mix_sample_kernel(l, sinks_ref, z_ref, col_ref, s_ref, ck_ref, cv_ref, sc_ref,
                       rng_ref, qg_ref, kg_ref, kgc_ref, cw_ref, cb_ref, lng_ref, lnb_ref, pw_ref,
                       *rest):
    mix_ref, so_ref, cko_ref, cvo_ref, sco_ref, o_scr, oa_scr = rest[-7:]
    NB = SAMPLE_BLOCK
    P = WINDOW

    def put(ref, idx, val):
        for slab in range(ref.shape[0]):
            ref[(slab,) + idx] = val

    for bl in range(NB):
        for h in range(RET_HEADS):
            gamma = math.exp(LOG_DECAY[h])
            r0 = RET_DK * h
            S = s_ref[0, bl, h]
            qc = col_ref[0, r0:r0 + RET_DK, bl:bl + 1]
            kc = col_ref[0, 256 + r0:256 + r0 + RET_DK, bl:bl + 1] * (RET_DK ** -0.5)
            v = z_ref[bl:bl + 1, C_RV + 128 * h:C_RV + 128 * h + 128]
            qk = jnp.sum(qc * kc, axis=0, keepdims=True)
            o_scr[bl:bl + 1, 128 * h:128 * h + 128] = (
                gamma * jnp.sum(qc * S, axis=0, keepdims=True) + qk * v)
            put(so_ref, (bl, h), gamma * S + kc * v)
    for h in range(RET_HEADS):
        o = _rms(o_scr[:, 128 * h:128 * h + 128], rng_ref[l, h:h + 1, :])
        mix_ref[:, M_RET + 128 * h:M_RET + 128 * h + 128] = (
            o * _silu(z_ref[:, C_RG + 128 * h:C_RG + 128 * h + 128]))

    q_tiles = [_seg_rms(z_ref[:, C_AQ + 128 * t:C_AQ + 128 * t + 128], qg_ref[l:l + 1, :])
               * (HEAD_DIM ** -0.5) for t in range(2)]
    kn = _seg_rms(z_ref[:, C_AK:C_AK + 128], kg_ref[l:l + 1, :])
    vn = z_ref[:, C_AV:C_AV + 128]
    akc = col_ref[0, 512:640, :]
    vnc = col_ref[0, 640:768, :]
    ms = jnp.concatenate(
        [jnp.broadcast_to(jnp.mean(akc[64 * s:64 * s + 64, :] ** 2, axis=0, keepdims=True), (64, NB))
         for s in range(KV_HEADS)], axis=0)
    knc = akc * lax.rsqrt(ms + EPS) * kgc_ref[l]
    rid = lax.broadcasted_iota(jnp.int32, (8, 128), 0)
    lane = lax.broadcasted_iota(jnp.int32, (8, 128), 1)
    sel = ((rid % 2 == 0) == (lane < 64)) & (rid < ATT_HEADS)
    rcol = lax.broadcasted_iota(jnp.int32, (8, 1), 0)
    heads = [2 * (r % 2) + r // 2 for r in range(ATT_HEADS)]
    slope_col = jnp.zeros((8, 1), F32)
    sink_col = jnp.zeros((8, 1), F32)
    for r, hd in enumerate(heads):
        slope_col = jnp.where(rcol == r, SLOPES[hd], slope_col)
        sink_col = jnp.where(rcol == r, sinks_ref[l, hd], sink_col)
    dist = float(P) - lane.astype(F32)
    key_ok = (dist < float(WINDOW)) & (lane + (PAST_LEN - P) >= 0)
    newest = lax.broadcasted_iota(jnp.int32, (128, P), 1) == P - 1
    for bl in range(NB):
        KT = ck_ref[0, bl].reshape(128, P)
        VT = cv_ref[0, bl].reshape(128, P)
        qrows = jnp.where(rid < 2, q_tiles[0][bl:bl + 1, :], q_tiles[1][bl:bl + 1, :])
        q4 = jnp.where(sel, qrows, 0.0)
        s = _dot(q4.astype(BF16), KT.astype(BF16))
        s = jnp.where(key_ok, s - slope_col * dist, NEG_INF)
        s_new = jnp.sum(q4 * kn[bl:bl + 1, :], axis=-1, keepdims=True)
        m = jnp.maximum(jnp.maximum(jnp.max(s, axis=-1, keepdims=True), s_new), sink_col)
        e = jnp.exp(s - m)
        e_new = jnp.exp(s_new - m)
        inv = 1.0 / (jnp.sum(e, axis=-1, keepdims=True) + e_new + jnp.exp(sink_col - m))
        o = _dot_nt((e * inv).astype(BF16), VT.astype(BF16)) + (e_new * inv) * vn[bl:bl + 1, :]
        for t in range(2):
            oa_scr[bl:bl + 1, 128 * t:128 * t + 128] = jnp.where(
                lane[0:1, :] < 64, o[2 * t:2 * t + 1, :], o[2 * t + 1:2 * t + 2, :])
        put(cko_ref, (bl,), jnp.where(newest, knc[:, bl:bl + 1], pltpu.roll(KT, P - 1, 1)).reshape(
            KV_HEADS, HEAD_DIM, P))
        put(cvo_ref, (bl,), jnp.where(newest, vnc[:, bl:bl + 1], pltpu.roll(VT, P - 1, 1)).reshape(
            KV_HEADS, HEAD_DIM, P))
    mix_ref[:, M_ATT:M_ATT + 256] = oa_scr[...]

    u = z_ref[:, C_CA:C_CA + CONV_CH] * jax.nn.sigmoid(z_ref[:, C_CG:C_CG + CONV_CH])
    KT1 = CONV_K - 1
    y = cw_ref[l, KT1:KT1 + 1, :] * u
    for k in range(KT1):
        y = y + cw_ref[l, k:k + 1, :] * sc_ref[0, k]
        if k > 0:
            put(sco_ref, (k - 1,), sc_ref[0, k])
    put(sco_ref, (KT1 - 1,), u)
    y = _silu(_layer_norm(y + cb_ref[l:l + 1, :], lng_ref[l:l + 1, :], lnb_ref[l:l + 1, :]))
    mix_ref[:, M_CONV:M_CONV + CONV_CH] = _dot(y.astype(BF16), pw_ref[l].astype(BF16))


def _mix_sample(l, z, cols, s, ckt, cvt, sct, sinks, rng, qg2, kg2, kgc, cw, cb, lng, lnb, pw, prev):
    ns = z.shape[0]
    NB = SAMPLE_BLOCK
    whole = lambda a: pl.BlockSpec(a.shape, lambda i: (0,) * a.ndim)

    def specs(slabs, at):
        return [pl.BlockSpec((slabs, NB, RET_HEADS, RET_DK, RET_DV), lambda i: (at, i, 0, 0, 0)),
                pl.BlockSpec((slabs, NB, KV_HEADS, HEAD_DIM, WINDOW), lambda i: (at, i, 0, 0, 0)),
                pl.BlockSpec((slabs, NB, KV_HEADS, HEAD_DIM, WINDOW), lambda i: (at, i, 0, 0, 0)),
                pl.BlockSpec((slabs, CONV_K - 1, NB, CONV_CH), lambda i: (at, 0, i, 0))]

    state_specs = specs(1, l)
    out_state_specs = specs(s.shape[0], 0) if prev is None else state_specs
    in_specs = [pl.BlockSpec(memory_space=pltpu.SMEM),
                pl.BlockSpec((NB, IN_DIM), lambda i: (i, 0)),
                pl.BlockSpec((1, 768, NB), lambda i: (i, 0, 0)),
                *state_specs,
                whole(rng), whole(qg2), whole(kg2), whole(kgc), whole(cw), whole(cb), whole(lng),
                whole(lnb), whole(pw)]
    args = [sinks, z, cols, s, ckt, cvt, sct, rng, qg2, kg2, kgc, cw, cb, lng, lnb, pw]
    aliases = {}
    if prev is not None:
        for k, a in enumerate(prev):
            aliases[len(args)] = 1 + k
            in_specs.append(pl.BlockSpec(memory_space=pl.ANY))
            args.append(a)
    return pl.pallas_call(
        functools.partial(_mix_sample_kernel, l),
        grid=(ns // NB,),
        in_specs=in_specs,
        out_specs=[pl.BlockSpec((NB, MIX_DIM), lambda i: (i, 0)), *out_state_specs],
        out_shape=[jax.ShapeDtypeStruct((ns, MIX_DIM), F32),
                   jax.ShapeDtypeStruct(s.shape, F32), jax.ShapeDtypeStruct(ckt.shape, F32),
                   jax.ShapeDtypeStruct(cvt.shape, F32), jax.ShapeDtypeStruct(sct.shape, F32)],
        scratch_shapes=[pltpu.VMEM((NB, 512), F32), pltpu.VMEM((NB, 256), F32)],
        input_output_aliases=aliases,
        compiler_params=pltpu.CompilerParams(dimension_semantics=("arbitrary",)),
        name="mix_sample",
    )(*args)


def kernel(x_prompt, x_sample, state_ret, cache_k_win, cache_v_win, state_conv, ffn1_norm, ffn1_wg, ffn1_wu, ffn1_wd, mix_norm, w_in, ret_norm_g, q_norm_g, k_norm_g, sinks, conv_w, conv_b, conv_ln_g, conv_ln_b, conv_pw, w_out, ffn2_norm, ffn2_wg, ffn2_wu, ffn2_wd):
    nb, seq, _ = x_prompt.shape
    ns = x_sample.shape[0]
    assert x_sample.shape[1] == 1 and seq % TOKEN_TILE == 0
    assert ns % SAMPLE_BLOCK == 0 and ns <= TOKEN_TILE and cache_k_win.shape[2] == WINDOW
    hp = x_prompt.reshape(nb * seq, D_MODEL)
    hs = x_sample.reshape(ns, D_MODEL)
    ckt = cache_k_win.transpose(0, 1, 3, 4, 2)
    cvt = cache_v_win.transpose(0, 1, 3, 4, 2)
    sct = state_conv.transpose(0, 2, 1, 3)
    qg2 = jnp.tile(q_norm_g, (1, 2))
    kg2 = jnp.tile(k_norm_g, (1, 2))
    kgc = kg2.reshape(DEPTH, 128, 1)
    conv_params = (conv_w, conv_b, conv_ln_g, conv_ln_b, conv_pw)
    prompt_states = [[] for _ in range(4)]
    sample_states = None
    for l in range(DEPTH):
        hp1, zp, mixc, c1, hs1, zs = _ffn_in(l, nb, hp, hs, ffn1_norm, mix_norm, *conv_params,
                                             ffn1_wg, ffn1_wu, ffn1_wd, w_in)
        cols = jnp.concatenate([zs[:, C_RQ:C_RQ + 512], zs[:, C_AK:C_AK + 256]], axis=1)
        cols = cols.T.reshape(768, ns // SAMPLE_BLOCK, SAMPLE_BLOCK).transpose(1, 0, 2)
        mixs, *sample_states = _mix_sample(l, zs, cols, state_ret, ckt, cvt, sct, sinks,
                                           ret_norm_g, qg2, kg2, kgc, conv_w, conv_b, conv_ln_g,
                                           conv_ln_b, conv_pw, sample_states)
        hp, hs, r1, k1, v1 = _mix_out_ffn(l, nb, zp, hp1, mixc, hs1, mixs, ffn2_norm, sinks,
                                          ret_norm_g, qg2, kg2, w_out, ffn2_wg, ffn2_wu, ffn2_wd)
        for lst, val in zip(prompt_states, (r1, k1, v1, c1)):
            lst.append(val)
    ret_p, kwin_p, vwin_p, conv_p = (jnp.stack(s) for s in prompt_states)
    ret_s, kwin_s, vwin_s, conv_s = sample_states
    return (hp.reshape(nb, seq, D_MODEL), hs.reshape(ns, 1, D_MODEL),
            ret_p.reshape(DEPTH, nb, RET_HEADS, RET_DK, RET_DV), ret_s,
            kwin_p.reshape(DEPTH, nb, KV_HEADS, HEAD_DIM, WINDOW).transpose(0, 1, 4, 2, 3),
            kwin_s.transpose(0, 1, 4, 2, 3),
            vwin_p.reshape(DEPTH, nb, KV_HEADS, HEAD_DIM, WINDOW).transpose(0, 1, 4, 2, 3),
            vwin_s.transpose(0, 1, 4, 2, 3),
            conv_p, conv_s.transpose(0, 2, 1, 3))
```

```python
import functools
import math

import jax
import jax.numpy as jnp
from jax import lax
from jax.experimental import pallas as pl
from jax.experimental.pallas import tpu as pltpu

F32 = jnp.float32
BF16 = jnp.bfloat16

D_MODEL = 1024
DEPTH = 2
PAST_LEN = 8192
RET_HEADS = 4
RET_DK = 64
RET_DV = 128
RET_CHUNK = 128
ATT_HEADS = 4
KV_HEADS = 2
HEAD_DIM = 64
WINDOW = 128
CONV_CH = 256
CONV_K = 31
FFN_DIM = 2816
EPS = 1e-6
NEG_INF = -1e30

C_RQ, C_RK, C_RV, C_RG = 0, 256, 512, 1024
C_AQ, C_AK, C_AV, C_CA, C_CG = 1536, 1792, 1920, 2048, 2304
IN_DIM = 2560
MIX_DIM = 1024
M_RET, M_ATT, M_CONV = 0, 512, 768

LOG_DECAY = [math.log1p(-2.0 ** (-5 - h)) for h in range(RET_HEADS)]
SLOPES = [2.0 ** (-8.0 * (h + 1) / ATT_HEADS) for h in range(ATT_HEADS)]

V7X_VMEM_LIMIT = 60 * 1024 * 1024
TOKEN_TILE = 512
FFN_COLS = 256
OUT_COLS = 1024
IN_COLS = 1280
STAGE_ROWS = 128
SAMPLE_BLOCK = 16
CONV_ROWS = 32
CONV_PAD = 32
CONV_SPAN = CONV_ROWS + CONV_PAD - 8


def _dot(a, b):
    return jnp.dot(a, b, preferred_element_type=F32)


def _dot_nt(a, b):
    return lax.dot_general(a, b, (((1,), (1,)), ((), ())), preferred_element_type=F32)


def _dot_tn(a, b):
    return lax.dot_general(a, b, (((0,), (0,)), ((), ())), preferred_element_type=F32)


def _silu(x):
    return x * jax.nn.sigmoid(x)


def _rms(x, g):
    return x * lax.rsqrt(jnp.mean(x * x, axis=-1, keepdims=True) + EPS) * g


def _seg_rms(x, g):
    r = lax.broadcasted_iota(jnp.int32, (128, 128), 0)
    c = lax.broadcasted_iota(jnp.int32, (128, 128), 1)
    seg = jnp.where((r < 64) == (c < 64), 1.0 / HEAD_DIM, 0.0).astype(BF16)
    xx = x * x
    hi = xx.astype(BF16)
    lo = (xx - hi.astype(F32)).astype(BF16)
    ms = _dot(hi, seg) + _dot(lo, seg)
    return x * lax.rsqrt(ms + EPS) * g


def _layer_norm(y, g, b):
    mu = jnp.mean(y, axis=-1, keepdims=True)
    d = y - mu
    var = jnp.mean(d * d, axis=-1, keepdims=True)
    return d * lax.rsqrt(var + EPS) * g + b


def _stage_copy(src, l, j, stage, sem, slot, width):
    return pltpu.make_async_copy(src.at[l, pl.ds(j * STAGE_ROWS, STAGE_ROWS), :],
                                 stage.at[slot, :, pl.ds(0, width)], sem.at[slot])


def _swap_inner_heads(t0, t1):
    low = lax.broadcasted_iota(jnp.int32, t0.shape, 1) < 64
    return (jnp.where(low, t0, pltpu.roll(t1, 64, 1)), jnp.where(low, pltpu.roll(t0, 64, 1), t1))


def _stage_weight(src, l, dst, stage, sem, permute_q_cols=False, gate_up_half=None):
    rows = dst.shape[0]
    width = src.shape[2]
    n = rows // STAGE_ROWS
    _stage_copy(src, l, 0, stage, sem, 0, width).start()

    def body(j, carry):
        slot = lax.rem(j, 2)

        @pl.when(j + 1 < n)
        def _():
            _stage_copy(src, l, j + 1, stage, sem, 1 - slot, width).start()

        _stage_copy(src, l, j, stage, sem, slot, width).wait()
        r0 = pl.multiple_of(j * STAGE_ROWS, STAGE_ROWS)
        if gate_up_half is None:
            dst[pl.ds(r0, STAGE_ROWS), :] = stage[slot, :, 0:width].astype(BF16)
        else:
            for c in range(width // FFN_COLS):
                d0 = (2 * c + gate_up_half) * FFN_COLS
                dst[pl.ds(r0, STAGE_ROWS), d0:d0 + FFN_COLS] = (
                    stage[slot, :, c * FFN_COLS:(c + 1) * FFN_COLS].astype(BF16))
        if permute_q_cols:
            t0, t1 = _swap_inner_heads(stage[slot, :, C_AQ:C_AQ + 128],
                                       stage[slot, :, C_AQ + 128:C_AQ + 256])
            dst[pl.ds(r0, STAGE_ROWS), C_AQ:C_AQ + 128] = t0.astype(BF16)
            dst[pl.ds(r0, STAGE_ROWS), C_AQ + 128:C_AQ + 256] = t1.astype(BF16)
        return carry

    lax.fori_loop(0, n, body, 0)


def _stage_w_out(src, l, dst, stage, sem):
    rows, width = dst.shape
    n = rows // STAGE_ROWS
    _stage_copy(src, l, 0, stage, sem, 0, width).start()
    for j in range(n):
        slot = j % 2
        if j + 1 < n:
            _stage_copy(src, l, j + 1, stage, sem, 1 - slot, width).start()
        _stage_copy(src, l, j, stage, sem, slot, width).wait()
        r0 = j * STAGE_ROWS
        if r0 == M_ATT:
            dst[M_ATT:M_ATT + 64, :] = stage[slot, 0:64, 0:width].astype(BF16)
            dst[M_ATT + 128:M_ATT + 192, :] = stage[slot, 64:128, 0:width].astype(BF16)
        elif r0 == M_ATT + 128:
            dst[M_ATT + 64:M_ATT + 128, :] = stage[slot, 0:64, 0:width].astype(BF16)
            dst[M_ATT + 192:M_ATT + 256, :] = stage[slot, 64:128, 0:width].astype(BF16)
        else:
            dst[r0:r0 + STAGE_ROWS, :] = stage[slot, :, 0:width].astype(BF16)


def _run(steps):
    for _ in steps:
        pass


def _interleave(major, minor, major_total, minor_total):
    major_done = minor_done = 0
    for cost in major:
        major_done += cost
        while minor_done * major_total < major_done * minor_total:
            step = next(minor, None)
            if step is None:
                break
            minor_done += step
    _run(minor)


def _matmul_cost(rows, k, n):
    return rows * k * n // (256 * 1024)


def _ffn_steps(x_ref, g, wgu_ref, wd_ref, out_ref, xn_scr, a_scr):
    rows = x_ref.shape[0]
    xn_scr[...] = _rms(x_ref[...], g).astype(BF16)
    for c in range(FFN_DIM // FFN_COLS):
        sl = slice(c * FFN_COLS, (c + 1) * FFN_COLS)
        gu = _dot(xn_scr[...], wgu_ref[:, 2 * c * FFN_COLS:2 * (c + 1) * FFN_COLS])
        gate, up = gu[:, 0:FFN_COLS], gu[:, FFN_COLS:2 * FFN_COLS]
        a_scr[:, sl] = (_silu(gate) * up).astype(BF16)
        yield 2 * _matmul_cost(rows, D_MODEL, FFN_COLS)
    for c in range(D_MODEL // OUT_COLS):
        sl = slice(c * OUT_COLS, (c + 1) * OUT_COLS)
        out_ref[:, sl] = x_ref[:, sl] + 0.5 * _dot(a_scr[...], wd_ref[:, sl])
        yield _matmul_cost(rows, FFN_DIM, OUT_COLS)


def _out_ffn_steps(l, x1_ref, mix_ref, x3_ref, g_ref, wout_b, wgu_b, wd_b, x2, xn, a):
    rows = x1_ref.shape[0]
    mix = mix_ref[...].astype(BF16)
    for c in range(D_MODEL // OUT_COLS):
        sl = slice(c * OUT_COLS, (c + 1) * OUT_COLS)
        x2[:, sl] = x1_ref[:, sl] + _dot(mix, wout_b[:, sl])
        yield _matmul_cost(rows, MIX_DIM, OUT_COLS)
    yield from _ffn_steps(x2, g_ref[l:l + 1, :], wgu_b, wd_b, x3_ref, xn, a)


def _out_ffn_cost(rows):
    return _matmul_cost(rows, MIX_DIM * D_MODEL + 3 * D_MODEL * FFN_DIM, 1)


def _ffn_in_steps(l, x_ref, x1_ref, z_ref, g1_ref, g2_ref, wgu_b, wd_b, win_b, xn, a):
    rows = x_ref.shape[0]
    yield from _ffn_steps(x_ref, g1_ref[l:l + 1, :], wgu_b, wd_b, x1_ref, xn, a)
    xn[...] = _rms(x1_ref[...], g2_ref[l:l + 1, :]).astype(BF16)
    for c in range(IN_DIM // IN_COLS):
        sl = slice(c * IN_COLS, (c + 1) * IN_COLS)
        z_ref[:, sl] = _dot(xn[...], win_b[:, sl])
        yield _matmul_cost(rows, D_MODEL, IN_COLS)


def _ffn_in_cost(rows):
    return _matmul_cost(rows, 3 * D_MODEL * FFN_DIM + D_MODEL * IN_DIM, 1)


def _ffn_in_kernel(l, n_tiles, tps, xp_ref, xs_ref, g1_ref, g2_ref, cw_ref, cb_ref, lng_ref, lnb_ref,
                   pw_ref, wg_hbm, wu_hbm, wd_hbm, win_hbm,
                   x1p_ref, zp_ref, mixc_ref, conv_ref, x1s_ref, zs_ref,
                   wgu_b, wd_b, win_b, stage, sem, xn_scr, a_scr, zcg, ubuf, ush, pw_b, cwb):
    i = pl.program_id(0)
    cprev = lax.rem(i + tps - 1, tps)
    dense_w = (g1_ref, g2_ref, wgu_b, wd_b, win_b)

    @pl.when(i == 0)
    def _():
        _stage_weight(wg_hbm, l, wgu_b, stage, sem, gate_up_half=0)
        _stage_weight(wu_hbm, l, wgu_b, stage, sem, gate_up_half=1)
        _stage_weight(wd_hbm, l, wd_b, stage, sem)
        _stage_weight(win_hbm, l, win_b, stage, sem, permute_q_cols=True)
        pw_b[...] = pw_ref[l].astype(BF16)
        for k in range(CONV_K):
            cwb[k] = jnp.broadcast_to(cw_ref[l, k:k + 1, :], (8, CONV_CH))

    @pl.when((i > 0) & (cprev == 0))
    def _():
        ubuf[0:CONV_PAD, :] = jnp.zeros((CONV_PAD, CONV_CH), F32)

    @pl.when((i > 0) & (cprev > 0))
    def _():
        ubuf[0:CONV_PAD, :] = ubuf[TOKEN_TILE:TOKEN_TILE + CONV_PAD, :]

    def dense():
        return _ffn_in_steps(l, xp_ref, x1p_ref, zp_ref, *dense_w, xn_scr, a_scr)

    def conv():
        return _conv_tile_steps(l, zcg, mixc_ref, cwb, cb_ref, lng_ref, lnb_ref, ubuf, ush, pw_b)

    @pl.when(i == 0)
    def _():
        _run(dense())

    @pl.when((i > 0) & (i < n_tiles))
    def _():
        _interleave(dense(), conv(), _ffn_in_cost(TOKEN_TILE), CONV_TILE_COST)

    @pl.when(i == n_tiles)
    def _():
        _run(conv())
        ns = xs_ref.shape[0]
        _run(_ffn_in_steps(l, xs_ref, x1s_ref, zs_ref, *dense_w, xn_scr.at[0:ns], a_scr.at[0:ns]))

    @pl.when(i < n_tiles)
    def _():
        zcg[...] = zp_ref[:, C_CA:C_CA + 2 * CONV_CH]

    @pl.when((i > 0) & (cprev == tps - 1))
    def _():
        conv_ref[0] = ubuf[CONV_PAD + TOKEN_TILE - (CONV_K - 1):CONV_PAD + TOKEN_TILE, :]


def _stage_scratch():
    return [pltpu.VMEM((2, STAGE_ROWS, FFN_DIM), F32), pltpu.SemaphoreType.DMA((2,))]


def _ffn_in(l, nb, xp, xs, g1, g2, cw, cb, lng, lnb, pw, wg, wu, wd, win):
    rows, ns = xp.shape[0], xs.shape[0]
    n_tiles = rows // TOKEN_TILE
    tps = n_tiles // nb
    cur = lambda w: pl.BlockSpec((TOKEN_TILE, w), lambda i: (jnp.minimum(i, n_tiles - 1), 0))
    prev = lambda w: pl.BlockSpec((TOKEN_TILE, w), lambda i: (jnp.maximum(i - 1, 0), 0))
    once = lambda w: pl.BlockSpec((ns, w), lambda i: (0, 0), pipeline_mode=pl.Buffered(1))
    whole = lambda a: pl.BlockSpec(a.shape, lambda i: (0,) * a.ndim)
    hbm = pl.BlockSpec(memory_space=pl.ANY)
    return pl.pallas_call(
        functools.partial(_ffn_in_kernel, l, n_tiles, tps),
        grid=(n_tiles + 1,),
        in_specs=[cur(D_MODEL), once(D_MODEL), whole(g1), whole(g2), whole(cw), whole(cb),
                  whole(lng), whole(lnb), whole(pw), hbm, hbm, hbm, hbm],
        out_specs=[cur(D_MODEL), cur(IN_DIM), prev(CONV_CH),
                   pl.BlockSpec((1, CONV_K - 1, CONV_CH),
                                lambda i: (jnp.maximum(i - 1, 0) // tps, 0, 0)),
                   once(D_MODEL), once(IN_DIM)],
        out_shape=[jax.ShapeDtypeStruct((rows, D_MODEL), F32),
                   jax.ShapeDtypeStruct((rows, IN_DIM), F32),
                   jax.ShapeDtypeStruct((rows, CONV_CH), BF16),
                   jax.ShapeDtypeStruct((nb, CONV_K - 1, CONV_CH), F32),
                   jax.ShapeDtypeStruct((ns, D_MODEL), F32),
                   jax.ShapeDtypeStruct((ns, IN_DIM), F32)],
        scratch_shapes=[pltpu.VMEM((D_MODEL, 2 * FFN_DIM), BF16),
                        pltpu.VMEM((FFN_DIM, D_MODEL), BF16), pltpu.VMEM((D_MODEL, IN_DIM), BF16),
                        *_stage_scratch(),
                        pltpu.VMEM((TOKEN_TILE, D_MODEL), BF16),
                        pltpu.VMEM((TOKEN_TILE, FFN_DIM), BF16),
                        pltpu.VMEM((TOKEN_TILE, 2 * CONV_CH), F32),
                        pltpu.VMEM((CONV_PAD + TOKEN_TILE, CONV_CH), F32),
                        pltpu.VMEM((2, 7, CONV_SPAN, CONV_CH), F32),
                        pltpu.VMEM((CONV_CH, CONV_CH), BF16),
                        pltpu.VMEM((CONV_K, 8, CONV_CH), F32)],
        compiler_params=pltpu.CompilerParams(dimension_semantics=("arbitrary",),
                                             vmem_limit_bytes=V7X_VMEM_LIMIT),
        name="ffn_in",
    )(xp, xs, g1, g2, cw, cb, lng, lnb, pw, wg, wu, wd, win)


def _mix_constants(dmask, qdec, kdec, abias):
    T = RET_CHUNK
    row_f = lax.broadcasted_iota(jnp.int32, (T, T), 0).astype(F32)
    col_f = lax.broadcasted_iota(jnp.int32, (T, T), 1).astype(F32)
    low = lax.broadcasted_iota(jnp.int32, (T, T), 1) < 64
    for h in range(RET_HEADS):
        lg = LOG_DECAY[h]
        diff = row_f - col_f
        dmask[h] = jnp.where(diff >= 0, jnp.exp(lg * jnp.maximum(diff, 0.0)), 0.0)
        qdec[h] = jnp.exp(lg * (row_f + 1.0))
    for p in range(RET_HEADS // 2):
        lgp = jnp.where(low, LOG_DECAY[2 * p], LOG_DECAY[2 * p + 1])
        kdec[p] = jnp.exp(lgp * (T - 1.0 - row_f))
    qpos = lax.broadcasted_iota(jnp.int32, (T, 2 * T), 0) + T
    kpos = lax.broadcasted_iota(jnp.int32, (T, 2 * T), 1)
    dist = (qpos - kpos).astype(F32)
    for h in range(ATT_HEADS):
        abias[h] = jnp.where((dist >= 0) & (dist < WINDOW), -SLOPES[h] * dist, NEG_INF)


def _mix_tile_steps(l, first, z_ref, mix_ref, sinks_ref, rng_ref, qg_ref, kg_ref,
                    s_scr, kbuf, vbuf, qbuf, klast, dmask, qdec, kdec, abias):
    T = RET_CHUNK
    NCH = TOKEN_TILE // T
    row_i = lax.broadcasted_iota(jnp.int32, (T, T), 0)
    low = lax.broadcasted_iota(jnp.int32, (T, T), 1) < 64

    def zc(r0, nrows, a, w):
        return z_ref[r0:r0 + nrows, a:a + w]

    def seg_mean_sq(x):
        low_x = lax.broadcasted_iota(jnp.int32, x.shape, 1) < 64
        xx = x * x
        s_lo = jnp.sum(jnp.where(low_x, xx, 0.0), axis=-1, keepdims=True)
        s_hi = jnp.sum(jnp.where(low_x, 0.0, xx), axis=-1, keepdims=True)
        return jnp.where(low_x, s_lo, s_hi) * (1.0 / HEAD_DIM)

    def retention_unit(p, j):
        s_cur = s_scr[p]
        r0 = T * j
        qp = zc(r0, T, C_RQ + 128 * p, 128)
        kp = zc(r0, T, C_RK + 128 * p, 128) * (RET_DK ** -0.5)
        vb = zc(r0, T, C_RV + 256 * p, 256).astype(BF16)
        kb = kp.astype(BF16)
        sb = s_cur.astype(BF16)
        qm = jnp.concatenate([jnp.where(low, qp, 0.0), jnp.where(low, 0.0, qp)], axis=0).astype(BF16)
        a = _dot_nt(qm, kb)
        qs = _dot(qm, sb)
        upd = _dot_tn((kp * kdec[p]).astype(BF16), vb)
        yield RET_COST // 4
        ab = (a * dmask[2 * p:2 * p + 2].reshape(2 * T, T)).astype(BF16)
        yield RET_COST // 4
        o = _dot(ab, vb)
        yield RET_COST // 4
        for e in range(2):
            h = 2 * p + e
            oh = _rms(o[T * e:T * e + T, 128 * e:128 * e + 128] + qs[T * e:T * e + T, :] * qdec[h],
                      rng_ref[l, h:h + 1, :])
            mix_ref[r0:r0 + T, M_RET + 128 * h:M_RET + 128 * h + 128] = (
                oh * _silu(zc(r0, T, C_RG + 128 * h, 128))).astype(BF16)
        top = row_i < 64
        new = jnp.where(top, upd[:, 0:128], upd[:, 128:256])
        cd = jnp.where(top, math.exp(LOG_DECAY[2 * p] * T), math.exp(LOG_DECAY[2 * p + 1] * T))
        s_scr[p] = s_cur * cd + new
        yield RET_COST // 4

    def attention_prep():
        ak = zc(0, TOKEN_TILE, C_AK, 128)
        aq = [zc(0, TOKEN_TILE, C_AQ + 128 * t, 128) for t in range(2)]
        ms_k = seg_mean_sq(ak)
        ms_q = [seg_mean_sq(aq[t]) for t in range(2)]
        yield ATT_PREP_COST // 2
        kn = ak * lax.rsqrt(ms_k + EPS) * kg_ref[l:l + 1, :]
        klast[...] = kn[TOKEN_TILE - T:TOKEN_TILE, :]
        kbuf[T:T + TOKEN_TILE, :] = kn.astype(BF16)
        vbuf[T:T + TOKEN_TILE, :] = zc(0, TOKEN_TILE, C_AV, 128).astype(BF16)
        for t in range(2):
            qt = aq[t] * lax.rsqrt(ms_q[t] + EPS) * qg_ref[l:l + 1, :] * (HEAD_DIM ** -0.5)
            for j in range(NCH):
                qj = qt[T * j:T * j + T, :]
                qbuf[j, (2 * t) * T:(2 * t + 1) * T, :] = jnp.where(low, qj, 0.0).astype(BF16)
                qbuf[j, (2 * t + 1) * T:(2 * t + 2) * T, :] = jnp.where(low, 0.0, qj).astype(BF16)
        yield ATT_PREP_COST // 2

    def attention_unit(j):
        r0 = T * j
        kcat = kbuf[r0:r0 + 2 * T, :]
        vcat = vbuf[r0:r0 + 2 * T, :]
        s = _dot_nt(qbuf[j], kcat)
        yield ATT_COST // 2
        probs = []
        for t in range(2):
            for kvh in range(KV_HEADS):
                head = 2 * kvh + t
                sink = sinks_ref[l, head]
                bias = abias[head]
                if j == 0:
                    no_prev = lax.broadcasted_iota(jnp.int32, (T, 2 * T), 1) < jnp.where(first, T, 0)
                    bias = jnp.where(no_prev, NEG_INF, bias)
                blk = 2 * t + kvh
                sb = s[T * blk:T * blk + T, :] + bias
                m = jnp.maximum(jnp.max(sb, axis=-1, keepdims=True), sink)
                e = jnp.exp(sb - m)
                inv = 1.0 / (jnp.sum(e, axis=-1, keepdims=True) + jnp.exp(sink - m))
                probs.append((e * inv).astype(BF16))
            yield ATT_COST
        o = _dot(jnp.concatenate(probs, axis=0), vcat)
        for t in range(2):
            mix_ref[r0:r0 + T, M_ATT + 128 * t:M_ATT + 128 * t + 128] = jnp.where(
                low, o[T * 2 * t:T * (2 * t + 1), :], o[T * (2 * t + 1):T * (2 * t + 2), :]
            ).astype(BF16)
        yield ATT_COST // 2

    def ret_lane():
        for j in range(NCH):
            for p in range(RET_HEADS // 2):
                yield from retention_unit(p, j)

    def att_lane():
        yield from attention_prep()
        for j in range(NCH):
            yield from attention_unit(j)

    live = [att_lane(), ret_lane()]
    while live:
        for lane in list(live):
            cost = next(lane, None)
            if cost is None:
                live.remove(lane)
            else:
                yield cost


def _conv_tile_steps(l, zcg_ref, out_ref, cwb, cb_ref, lng_ref, lnb_ref, ubuf, ush, pw_b):
    ubuf[CONV_PAD:CONV_PAD + TOKEN_TILE, :] = (
        zcg_ref[:, 0:CONV_CH] * jax.nn.sigmoid(zcg_ref[:, CONV_CH:2 * CONV_CH]))
    yield CONV_PREP_COST
    for rb in range(TOKEN_TILE // CONV_ROWS):
        base = rb * CONV_ROWS
        sh = ush.at[rb % 2]
        for r in range(1, 8):
            sh[r - 1] = ubuf[base + r:base + r + CONV_SPAN, :]
        acc = None
        for k in range(CONV_K):
            off = CONV_PAD - (CONV_K - 1) + k
            a8, r = off // 8 * 8, off % 8
            src = (ubuf[base + a8:base + a8 + CONV_ROWS, :] if r == 0
                   else sh[r - 1, a8:a8 + CONV_ROWS, :])
            term = cwb[k][None] * src.reshape(CONV_ROWS // 8, 8, CONV_CH)
            acc = term if acc is None else acc + term
        y = _silu(_layer_norm(acc.reshape(CONV_ROWS, CONV_CH) + cb_ref[l:l + 1, :],
                              lng_ref[l:l + 1, :], lnb_ref[l:l + 1, :])).astype(BF16)
        yield CONV_COST - CONV_COST // 8
        out_ref[base:base + CONV_ROWS, :] = _dot(y, pw_b[...]).astype(BF16)
        yield CONV_COST // 8


RET_COST, ATT_PREP_COST, ATT_COST, CONV_PREP_COST, CONV_COST = 160, 250, 220, 200, 410
MIX_COST = ((RET_HEADS // 2) * (TOKEN_TILE // RET_CHUNK) * RET_COST + ATT_PREP_COST
            + 3 * (TOKEN_TILE // RET_CHUNK) * ATT_COST)
CONV_TILE_COST = CONV_PREP_COST + (TOKEN_TILE // CONV_ROWS) * CONV_COST


def _mix_out_ffn_kernel(l, n_tiles, tps, sinks_ref, z_ref, x1p_ref, mixc_ref, x1s_ref, mixs_ref, g_ref,
                        rng_ref, qg_ref, kg_ref, wout_hbm, wg_hbm, wu_hbm, wd_hbm,
                        x3p_ref, x3s_ref, ret_ref, kwin_ref, vwin_ref,
                        wout_b, wgu_b, wd_b, stage, sem, x2_scr, xn_scr, a_scr, mix_new, mix_old,
                        s_scr, kbuf, vbuf, qbuf, klast, dmask, qdec, kdec, abias):
    i = pl.program_id(0)
    c = lax.rem(i, tps)
    T = RET_CHUNK
    mix_args = (sinks_ref, rng_ref, qg_ref, kg_ref,
                s_scr, kbuf, vbuf, qbuf, klast, dmask, qdec, kdec, abias)
    dense_w = (g_ref, wout_b, wgu_b, wd_b)

    @pl.when(i == 0)
    def _():
        _stage_w_out(wout_hbm, l, wout_b, stage, sem)
        _stage_weight(wg_hbm, l, wgu_b, stage, sem, gate_up_half=0)
        _stage_weight(wu_hbm, l, wgu_b, stage, sem, gate_up_half=1)
        _stage_weight(wd_hbm, l, wd_b, stage, sem)
        _mix_constants(dmask, qdec, kdec, abias)

    @pl.when(c == 0)
    def _():
        s_scr[...] = jnp.zeros_like(s_scr)
        kbuf[0:T, :] = jnp.zeros((T, 128), BF16)
        vbuf[0:T, :] = jnp.zeros((T, 128), BF16)

    @pl.when(c > 0)
    def _():
        kbuf[0:T, :] = kbuf[TOKEN_TILE:TOKEN_TILE + T, :]
        vbuf[0:T, :] = vbuf[TOKEN_TILE:TOKEN_TILE + T, :]

    @pl.when(i > 0)
    def _():
        mix_old[:, 0:M_CONV] = mix_new[:, 0:M_CONV]
        mix_old[:, M_CONV:MIX_DIM] = mixc_ref[...]

    def mixers():
        return _mix_tile_steps(l, c == 0, z_ref, mix_new, *mix_args)

    def dense():
        return _out_ffn_steps(l, x1p_ref, mix_old, x3p_ref, *dense_w, x2_scr, xn_scr, a_scr)

    @pl.when(i == 0)
    def _():
        _run(mixers())

    @pl.when((i > 0) & (i < n_tiles))
    def _():
        _interleave(dense(), mixers(), _out_ffn_cost(TOKEN_TILE), MIX_COST)

    @pl.when(i == n_tiles)
    def _():
        _run(dense())
        ns = x1s_ref.shape[0]
        _run(_out_ffn_steps(l, x1s_ref, mixs_ref, x3s_ref, *dense_w,
                            x2_scr.at[0:ns], xn_scr.at[0:ns], a_scr.at[0:ns]))

    @pl.when((c == tps - 1) & (i < n_tiles))
    def _():
        ret_ref[0] = s_scr[...]
        kwin_ref[0] = klast[...].T
        vwin_ref[0] = z_ref[TOKEN_TILE - T:TOKEN_TILE, C_AV:C_AV + 128].T


def _mix_out_ffn(l, nb, z, x1p, mixc, x1s, mixs, g, sinks, rng, qg2, kg2, wout, wg, wu, wd):
    rows, ns = x1p.shape[0], x1s.shape[0]
    n_tiles = rows // TOKEN_TILE
    tps = n_tiles // nb
    T = RET_CHUNK
    cur = lambda w: pl.BlockSpec((TOKEN_TILE, w), lambda i: (jnp.minimum(i, n_tiles - 1), 0))
    prev = lambda w: pl.BlockSpec((TOKEN_TILE, w), lambda i: (jnp.maximum(i - 1, 0), 0))
    once = lambda w: pl.BlockSpec((ns, w), lambda i: (0, 0), pipeline_mode=pl.Buffered(1))
    whole = lambda a: pl.BlockSpec(a.shape, lambda i: (0,) * a.ndim)
    per_seq = lambda shape: pl.BlockSpec(
        (1,) + shape, lambda i: (jnp.minimum(i, n_tiles - 1) // tps,) + (0,) * len(shape))
    hbm = pl.BlockSpec(memory_space=pl.ANY)
    return pl.pallas_call(
        functools.partial(_mix_out_ffn_kernel, l, n_tiles, tps),
        grid=(n_tiles + 1,),
        in_specs=[pl.BlockSpec(memory_space=pltpu.SMEM), cur(IN_DIM), prev(D_MODEL), prev(CONV_CH),
                  once(D_MODEL), once(MIX_DIM), whole(g),
                  whole(rng), whole(qg2), whole(kg2), hbm, hbm, hbm, hbm],
        out_specs=[prev(D_MODEL), once(D_MODEL),
                   per_seq((2, 128, RET_DV)), per_seq((128, WINDOW)), per_seq((128, WINDOW))],
        out_shape=[jax.ShapeDtypeStruct((rows, D_MODEL), F32),
                   jax.ShapeDtypeStruct((ns, D_MODEL), F32),
                   jax.ShapeDtypeStruct((nb, 2, 128, RET_DV), F32),
                   jax.ShapeDtypeStruct((nb, 128, WINDOW), F32),
                   jax.ShapeDtypeStruct((nb, 128, WINDOW), F32)],
        scratch_shapes=[pltpu.VMEM((MIX_DIM, D_MODEL), BF16),
                        pltpu.VMEM((D_MODEL, 2 * FFN_DIM), BF16),
                        pltpu.VMEM((FFN_DIM, D_MODEL), BF16),
                        *_stage_scratch(),
                        pltpu.VMEM((TOKEN_TILE, D_MODEL), F32),
                        pltpu.VMEM((TOKEN_TILE, D_MODEL), BF16),
                        pltpu.VMEM((TOKEN_TILE, FFN_DIM), BF16),
                        pltpu.VMEM((TOKEN_TILE, MIX_DIM), BF16),
                        pltpu.VMEM((TOKEN_TILE, MIX_DIM), BF16),
                        pltpu.VMEM((2, 128, RET_DV), F32),
                        pltpu.VMEM((T + TOKEN_TILE, 128), BF16),
                        pltpu.VMEM((T + TOKEN_TILE, 128), BF16),
                        pltpu.VMEM((TOKEN_TILE // T, 2 * KV_HEADS * T, 128), BF16),
                        pltpu.VMEM((T, 128), F32),
                        pltpu.VMEM((RET_HEADS, T, T), F32), pltpu.VMEM((RET_HEADS, T, T), F32),
                        pltpu.VMEM((2, T, T), F32), pltpu.VMEM((ATT_HEADS, T, 2 * T), F32)],
        compiler_params=pltpu.CompilerParams(dimension_semantics=("arbitrary",),
                                             vmem_limit_bytes=V7X_VMEM_LIMIT),
        name="mix_out_ffn",
    )(sinks, z, x1p, mixc, x1s, mixs, g, rng, qg2, kg2, wout, wg, wu, wd)


def _mix_sample_kernel(l, sinks_ref, z_ref, col_ref, s_ref, ck_ref, cv_ref, sc_ref,
                       rng_ref, qg_ref, kg_ref, kgc_ref, cw_ref, cb_ref, lng_ref, lnb_ref, pw_ref,
                       *rest):
    mix_ref, so_ref, cko_ref, cvo_ref, sco_ref, o_scr, oa_scr = rest[-7:]
    NB = SAMPLE_BLOCK
    P = WINDOW

    def put(ref, idx, val):
        for slab in range(ref.shape[0]):
            ref[(slab,) + idx] = val

    for bl in range(NB):
        for h in range(RET_HEADS):
            gamma = math.exp(LOG_DECAY[h])
            r0 = RET_DK * h
            S = s_ref[0, bl, h]
            qc = col_ref[0, r0:r0 + RET_DK, bl:bl + 1]
            kc = col_ref[0, 256 + r0:256 + r0 + RET_DK, bl:bl + 1] * (RET_DK ** -0.5)
            v = z_ref[bl:bl + 1, C_RV + 128 * h:C_RV + 128 * h + 128]
            qk = jnp.sum(qc * kc, axis=0, keepdims=True)
            o_scr[bl:bl + 1, 128 * h:128 * h + 128] = (
                gamma * jnp.sum(qc * S, axis=0, keepdims=True) + qk * v)
            put(so_ref, (bl, h), gamma * S + kc * v)
    for h in range(RET_HEADS):
        o = _rms(o_scr[:, 128 * h:128 * h + 128], rng_ref[l, h:h + 1, :])
        mix_ref[:, M_RET + 128 * h:M_RET + 128 * h + 128] = (
            o * _silu(z_ref[:, C_RG + 128 * h:C_RG + 128 * h + 128]))

    q_tiles = [_seg_rms(z_ref[:, C_AQ + 128 * t:C_AQ + 128 * t + 128], qg_ref[l:l + 1, :])
               * (HEAD_DIM ** -0.5) for t in range(2)]
    kn = _seg_rms(z_ref[:, C_AK:C_AK + 128], kg_ref[l:l + 1, :])
    vn = z_ref[:, C_AV:C_AV + 128]
    akc = col_ref[0, 512:640, :]
    vnc = col_ref[0, 640:768, :]
    ms = jnp.concatenate(
        [jnp.broadcast_to(jnp.mean(akc[64 * s:64 * s + 64, :] ** 2, axis=0, keepdims=True), (64, NB))
         for s in range(KV_HEADS)], axis=0)
    knc = akc * lax.rsqrt(ms + EPS) * kgc_ref[l]
    rid = lax.broadcasted_iota(jnp.int32, (8, 128), 0)
    lane = lax.broadcasted_iota(jnp.int32, (8, 128), 1)
    sel = ((rid % 2 == 0) == (lane < 64)) & (rid < ATT_HEADS)
    rcol = lax.broadcasted_iota(jnp.int32, (8, 1), 0)
    heads = [2 * (r % 2) + r // 2 for r in range(ATT_HEADS)]
    slope_col = jnp.zeros((8, 1), F32)
    sink_col = jnp.zeros((8, 1), F32)
    for r, hd in enumerate(heads):
        slope_col = jnp.where(rcol == r, SLOPES[hd], slope_col)
        sink_col = jnp.where(rcol == r, sinks_ref[l, hd], sink_col)
    dist = float(P) - lane.astype(F32)
    key_ok = (dist < float(WINDOW)) & (lane + (PAST_LEN - P) >= 0)
    newest = lax.broadcasted_iota(jnp.int32, (128, P), 1) == P - 1
    for bl in range(NB):
        KT = ck_ref[0, bl].reshape(128, P)
        VT = cv_ref[0, bl].reshape(128, P)
        qrows = jnp.where(rid < 2, q_tiles[0][bl:bl + 1, :], q_tiles[1][bl:bl + 1, :])
        q4 = jnp.where(sel, qrows, 0.0)
        s = _dot(q4.astype(BF16), KT.astype(BF16))
        s = jnp.where(key_ok, s - slope_col * dist, NEG_INF)
        s_new = jnp.sum(q4 * kn[bl:bl + 1, :], axis=-1, keepdims=True)
        m = jnp.maximum(jnp.maximum(jnp.max(s, axis=-1, keepdims=True), s_new), sink_col)
        e = jnp.exp(s - m)
        e_new = jnp.exp(s_new - m)
        inv = 1.0 / (jnp.sum(e, axis=-1, keepdims=True) + e_new + jnp.exp(sink_col - m))
        o = _dot_nt((e * inv).astype(BF16), VT.astype(BF16)) + (e_new * inv) * vn[bl:bl + 1, :]
        for t in range(2):
            oa_scr[bl:bl + 1, 128 * t:128 * t + 128] = jnp.where(
                lane[0:1, :] < 64, o[2 * t:2 * t + 1, :], o[2 * t + 1:2 * t + 2, :])
        put(cko_ref, (bl,), jnp.where(newest, knc[:, bl:bl + 1], pltpu.roll(KT, P - 1, 1)).reshape(
            KV_HEADS, HEAD_DIM, P))
        put(cvo_ref, (bl,), jnp.where(newest, vnc[:, bl:bl + 1], pltpu.roll(VT, P - 1, 1)).reshape(
            KV_HEADS, HEAD_DIM, P))
    mix_ref[:, M_ATT:M_ATT + 256] = oa_scr[...]

    u = z_ref[:, C_CA:C_CA + CONV_CH] * jax.nn.sigmoid(z_ref[:, C_CG:C_CG + CONV_CH])
    KT1 = CONV_K - 1
    y = cw_ref[l, KT1:KT1 + 1, :] * u
    for k in range(KT1):
        y = y + cw_ref[l, k:k + 1, :] * sc_ref[0, k]
        if k > 0:
            put(sco_ref, (k - 1,), sc_ref[0, k])
    put(sco_ref, (KT1 - 1,), u)
    y = _silu(_layer_norm(y + cb_ref[l:l + 1, :], lng_ref[l:l + 1, :], lnb_ref[l:l + 1, :]))
    mix_ref[:, M_CONV:M_CONV + CONV_CH] = _dot(y.astype(BF16), pw_ref[l].astype(BF16))


def _mix_sample(l, z, cols, s, ckt, cvt, sct, sinks, rng, qg2, kg2, kgc, cw, cb, lng, lnb, pw, prev):
    ns = z.shape[0]
    NB = SAMPLE_BLOCK
    whole = lambda a: pl.BlockSpec(a.shape, lambda i: (0,) * a.ndim)

    def specs(slabs, at):
        return [pl.BlockSpec((slabs, NB, RET_HEADS, RET_DK, RET_DV), lambda i: (at, i, 0, 0, 0)),
                pl.BlockSpec((slabs, NB, KV_HEADS, HEAD_DIM, WINDOW), lambda i: (at, i, 0, 0, 0)),
                pl.BlockSpec((slabs, NB, KV_HEADS, HEAD_DIM, WINDOW), lambda i: (at, i, 0, 0, 0)),
                pl.BlockSpec((slabs, CONV_K - 1, NB, CONV_CH), lambda i: (at, 0, i, 0))]

    state_specs = specs(1, l)
    out_state_specs = specs(s.shape[0], 0) if prev is None else state_specs
    in_specs = [pl.BlockSpec(memory_space=pltpu.SMEM),
                pl.BlockSpec((NB, IN_DIM), lambda i: (i, 0)),
                pl.BlockSpec((1, 768, NB), lambda i: (i, 0, 0)),
                *state_specs,
                whole(rng), whole(qg2), whole(kg2), whole(kgc), whole(cw), whole(cb), whole(lng),
                whole(lnb), whole(pw)]
    args = [sinks, z, cols, s, ckt, cvt, sct, rng, qg2, kg2, kgc, cw, cb, lng, lnb, pw]
    aliases = {}
    if prev is not None:
        for k, a in enumerate(prev):
            aliases[len(args)] = 1 + k
            in_specs.append(pl.BlockSpec(memory_space=pl.ANY))
            args.append(a)
    return pl.pallas_call(
        functools.partial(_mix_sample_kernel, l),
        grid=(ns // NB,),
        in_specs=in_specs,
        out_specs=[pl.BlockSpec((NB, MIX_DIM), lambda i: (i, 0)), *out_state_specs],
        out_shape=[jax.ShapeDtypeStruct((ns, MIX_DIM), F32),
                   jax.ShapeDtypeStruct(s.shape, F32), jax.ShapeDtypeStruct(ckt.shape, F32),
                   jax.ShapeDtypeStruct(cvt.shape, F32), jax.ShapeDtypeStruct(sct.shape, F32)],
        scratch_shapes=[pltpu.VMEM((NB, 512), F32), pltpu.VMEM((NB, 256), F32)],
        input_output_aliases=aliases,
        compiler_params=pltpu.CompilerParams(dimension_semantics=("arbitrary",)),
        name="mix_sample",
    )(*args)


def kernel(x_prompt, x_sample, state_ret, cache_k_win, cache_v_win, state_conv, ffn1_norm, ffn1_wg, ffn1_wu, ffn1_wd, mix_norm, w_in, ret_norm_g, q_norm_g, k_norm_g, sinks, conv_w, conv_b, conv_ln_g, conv_ln_b, conv_pw, w_out, ffn2_norm, ffn2_wg, ffn2_wu, ffn2_wd):
    nb, seq, _ = x_prompt.shape
    ns = x_sample.shape[0]
    assert x_sample.shape[1] == 1 and seq % TOKEN_TILE == 0
    assert ns % SAMPLE_BLOCK == 0 and ns <= TOKEN_TILE and cache_k_win.shape[2] == WINDOW
    hp = x_prompt.reshape(nb * seq, D_MODEL)
    hs = x_sample.reshape(ns, D_MODEL)
    ckt = cache_k_win.transpose(0, 1, 3, 4, 2)
    cvt = cache_v_win.transpose(0, 1, 3, 4, 2)
    sct = state_conv.transpose(0, 2, 1, 3)
    qg2 = jnp.tile(q_norm_g, (1, 2))
    kg2 = jnp.tile(k_norm_g, (1, 2))
    kgc = kg2.reshape(DEPTH, 128, 1)
    conv_params = (conv_w, conv_b, conv_ln_g, conv_ln_b, conv_pw)
    prompt_states = [[] for _ in range(4)]
    sample_states = None
    for l in range(DEPTH):
        hp1, zp, mixc, c1, hs1, zs = _ffn_in(l, nb, hp, hs, ffn1_norm, mix_norm, *conv_params,
                                             ffn1_wg, ffn1_wu, ffn1_wd, w_in)
        cols = jnp.concatenate([zs[:, C_RQ:C_RQ + 512], zs[:, C_AK:C_AK + 256]], axis=1)
        cols = cols.T.reshape(768, ns // SAMPLE_BLOCK, SAMPLE_BLOCK).transpose(1, 0, 2)
        mixs, *sample_states = _mix_sample(l, zs, cols, state_ret, ckt, cvt, sct, sinks,
                                           ret_norm_g, qg2, kg2, kgc, conv_w, conv_b, conv_ln_g,
                                           conv_ln_b, conv_pw, sample_states)
        hp, hs, r1, k1, v1 = _mix_out_ffn(l, nb, zp, hp1, mixc, hs1, mixs, ffn2_norm, sinks,
                                          ret_norm_g, qg2, kg2, w_out, ffn2_wg, ffn2_wu, ffn2_wd)
        for lst, val in zip(prompt_states, (r1, k1, v1, c1)):
            lst.append(val)
    ret_p, kwin_p, vwin_p, conv_p = (jnp.stack(s) for s in prompt_states)
    ret_s, kwin_s, vwin_s, conv_s = sample_states
    return (hp.reshape(nb, seq, D_MODEL), hs.reshape(ns, 1, D_MODEL),
            ret_p.reshape(DEPTH, nb, RET_HEADS, RET_DK, RET_DV), ret_s,
            kwin_p.reshape(DEPTH, nb, KV_HEADS, HEAD_DIM, WINDOW).transpose(0, 1, 4, 2, 3),
            kwin_s.transpose(0, 1, 4, 2, 3),
            vwin_p.reshape(DEPTH, nb, KV_HEADS, HEAD_DIM, WINDOW).transpose(0, 1, 4, 2, 3),
            vwin_s.transpose(0, 1, 4, 2, 3),
            conv_p, conv_s.transpose(0, 2, 1, 3))
```

```python
import functools
import math

import jax
import jax.numpy as jnp
from jax import lax
from jax.experimental import pallas as pl
from jax.experimental.pallas import tpu as pltpu

F32 = jnp.float32
BF16 = jnp.bfloat16

D_MODEL = 1024
DEPTH = 2
PAST_LEN = 8192
RET_HEADS = 4
RET_DK = 64
RET_DV = 128
RET_CHUNK = 128
ATT_HEADS = 4
KV_HEADS = 2
HEAD_DIM = 64
WINDOW = 128
CONV_CH = 256
CONV_K = 31
FFN_DIM = 2816
EPS = 1e-6
NEG_INF = -1e30

C_RQ, C_RK, C_RV, C_RG = 0, 256, 512, 1024
C_AQ, C_AK, C_AV, C_CA, C_CG = 1536, 1792, 1920, 2048, 2304
IN_DIM = 2560
MIX_DIM = 1024
M_RET, M_ATT, M_CONV = 0, 512, 768

LOG_DECAY = [math.log1p(-2.0 ** (-5 - h)) for h in range(RET_HEADS)]
SLOPES = [2.0 ** (-8.0 * (h + 1) / ATT_HEADS) for h in range(ATT_HEADS)]

V7X_VMEM_LIMIT = 60 * 1024 * 1024
TOKEN_TILE = 512
FFN_COLS = 256
OUT_COLS = 512
STAGE_ROWS = 128
STAGE_SLOTS = 4
SAMPLE_BLOCK = 16
CONV_ROWS = 32
CONV_PAD = 32
CONV_SPAN = CONV_ROWS + CONV_PAD - 8


def _dot(a, b):
    return jnp.dot(a, b, preferred_element_type=F32)


def _dot_nt(a, b):
    return lax.dot_general(a, b, (((1,), (1,)), ((), ())), preferred_element_type=F32)


def _dot_tn(a, b):
    return lax.dot_general(a, b, (((0,), (0,)), ((), ())), preferred_element_type=F32)


def _silu(x):
    return x * jax.nn.sigmoid(x)


def _rms(x, g):
    return x * lax.rsqrt(jnp.mean(x * x, axis=-1, keepdims=True) + EPS) * g


def _seg_rms(x, g):
    r = lax.broadcasted_iota(jnp.int32, (128, 128), 0)
    c = lax.broadcasted_iota(jnp.int32, (128, 128), 1)
    seg = jnp.where((r < 64) == (c < 64), 1.0 / HEAD_DIM, 0.0).astype(BF16)
    xx = x * x
    hi = xx.astype(BF16)
    lo = (xx - hi.astype(F32)).astype(BF16)
    ms = _dot(hi, seg) + _dot(lo, seg)
    return x * lax.rsqrt(ms + EPS) * g


def _layer_norm(y, g, b):
    mu = jnp.mean(y, axis=-1, keepdims=True)
    d = y - mu
    var = jnp.mean(d * d, axis=-1, keepdims=True)
    return d * lax.rsqrt(var + EPS) * g + b


def _stage_copy(src, l, j, stage, sem, slot, width):
    return pltpu.make_async_copy(src.at[l, pl.ds(j * STAGE_ROWS, STAGE_ROWS), :],
                                 stage.at[slot, :, pl.ds(0, width)], sem.at[slot])


def _swap_inner_heads(t0, t1):
    low = lax.broadcasted_iota(jnp.int32, t0.shape, 1) < 64
    return (jnp.where(low, t0, pltpu.roll(t1, 64, 1)), jnp.where(low, pltpu.roll(t0, 64, 1), t1))


def _stage_weight(src, l, dst, stage, sem, permute_q_cols=False, gate_up_half=None):
    rows = dst.shape[0]
    width = src.shape[2]
    n = rows // STAGE_ROWS
    ahead = STAGE_SLOTS - 1
    for j0 in range(min(ahead, n)):
        _stage_copy(src, l, j0, stage, sem, j0, width).start()

    def body(j, carry):
        slot = lax.rem(j, STAGE_SLOTS)

        @pl.when(j + ahead < n)
        def _():
            _stage_copy(src, l, j + ahead, stage, sem, lax.rem(j + ahead, STAGE_SLOTS), width).start()

        _stage_copy(src, l, j, stage, sem, slot, width).wait()
        r0 = pl.multiple_of(j * STAGE_ROWS, STAGE_ROWS)
        if gate_up_half is None:
            dst[pl.ds(r0, STAGE_ROWS), :] = stage[slot, :, 0:width].astype(BF16)
        else:
            for c in range(width // FFN_COLS):
                d0 = (2 * c + gate_up_half) * FFN_COLS
                dst[pl.ds(r0, STAGE_ROWS), d0:d0 + FFN_COLS] = (
                    stage[slot, :, c * FFN_COLS:(c + 1) * FFN_COLS].astype(BF16))
        if permute_q_cols:
            t0, t1 = _swap_inner_heads(stage[slot, :, C_AQ:C_AQ + 128],
                                       stage[slot, :, C_AQ + 128:C_AQ + 256])
            dst[pl.ds(r0, STAGE_ROWS), C_AQ:C_AQ + 128] = t0.astype(BF16)
            dst[pl.ds(r0, STAGE_ROWS), C_AQ + 128:C_AQ + 256] = t1.astype(BF16)
        return carry

    lax.fori_loop(0, n, body, 0)


def _stage_w_out(src, l, dst, stage, sem):
    rows, width = dst.shape
    n = rows // STAGE_ROWS
    ahead = STAGE_SLOTS - 1
    for j0 in range(min(ahead, n)):
        _stage_copy(src, l, j0, stage, sem, j0, width).start()
    for j in range(n):
        slot = j % STAGE_SLOTS
        if j + ahead < n:
            _stage_copy(src, l, j + ahead, stage, sem, (j + ahead) % STAGE_SLOTS, width).start()
        _stage_copy(src, l, j, stage, sem, slot, width).wait()
        r0 = j * STAGE_ROWS
        if r0 == M_ATT:
            dst[M_ATT:M_ATT + 64, :] = stage[slot, 0:64, 0:width].astype(BF16)
            dst[M_ATT + 128:M_ATT + 192, :] = stage[slot, 64:128, 0:width].astype(BF16)
        elif r0 == M_ATT + 128:
            dst[M_ATT + 64:M_ATT + 128, :] = stage[slot, 0:64, 0:width].astype(BF16)
            dst[M_ATT + 192:M_ATT + 256, :] = stage[slot, 64:128, 0:width].astype(BF16)
        else:
            dst[r0:r0 + STAGE_ROWS, :] = stage[slot, :, 0:width].astype(BF16)


def _run(steps):
    for _ in steps:
        pass


def _interleave(major, minor, major_total, minor_total):
    major_done = minor_done = 0
    for cost in major:
        major_done += cost
        while minor_done * major_total < major_done * minor_total:
            step = next(minor, None)
            if step is None:
                break
            minor_done += step
    _run(minor)


def _matmul_cost(rows, k, n):
    return rows * k * n // (256 * 1024)


def _ffn_steps(x_ref, g, wgu_ref, wd_ref, out_ref, xn_scr, a_scr):
    rows = x_ref.shape[0]
    xn_scr[...] = _rms(x_ref[...], g).astype(BF16)
    for c in range(FFN_DIM // FFN_COLS):
        sl = slice(c * FFN_COLS, (c + 1) * FFN_COLS)
        gu = _dot(xn_scr[...], wgu_ref[:, 2 * c * FFN_COLS:2 * (c + 1) * FFN_COLS])
        gate, up = gu[:, 0:FFN_COLS], gu[:, FFN_COLS:2 * FFN_COLS]
        a_scr[:, sl] = (_silu(gate) * up).astype(BF16)
        yield 2 * _matmul_cost(rows, D_MODEL, FFN_COLS)
    for c in range(D_MODEL // OUT_COLS):
        sl = slice(c * OUT_COLS, (c + 1) * OUT_COLS)
        out_ref[:, sl] = x_ref[:, sl] + 0.5 * _dot(a_scr[...], wd_ref[:, sl])
        yield _matmul_cost(rows, FFN_DIM, OUT_COLS)


def _out_ffn_steps(l, x1_ref, mix_ref, x3_ref, g_ref, wout_b, wgu_b, wd_b, x2, xn, a):
    rows = x1_ref.shape[0]
    mix = mix_ref[...].astype(BF16)
    for c in range(D_MODEL // OUT_COLS):
        sl = slice(c * OUT_COLS, (c + 1) * OUT_COLS)
        x2[:, sl] = x1_ref[:, sl] + _dot(mix, wout_b[:, sl])
        yield _matmul_cost(rows, MIX_DIM, OUT_COLS)
    yield from _ffn_steps(x2, g_ref[l:l + 1, :], wgu_b, wd_b, x3_ref, xn, a)


def _out_ffn_cost(rows):
    return _matmul_cost(rows, MIX_DIM * D_MODEL + 3 * D_MODEL * FFN_DIM, 1)


def _ffn_in_steps(l, x_ref, x1_ref, z_ref, g1_ref, g2_ref, wgu_b, wd_b, win_b, xn, a):
    rows = x_ref.shape[0]
    yield from _ffn_steps(x_ref, g1_ref[l:l + 1, :], wgu_b, wd_b, x1_ref, xn, a)
    xn[...] = _rms(x1_ref[...], g2_ref[l:l + 1, :]).astype(BF16)
    for c in range(IN_DIM // OUT_COLS):
        sl = slice(c * OUT_COLS, (c + 1) * OUT_COLS)
        z_ref[:, sl] = _dot(xn[...], win_b[:, sl])
        yield _matmul_cost(rows, D_MODEL, OUT_COLS)


def _ffn_in_cost(rows):
    return _matmul_cost(rows, 3 * D_MODEL * FFN_DIM + D_MODEL * IN_DIM, 1)


def _ffn_in_kernel(l, n_tiles, tps, xp_ref, xs_ref, g1_ref, g2_ref, cw_ref, cb_ref, lng_ref, lnb_ref,
                   pw_ref, wg_hbm, wu_hbm, wd_hbm, win_hbm,
                   x1p_ref, zp_ref, mixc_ref, conv_ref, x1s_ref, zs_ref,
                   wgu_b, wd_b, win_b, stage, sem, xn_scr, a_scr, zcg, ubuf, ush, pw_b, cwb):
    i = pl.program_id(0)
    cprev = lax.rem(i + tps - 1, tps)
    dense_w = (g1_ref, g2_ref, wgu_b, wd_b, win_b)

    @pl.when(i == 0)
    def _():
        _stage_weight(wg_hbm, l, wgu_b, stage, sem, gate_up_half=0)
        _stage_weight(wu_hbm, l, wgu_b, stage, sem, gate_up_half=1)
        _stage_weight(wd_hbm, l, wd_b, stage, sem)
        _stage_weight(win_hbm, l, win_b, stage, sem, permute_q_cols=True)
        pw_b[...] = pw_ref[l].astype(BF16)
        for k in range(CONV_K):
            cwb[k] = jnp.broadcast_to(cw_ref[l, k:k + 1, :], (8, CONV_CH))

    @pl.when((i > 0) & (cprev == 0))
    def _():
        ubuf[0:CONV_PAD, :] = jnp.zeros((CONV_PAD, CONV_CH), F32)

    @pl.when((i > 0) & (cprev > 0))
    def _():
        ubuf[0:CONV_PAD, :] = ubuf[TOKEN_TILE:TOKEN_TILE + CONV_PAD, :]

    def dense():
        return _ffn_in_steps(l, xp_ref, x1p_ref, zp_ref, *dense_w, xn_scr, a_scr)

    def conv():
        return _conv_tile_steps(l, zcg, mixc_ref, cwb, cb_ref, lng_ref, lnb_ref, ubuf, ush, pw_b)

    @pl.when(i == 0)
    def _():
        _run(dense())

    @pl.when((i > 0) & (i < n_tiles))
    def _():
        _interleave(dense(), conv(), _ffn_in_cost(TOKEN_TILE), CONV_TILE_COST)

    @pl.when(i == n_tiles)
    def _():
        _run(conv())
        ns = xs_ref.shape[0]
        _run(_ffn_in_steps(l, xs_ref, x1s_ref, zs_ref, *dense_w, xn_scr.at[0:ns], a_scr.at[0:ns]))

    @pl.when(i < n_tiles)
    def _():
        zcg[...] = zp_ref[:, C_CA:C_CA + 2 * CONV_CH]

    @pl.when((i > 0) & (cprev == tps - 1))
    def _():
        conv_ref[0] = ubuf[CONV_PAD + TOKEN_TILE - (CONV_K - 1):CONV_PAD + TOKEN_TILE, :]


def _stage_scratch():
    return [pltpu.VMEM((STAGE_SLOTS, STAGE_ROWS, FFN_DIM), F32),
            pltpu.SemaphoreType.DMA((STAGE_SLOTS,))]


def _ffn_in(l, nb, xp, xs, g1, g2, cw, cb, lng, lnb, pw, wg, wu, wd, win):
    rows, ns = xp.shape[0], xs.shape[0]
    n_tiles = rows // TOKEN_TILE
    tps = n_tiles // nb
    cur = lambda w: pl.BlockSpec((TOKEN_TILE, w), lambda i: (jnp.minimum(i, n_tiles - 1), 0))
    prev = lambda w: pl.BlockSpec((TOKEN_TILE, w), lambda i: (jnp.maximum(i - 1, 0), 0))
    once = lambda w: pl.BlockSpec((ns, w), lambda i: (0, 0), pipeline_mode=pl.Buffered(1))
    whole = lambda a: pl.BlockSpec(a.shape, lambda i: (0,) * a.ndim)
    hbm = pl.BlockSpec(memory_space=pl.ANY)
    return pl.pallas_call(
        functools.partial(_ffn_in_kernel, l, n_tiles, tps),
        grid=(n_tiles + 1,),
        in_specs=[cur(D_MODEL), once(D_MODEL), whole(g1), whole(g2), whole(cw), whole(cb),
                  whole(lng), whole(lnb), whole(pw), hbm, hbm, hbm, hbm],
        out_specs=[cur(D_MODEL), cur(IN_DIM), prev(CONV_CH),
                   pl.BlockSpec((1, CONV_K - 1, CONV_CH),
                                lambda i: (jnp.maximum(i - 1, 0) // tps, 0, 0)),
                   once(D_MODEL), once(IN_DIM)],
        out_shape=[jax.ShapeDtypeStruct((rows, D_MODEL), F32),
                   jax.ShapeDtypeStruct((rows, IN_DIM), F32),
                   jax.ShapeDtypeStruct((rows, CONV_CH), BF16),
                   jax.ShapeDtypeStruct((nb, CONV_K - 1, CONV_CH), F32),
                   jax.ShapeDtypeStruct((ns, D_MODEL), F32),
                   jax.ShapeDtypeStruct((ns, IN_DIM), F32)],
        scratch_shapes=[pltpu.VMEM((D_MODEL, 2 * FFN_DIM), BF16),
                        pltpu.VMEM((FFN_DIM, D_MODEL), BF16), pltpu.VMEM((D_MODEL, IN_DIM), BF16),
                        *_stage_scratch(),
                        pltpu.VMEM((TOKEN_TILE, D_MODEL), BF16),
                        pltpu.VMEM((TOKEN_TILE, FFN_DIM), BF16),
                        pltpu.VMEM((TOKEN_TILE, 2 * CONV_CH), F32),
                        pltpu.VMEM((CONV_PAD + TOKEN_TILE, CONV_CH), F32),
                        pltpu.VMEM((2, 7, CONV_SPAN, CONV_CH), F32),
                        pltpu.VMEM((CONV_CH, CONV_CH), BF16),
                        pltpu.VMEM((CONV_K, 8, CONV_CH), F32)],
        compiler_params=pltpu.CompilerParams(dimension_semantics=("arbitrary",),
                                             vmem_limit_bytes=V7X_VMEM_LIMIT),
        name="ffn_in",
    )(xp, xs, g1, g2, cw, cb, lng, lnb, pw, wg, wu, wd, win)


def _mix_constants(dmask, qdec, kdec, abias):
    T = RET_CHUNK
    row_f = lax.broadcasted_iota(jnp.int32, (T, T), 0).astype(F32)
    col_f = lax.broadcasted_iota(jnp.int32, (T, T), 1).astype(F32)
    low = lax.broadcasted_iota(jnp.int32, (T, T), 1) < 64
    for h in range(RET_HEADS):
        lg = LOG_DECAY[h]
        diff = row_f - col_f
        dmask[h] = jnp.where(diff >= 0, jnp.exp(lg * jnp.maximum(diff, 0.0)), 0.0)
        qdec[h] = jnp.exp(lg * (row_f + 1.0))
    for p in range(RET_HEADS // 2):
        lgp = jnp.where(low, LOG_DECAY[2 * p], LOG_DECAY[2 * p + 1])
        kdec[p] = jnp.exp(lgp * (T - 1.0 - row_f))
    qpos = lax.broadcasted_iota(jnp.int32, (T, 2 * T), 0) + T
    kpos = lax.broadcasted_iota(jnp.int32, (T, 2 * T), 1)
    dist = (qpos - kpos).astype(F32)
    for h in range(ATT_HEADS):
        abias[h] = jnp.where((dist >= 0) & (dist < WINDOW), -SLOPES[h] * dist, NEG_INF)


def _mix_tile_steps(l, first, z_ref, mix_ref, sinks_ref, rng_ref, qg_ref, kg_ref,
                    s_scr, kbuf, vbuf, qbuf, klast, dmask, qdec, kdec, abias):
    T = RET_CHUNK
    NCH = TOKEN_TILE // T
    row_i = lax.broadcasted_iota(jnp.int32, (T, T), 0)
    low = lax.broadcasted_iota(jnp.int32, (T, T), 1) < 64

    def zc(r0, nrows, a, w):
        return z_ref[r0:r0 + nrows, a:a + w]

    def seg_mean_sq(x):
        low_x = lax.broadcasted_iota(jnp.int32, x.shape, 1) < 64
        xx = x * x
        s_lo = jnp.sum(jnp.where(low_x, xx, 0.0), axis=-1, keepdims=True)
        s_hi = jnp.sum(jnp.where(low_x, 0.0, xx), axis=-1, keepdims=True)
        return jnp.where(low_x, s_lo, s_hi) * (1.0 / HEAD_DIM)

    def retention_unit(p, j):
        s_cur = s_scr[p]
        r0 = T * j
        qp = zc(r0, T, C_RQ + 128 * p, 128)
        kp = zc(r0, T, C_RK + 128 * p, 128) * (RET_DK ** -0.5)
        vb = zc(r0, T, C_RV + 256 * p, 256).astype(BF16)
        kb = kp.astype(BF16)
        sb = s_cur.astype(BF16)
        qm = jnp.concatenate([jnp.where(low, qp, 0.0), jnp.where(low, 0.0, qp)], axis=0).astype(BF16)
        a = _dot_nt(qm, kb)
        qs = _dot(qm, sb)
        upd = _dot_tn((kp * kdec[p]).astype(BF16), vb)
        yield RET_COST // 4
        ab = (a * dmask[2 * p:2 * p + 2].reshape(2 * T, T)).astype(BF16)
        yield RET_COST // 4
        o = _dot(ab, vb)
        yield RET_COST // 4
        for e in range(2):
            h = 2 * p + e
            oh = _rms(o[T * e:T * e + T, 128 * e:128 * e + 128] + qs[T * e:T * e + T, :] * qdec[h],
                      rng_ref[l, h:h + 1, :])
            mix_ref[r0:r0 + T, M_RET + 128 * h:M_RET + 128 * h + 128] = (
                oh * _silu(zc(r0, T, C_RG + 128 * h, 128))).astype(BF16)
        top = row_i < 64
        new = jnp.where(top, upd[:, 0:128], upd[:, 128:256])
        cd = jnp.where(top, math.exp(LOG_DECAY[2 * p] * T), math.exp(LOG_DECAY[2 * p + 1] * T))
        s_scr[p] = s_cur * cd + new
        yield RET_COST // 4

    def attention_prep():
        ak = zc(0, TOKEN_TILE, C_AK, 128)
        aq = [zc(0, TOKEN_TILE, C_AQ + 128 * t, 128) for t in range(2)]
        ms_k = seg_mean_sq(ak)
        ms_q = [seg_mean_sq(aq[t]) for t in range(2)]
        yield ATT_PREP_COST // 2
        kn = ak * lax.rsqrt(ms_k + EPS) * kg_ref[l:l + 1, :]
        klast[...] = kn[TOKEN_TILE - T:TOKEN_TILE, :]
        kbuf[T:T + TOKEN_TILE, :] = kn.astype(BF16)
        vbuf[T:T + TOKEN_TILE, :] = zc(0, TOKEN_TILE, C_AV, 128).astype(BF16)
        for t in range(2):
            qt = aq[t] * lax.rsqrt(ms_q[t] + EPS) * qg_ref[l:l + 1, :] * (HEAD_DIM ** -0.5)
            for j in range(NCH):
                qj = qt[T * j:T * j + T, :]
                qbuf[j, (2 * t) * T:(2 * t + 1) * T, :] = jnp.where(low, qj, 0.0).astype(BF16)
                qbuf[j, (2 * t + 1) * T:(2 * t + 2) * T, :] = jnp.where(low, 0.0, qj).astype(BF16)
        yield ATT_PREP_COST // 2

    def attention_unit(j):
        r0 = T * j
        kcat = kbuf[r0:r0 + 2 * T, :]
        vcat = vbuf[r0:r0 + 2 * T, :]
        s = _dot_nt(qbuf[j], kcat)
        yield ATT_COST // 2
        probs = []
        for t in range(2):
            for kvh in range(KV_HEADS):
                head = 2 * kvh + t
                sink = sinks_ref[l, head]
                bias = abias[head]
                if j == 0:
                    no_prev = lax.broadcasted_iota(jnp.int32, (T, 2 * T), 1) < jnp.where(first, T, 0)
                    bias = jnp.where(no_prev, NEG_INF, bias)
                blk = 2 * t + kvh
                sb = s[T * blk:T * blk + T, :] + bias
                m = jnp.maximum(jnp.max(sb, axis=-1, keepdims=True), sink)
                e = jnp.exp(sb - m)
                inv = 1.0 / (jnp.sum(e, axis=-1, keepdims=True) + jnp.exp(sink - m))
                probs.append((e * inv).astype(BF16))
            yield ATT_COST
        o = _dot(jnp.concatenate(probs, axis=0), vcat)
        for t in range(2):
            mix_ref[r0:r0 + T, M_ATT + 128 * t:M_ATT + 128 * t + 128] = jnp.where(
                low, o[T * 2 * t:T * (2 * t + 1), :], o[T * (2 * t + 1):T * (2 * t + 2), :]
            ).astype(BF16)
        yield ATT_COST // 2

    def ret_lane():
        for j in range(NCH):
            for p in range(RET_HEADS // 2):
                yield from retention_unit(p, j)

    def att_lane():
        yield from attention_prep()
        for j in range(NCH):
            yield from attention_unit(j)

    live = [att_lane(), ret_lane()]
    while live:
        for lane in list(live):
            cost = next(lane, None)
            if cost is None:
                live.remove(lane)
            else:
                yield cost


def _conv_tile_steps(l, zcg_ref, out_ref, cwb, cb_ref, lng_ref, lnb_ref, ubuf, ush, pw_b):
    ubuf[CONV_PAD:CONV_PAD + TOKEN_TILE, :] = (
        zcg_ref[:, 0:CONV_CH] * jax.nn.sigmoid(zcg_ref[:, CONV_CH:2 * CONV_CH]))
    yield CONV_PREP_COST
    for rb in range(TOKEN_TILE // CONV_ROWS):
        base = rb * CONV_ROWS
        sh = ush.at[rb % 2]
        for r in range(1, 8):
            sh[r - 1] = ubuf[base + r:base + r + CONV_SPAN, :]
        acc = None
        for k in range(CONV_K):
            off = CONV_PAD - (CONV_K - 1) + k
            a8, r = off // 8 * 8, off % 8
            src = (ubuf[base + a8:base + a8 + CONV_ROWS, :] if r == 0
                   else sh[r - 1, a8:a8 + CONV_ROWS, :])
            term = cwb[k][None] * src.reshape(CONV_ROWS // 8, 8, CONV_CH)
            acc = term if acc is None else acc + term
        y = _silu(_layer_norm(acc.reshape(CONV_ROWS, CONV_CH) + cb_ref[l:l + 1, :],
                              lng_ref[l:l + 1, :], lnb_ref[l:l + 1, :])).astype(BF16)
        yield CONV_COST - CONV_COST // 8
        out_ref[base:base + CONV_ROWS, :] = _dot(y, pw_b[...]).astype(BF16)
        yield CONV_COST // 8


RET_COST, ATT_PREP_COST, ATT_COST, CONV_PREP_COST, CONV_COST = 160, 250, 220, 200, 410
MIX_COST = ((RET_HEADS // 2) * (TOKEN_TILE // RET_CHUNK) * RET_COST + ATT_PREP_COST
            + 3 * (TOKEN_TILE // RET_CHUNK) * ATT_COST)
CONV_TILE_COST = CONV_PREP_COST + (TOKEN_TILE // CONV_ROWS) * CONV_COST


def _mix_out_ffn_kernel(l, n_tiles, tps, sinks_ref, z_ref, x1p_ref, mixc_ref, x1s_ref, mixs_ref, g_ref,
                        rng_ref, qg_ref, kg_ref, wout_hbm, wg_hbm, wu_hbm, wd_hbm,
                        x3p_ref, x3s_ref, ret_ref, kwin_ref, vwin_ref,
                        wout_b, wgu_b, wd_b, stage, sem, x2_scr, xn_scr, a_scr, mix_new, mix_old,
                        s_scr, kbuf, vbuf, qbuf, klast, dmask, qdec, kdec, abias):
    i = pl.program_id(0)
    c = lax.rem(i, tps)
    T = RET_CHUNK
    mix_args = (sinks_ref, rng_ref, qg_ref, kg_ref,
                s_scr, kbuf, vbuf, qbuf, klast, dmask, qdec, kdec, abias)
    dense_w = (g_ref, wout_b, wgu_b, wd_b)

    @pl.when(i == 0)
    def _():
        _stage_w_out(wout_hbm, l, wout_b, stage, sem)
        _stage_weight(wg_hbm, l, wgu_b, stage, sem, gate_up_half=0)
        _stage_weight(wu_hbm, l, wgu_b, stage, sem, gate_up_half=1)
        _stage_weight(wd_hbm, l, wd_b, stage, sem)
        _mix_constants(dmask, qdec, kdec, abias)

    @pl.when(c == 0)
    def _():
        s_scr[...] = jnp.zeros_like(s_scr)
        kbuf[0:T, :] = jnp.zeros((T, 128), BF16)
        vbuf[0:T, :] = jnp.zeros((T, 128), BF16)

    @pl.when(c > 0)
    def _():
        kbuf[0:T, :] = kbuf[TOKEN_TILE:TOKEN_TILE + T, :]
        vbuf[0:T, :] = vbuf[TOKEN_TILE:TOKEN_TILE + T, :]

    @pl.when(i > 0)
    def _():
        mix_old[:, 0:M_CONV] = mix_new[:, 0:M_CONV]
        mix_old[:, M_CONV:MIX_DIM] = mixc_ref[...]

    def mixers():
        return _mix_tile_steps(l, c == 0, z_ref, mix_new, *mix_args)

    def dense():
        return _out_ffn_steps(l, x1p_ref, mix_old, x3p_ref, *dense_w, x2_scr, xn_scr, a_scr)

    @pl.when(i == 0)
    def _():
        _run(mixers())

    @pl.when((i > 0) & (i < n_tiles))
    def _():
        _interleave(dense(), mixers(), _out_ffn_cost(TOKEN_TILE), MIX_COST)

    @pl.when(i == n_tiles)
    def _():
        _run(dense())
        ns = x1s_ref.shape[0]
        _run(_out_ffn_steps(l, x1s_ref, mixs_ref, x3s_ref, *dense_w,
                            x2_scr.at[0:ns], xn_scr.at[0:ns], a_scr.at[0:ns]))

    @pl.when((c == tps - 1) & (i < n_tiles))
    def _():
        ret_ref[0] = s_scr[...]
        kwin_ref[0] = klast[...].T
        vwin_ref[0] = z_ref[TOKEN_TILE - T:TOKEN_TILE, C_AV:C_AV + 128].T


def _mix_out_ffn(l, nb, z, x1p, mixc, x1s, mixs, g, sinks, rng, qg2, kg2, wout, wg, wu, wd):
    rows, ns = x1p.shape[0], x1s.shape[0]
    n_tiles = rows // TOKEN_TILE
    tps = n_tiles // nb
    T = RET_CHUNK
    cur = lambda w: pl.BlockSpec((TOKEN_TILE, w), lambda i: (jnp.minimum(i, n_tiles - 1), 0))
    prev = lambda w: pl.BlockSpec((TOKEN_TILE, w), lambda i: (jnp.maximum(i - 1, 0), 0))
    once = lambda w: pl.BlockSpec((ns, w), lambda i: (0, 0), pipeline_mode=pl.Buffered(1))
    whole = lambda a: pl.BlockSpec(a.shape, lambda i: (0,) * a.ndim)
    per_seq = lambda shape: pl.BlockSpec(
        (1,) + shape, lambda i: (jnp.minimum(i, n_tiles - 1) // tps,) + (0,) * len(shape))
    hbm = pl.BlockSpec(memory_space=pl.ANY)
    return pl.pallas_call(
        functools.partial(_mix_out_ffn_kernel, l, n_tiles, tps),
        grid=(n_tiles + 1,),
        in_specs=[pl.BlockSpec(memory_space=pltpu.SMEM), cur(IN_DIM), prev(D_MODEL), prev(CONV_CH),
                  once(D_MODEL), once(MIX_DIM), whole(g),
                  whole(rng), whole(qg2), whole(kg2), hbm, hbm, hbm, hbm],
        out_specs=[prev(D_MODEL), once(D_MODEL),
                   per_seq((2, 128, RET_DV)), per_seq((128, WINDOW)), per_seq((128, WINDOW))],
        out_shape=[jax.ShapeDtypeStruct((rows, D_MODEL), F32),
                   jax.ShapeDtypeStruct((ns, D_MODEL), F32),
                   jax.ShapeDtypeStruct((nb, 2, 128, RET_DV), F32),
                   jax.ShapeDtypeStruct((nb, 128, WINDOW), F32),
                   jax.ShapeDtypeStruct((nb, 128, WINDOW), F32)],
        scratch_shapes=[pltpu.VMEM((MIX_DIM, D_MODEL), BF16),
                        pltpu.VMEM((D_MODEL, 2 * FFN_DIM), BF16),
                        pltpu.VMEM((FFN_DIM, D_MODEL), BF16),
                        *_stage_scratch(),
                        pltpu.VMEM((TOKEN_TILE, D_MODEL), F32),
                        pltpu.VMEM((TOKEN_TILE, D_MODEL), BF16),
                        pltpu.VMEM((TOKEN_TILE, FFN_DIM), BF16),
                        pltpu.VMEM((TOKEN_TILE, MIX_DIM), BF16),
                        pltpu.VMEM((TOKEN_TILE, MIX_DIM), BF16),
                        pltpu.VMEM((2, 128, RET_DV), F32),
                        pltpu.VMEM((T + TOKEN_TILE, 128), BF16),
                        pltpu.VMEM((T + TOKEN_TILE, 128), BF16),
                        pltpu.VMEM((TOKEN_TILE // T, 2 * KV_HEADS * T, 128), BF16),
                        pltpu.VMEM((T, 128), F32),
                        pltpu.VMEM((RET_HEADS, T, T), F32), pltpu.VMEM((RET_HEADS, T, T), F32),
                        pltpu.VMEM((2, T, T), F32), pltpu.VMEM((ATT_HEADS, T, 2 * T), F32)],
        compiler_params=pltpu.CompilerParams(dimension_semantics=("arbitrary",),
                                             vmem_limit_bytes=V7X_VMEM_LIMIT),
        name="mix_out_ffn",
    )(sinks, z, x1p, mixc, x1s, mixs, g, rng, qg2, kg2, wout, wg, wu, wd)


def _mix_sample_kernel(l, sinks_ref, z_ref, col_ref, s_ref, ck_ref, cv_ref, sc_ref,
                       rng_ref, qg_ref, kg_ref, kgc_ref, cw_ref, cb_ref, lng_ref, lnb_ref, pw_ref,
                       *rest):
    mix_ref, so_ref, cko_ref, cvo_ref, sco_ref, o_scr, oa_scr = rest[-7:]
    NB = SAMPLE_BLOCK
    P = WINDOW

    def put(ref, idx, val):
        for slab in range(ref.shape[0]):
            ref[(slab,) + idx] = val

    for bl in range(NB):
        for h in range(RET_HEADS):
            gamma = math.exp(LOG_DECAY[h])
            r0 = RET_DK * h
            S = s_ref[0, bl, h]
            qc = col_ref[0, r0:r0 + RET_DK, bl:bl + 1]
            kc = col_ref[0, 256 + r0:256 + r0 + RET_DK, bl:bl + 1] * (RET_DK ** -0.5)
            v = z_ref[bl:bl + 1, C_RV + 128 * h:C_RV + 128 * h + 128]
            qk = jnp.sum(qc * kc, axis=0, keepdims=True)
            o_scr[bl:bl + 1, 128 * h:128 * h + 128] = (
                gamma * jnp.sum(qc * S, axis=0, keepdims=True) + qk * v)
            put(so_ref, (bl, h), gamma * S + kc * v)
    for h in range(RET_HEADS):
        o = _rms(o_scr[:, 128 * h:128 * h + 128], rng_ref[l, h:h + 1, :])
        mix_ref[:, M_RET + 128 * h:M_RET + 128 * h + 128] = (
            o * _silu(z_ref[:, C_RG + 128 * h:C_RG + 128 * h + 128]))

    q_tiles = [_seg_rms(z_ref[:, C_AQ + 128 * t:C_AQ + 128 * t + 128], qg_ref[l:l + 1, :])
               * (HEAD_DIM ** -0.5) for t in range(2)]
    kn = _seg_rms(z_ref[:, C_AK:C_AK + 128], kg_ref[l:l + 1, :])
    vn = z_ref[:, C_AV:C_AV + 128]
    akc = col_ref[0, 512:640, :]
    vnc = col_ref[0, 640:768, :]
    ms = jnp.concatenate(
        [jnp.broadcast_to(jnp.mean(akc[64 * s:64 * s + 64, :] ** 2, axis=0, keepdims=True), (64, NB))
         for s in range(KV_HEADS)], axis=0)
    knc = akc * lax.rsqrt(ms + EPS) * kgc_ref[l]
    rid = lax.broadcasted_iota(jnp.int32, (8, 128), 0)
    lane = lax.broadcasted_iota(jnp.int32, (8, 128), 1)
    sel = ((rid % 2 == 0) == (lane < 64)) & (rid < ATT_HEADS)
    rcol = lax.broadcasted_iota(jnp.int32, (8, 1), 0)
    heads = [2 * (r % 2) + r // 2 for r in range(ATT_HEADS)]
    slope_col = jnp.zeros((8, 1), F32)
    sink_col = jnp.zeros((8, 1), F32)
    for r, hd in enumerate(heads):
        slope_col = jnp.where(rcol == r, SLOPES[hd], slope_col)
        sink_col = jnp.where(rcol == r, sinks_ref[l, hd], sink_col)
    dist = float(P) - lane.astype(F32)
    key_ok = (dist < float(WINDOW)) & (lane + (PAST_LEN - P) >= 0)
    newest = lax.broadcasted_iota(jnp.int32, (128, P), 1) == P - 1
    for bl in range(NB):
        KT = ck_ref[0, bl].reshape(128, P)
        VT = cv_ref[0, bl].reshape(128, P)
        qrows = jnp.where(rid < 2, q_tiles[0][bl:bl + 1, :], q_tiles[1][bl:bl + 1, :])
        q4 = jnp.where(sel, qrows, 0.0)
        s = _dot(q4.astype(BF16), KT.astype(BF16))
        s = jnp.where(key_ok, s - slope_col * dist, NEG_INF)
        s_new = jnp.sum(q4 * kn[bl:bl + 1, :], axis=-1, keepdims=True)
        m = jnp.maximum(jnp.maximum(jnp.max(s, axis=-1, keepdims=True), s_new), sink_col)
        e = jnp.exp(s - m)
        e_new = jnp.exp(s_new - m)
        inv = 1.0 / (jnp.sum(e, axis=-1, keepdims=True) + e_new + jnp.exp(sink_col - m))
        o = _dot_nt((e * inv).astype(BF16), VT.astype(BF16)) + (e_new * inv) * vn[bl:bl + 1, :]
        for t in range(2):
            oa_scr[bl:bl + 1, 128 * t:128 * t + 128] = jnp.where(
                lane[0:1, :] < 64, o[2 * t:2 * t + 1, :], o[2 * t + 1:2 * t + 2, :])
        put(cko_ref, (bl,), jnp.where(newest, knc[:, bl:bl + 1], pltpu.roll(KT, P - 1, 1)).reshape(
            KV_HEADS, HEAD_DIM, P))
        put(cvo_ref, (bl,), jnp.where(newest, vnc[:, bl:bl + 1], pltpu.roll(VT, P - 1, 1)).reshape(
            KV_HEADS, HEAD_DIM, P))
    mix_ref[:, M_ATT:M_ATT + 256] = oa_scr[...]

    u = z_ref[:, C_CA:C_CA + CONV_CH] * jax.nn.sigmoid(z_ref[:, C_CG:C_CG + CONV_CH])
    KT1 = CONV_K - 1
    y = cw_ref[l, KT1:KT1 + 1, :] * u
    for k in range(KT1):
        y = y + cw_ref[l, k:k + 1, :] * sc_ref[0, k]
        if k > 0:
            put(sco_ref, (k - 1,), sc_ref[0, k])
    put(sco_ref, (KT1 - 1,), u)
    y = _silu(_layer_norm(y + cb_ref[l:l + 1, :], lng_ref[l:l + 1, :], lnb_ref[l:l + 1, :]))
    mix_ref[:, M_CONV:M_CONV + CONV_CH] = _dot(y.astype(BF16), pw_ref[l].astype(BF16))


def _mix_sample(l, z, cols, s, ckt, cvt, sct, sinks, rng, qg2, kg2, kgc, cw, cb, lng, lnb, pw, prev):
    ns = z.shape[0]
    NB = SAMPLE_BLOCK
    whole = lambda a: pl.BlockSpec(a.shape, lambda i: (0,) * a.ndim)

    def specs(slabs, at):
        return [pl.BlockSpec((slabs, NB, RET_HEADS, RET_DK, RET_DV), lambda i: (at, i, 0, 0, 0)),
                pl.BlockSpec((slabs, NB, KV_HEADS, HEAD_DIM, WINDOW), lambda i: (at, i, 0, 0, 0)),
                pl.BlockSpec((slabs, NB, KV_HEADS, HEAD_DIM, WINDOW), lambda i: (at, i, 0, 0, 0)),
                pl.BlockSpec((slabs, CONV_K - 1, NB, CONV_CH), lambda i: (at, 0, i, 0))]

    state_specs = specs(1, l)
    out_state_specs = specs(s.shape[0], 0) if prev is None else state_specs
    in_specs = [pl.BlockSpec(memory_space=pltpu.SMEM),
                pl.BlockSpec((NB, IN_DIM), lambda i: (i, 0)),
                pl.BlockSpec((1, 768, NB), lambda i: (i, 0, 0)),
                *state_specs,
                whole(rng), whole(qg2), whole(kg2), whole(kgc), whole(cw), whole(cb), whole(lng),
                whole(lnb), whole(pw)]
    args = [sinks, z, cols, s, ckt, cvt, sct, rng, qg2, kg2, kgc, cw, cb, lng, lnb, pw]
    aliases = {}
    if prev is not None:
        for k, a in enumerate(prev):
            aliases[len(args)] = 1 + k
            in_specs.append(pl.BlockSpec(memory_space=pl.ANY))
            args.append(a)
    return pl.pallas_call(
        functools.partial(_mix_sample_kernel, l),
        grid=(ns // NB,),
        in_specs=in_specs,
        out_specs=[pl.BlockSpec((NB, MIX_DIM), lambda i: (i, 0)), *out_state_specs],
        out_shape=[jax.ShapeDtypeStruct((ns, MIX_DIM), F32),
                   jax.ShapeDtypeStruct(s.shape, F32), jax.ShapeDtypeStruct(ckt.shape, F32),
                   jax.ShapeDtypeStruct(cvt.shape, F32), jax.ShapeDtypeStruct(sct.shape, F32)],
        scratch_shapes=[pltpu.VMEM((NB, 512), F32), pltpu.VMEM((NB, 256), F32)],
        input_output_aliases=aliases,
        compiler_params=pltpu.CompilerParams(dimension_semantics=("arbitrary",)),
        name="mix_sample",
    )(*args)


def kernel(x_prompt, x_sample, state_ret, cache_k_win, cache_v_win, state_conv, ffn1_norm, ffn1_wg, ffn1_wu, ffn1_wd, mix_norm, w_in, ret_norm_g, q_norm_g, k_norm_g, sinks, conv_w, conv_b, conv_ln_g, conv_ln_b, conv_pw, w_out, ffn2_norm, ffn2_wg, ffn2_wu, ffn2_wd):
    nb, seq, _ = x_prompt.shape
    ns = x_sample.shape[0]
    assert x_sample.shape[1] == 1 and seq % TOKEN_TILE == 0
    assert ns % SAMPLE_BLOCK == 0 and ns <= TOKEN_TILE and cache_k_win.shape[2] == WINDOW
    hp = x_prompt.reshape(nb * seq, D_MODEL)
    hs = x_sample.reshape(ns, D_MODEL)
    ckt = cache_k_win.transpose(0, 1, 3, 4, 2)
    cvt = cache_v_win.transpose(0, 1, 3, 4, 2)
    sct = state_conv.transpose(0, 2, 1, 3)
    qg2 = jnp.tile(q_norm_g, (1, 2))
    kg2 = jnp.tile(k_norm_g, (1, 2))
    kgc = kg2.reshape(DEPTH, 128, 1)
    conv_params = (conv_w, conv_b, conv_ln_g, conv_ln_b, conv_pw)
    prompt_states = [[] for _ in range(4)]
    sample_states = None
    for l in range(DEPTH):
        hp1, zp, mixc, c1, hs1, zs = _ffn_in(l, nb, hp, hs, ffn1_norm, mix_norm, *conv_params,
                                             ffn1_wg, ffn1_wu, ffn1_wd, w_in)
        cols = jnp.concatenate([zs[:, C_RQ:C_RQ + 512], zs[:, C_AK:C_AK + 256]], axis=1)
        cols = cols.T.reshape(768, ns // SAMPLE_BLOCK, SAMPLE_BLOCK).transpose(1, 0, 2)
        mixs, *sample_states = _mix_sample(l, zs, cols, state_ret, ckt, cvt, sct, sinks,
                                           ret_norm_g, qg2, kg2, kgc, conv_w, conv_b, conv_ln_g,
                                           conv_ln_b, conv_pw, sample_states)
        hp, hs, r1, k1, v1 = _mix_out_ffn(l, nb, zp, hp1, mixc, hs1, mixs, ffn2_norm, sinks,
                                          ret_norm_g, qg2, kg2, w_out, ffn2_wg, ffn2_wu, ffn2_wd)
        for lst, val in zip(prompt_states, (r1, k1, v1, c1)):
            lst.append(val)
    ret_p, kwin_p, vwin_p, conv_p = (jnp.stack(s) for s in prompt_states)
    ret_s, kwin_s, vwin_s, conv_s = sample_states
    return (hp.reshape(nb, seq, D_MODEL), hs.reshape(ns, 1, D_MODEL),
            ret_p.reshape(DEPTH, nb, RET_HEADS, RET_DK, RET_DV), ret_s,
            kwin_p.reshape(DEPTH, nb, KV_HEADS, HEAD_DIM, WINDOW).transpose(0, 1, 4, 2, 3),
            kwin_s.transpose(0, 1, 4, 2, 3),
            vwin_p.reshape(DEPTH, nb, KV_HEADS, HEAD_DIM, WINDOW).transpose(0, 1, 4, 2, 3),
            vwin_s.transpose(0, 1, 4, 2, 3),
            conv_p, conv_s.transpose(0, 2, 1, 3))
```
